```python
import jax, jax.numpy as jnp
from jax import lax
import numpy as np

D_MODEL = 2048
BATCH = 1
SEQ = 8192
DEPTH = 1

CHUNK = 64
Q_BLOCK = 128
MLA_HEADS = 16
MLA_Q_LORA = 768
MLA_KV_LORA = 512
MLA_NOPE = 128
MLA_ROPE = 64
MLA_V = 128
ROPE_THETA = 10000.0
GDN_QK_HEADS = 16
GDN_V_HEADS = 32
GDN_DK = 128
GDN_DV = 128
GDN_CONV = 4
GDN_QKV = 2 * GDN_QK_HEADS * GDN_DK + GDN_V_HEADS * GDN_DV
MEM_TOKENS = 256
MEM_HEADS = 4
MEM_HEAD_DIM = 128
N_GROUPS = 8
EXPERTS_PER_GROUP = 8
N_EXPERTS = N_GROUPS * EXPERTS_PER_GROUP
TOP_K = 2
D_EXPERT = 512
MOE_BLOCK = 128
RMS_EPS = 1e-6
LN_EPS = 1e-5
DN_ALPHA = (2.0 * DEPTH) ** 0.25
DN_BETA = (8.0 * DEPTH) ** -0.25
MIX_SPLITS = [MLA_Q_LORA, MLA_KV_LORA + MLA_ROPE, GDN_QKV, GDN_V_HEADS * GDN_DV,
              GDN_V_HEADS, GDN_V_HEADS, 2 * D_MODEL]
D_IN = sum(MIX_SPLITS)

kernel_name = "hybrid_mla_gdn_hmoe_deepnorm"


def _rmsnorm(x, w):
    xf = x.astype(jnp.float32)
    y = xf * lax.rsqrt(jnp.mean(xf * xf, axis=-1, keepdims=True) + RMS_EPS)
    return (y * w.astype(jnp.float32)).astype(x.dtype)


def _layernorm(x, g, b):
    xf = x.astype(jnp.float32)
    mu = jnp.mean(xf, axis=-1, keepdims=True)
    var = jnp.mean(jnp.square(xf - mu), axis=-1, keepdims=True)
    y = (xf - mu) * lax.rsqrt(var + LN_EPS)
    return (y * g.astype(jnp.float32) + b.astype(jnp.float32)).astype(x.dtype)


def _l2norm(x):
    return x * lax.rsqrt(jnp.sum(x * x, axis=-1, keepdims=True) + 1e-6)


def _rope(t, positions):
    inv = 1.0 / (ROPE_THETA ** (jnp.arange(0, MLA_ROPE, 2, dtype=jnp.float32) / MLA_ROPE))
    ang = positions.astype(jnp.float32)[..., None] * inv
    cos = jnp.concatenate([jnp.cos(ang), jnp.cos(ang)], -1)[:, :, None, :]
    sin = jnp.concatenate([jnp.sin(ang), jnp.sin(ang)], -1)[:, :, None, :]
    tf = t.astype(jnp.float32)
    t1, t2 = jnp.split(tf, 2, axis=-1)
    rot = jnp.concatenate([-t2, t1], axis=-1)
    return (tf * cos + rot * sin).astype(t.dtype)


def _mla(q_down, kv_down, positions, q_norm, w_uq, kv_norm, w_ukv):
    B, S, _ = q_down.shape
    H = MLA_HEADS
    cq = _rmsnorm(q_down, q_norm)
    q = (cq @ w_uq).reshape(B, S, H, MLA_NOPE + MLA_ROPE)
    q_nope, q_pe = q[..., :MLA_NOPE], q[..., MLA_NOPE:]
    ckv = _rmsnorm(kv_down[..., :MLA_KV_LORA], kv_norm)
    k_pe = kv_down[..., MLA_KV_LORA:][:, :, None, :]
    kv = (ckv @ w_ukv).reshape(B, S, H, MLA_NOPE + MLA_V)
    k_nope, v = kv[..., :MLA_NOPE], kv[..., MLA_NOPE:]
    q_pe = _rope(q_pe, positions)
    k_pe = _rope(k_pe, positions)[:, :, 0, :]
    scale = (MLA_NOPE + MLA_ROPE) ** -0.5
    nq = S // Q_BLOCK
    qn_b = q_nope.reshape(B, nq, Q_BLOCK, H, MLA_NOPE).swapaxes(0, 1)
    qp_b = q_pe.reshape(B, nq, Q_BLOCK, H, MLA_ROPE).swapaxes(0, 1)
    starts = jnp.arange(nq, dtype=jnp.int32) * Q_BLOCK
    key_chunk = jnp.arange(S, dtype=jnp.int32) // CHUNK

    def attend(args):
        qn, qp, q0 = args
        s = (jnp.einsum('bqhd,bkhd->bhqk', qn, k_nope)
             + jnp.einsum('bqhr,bkr->bhqk', qp, k_pe)).astype(jnp.float32) * scale
        q_chunk = (q0 + jnp.arange(Q_BLOCK, dtype=jnp.int32)) // CHUNK
        mask = key_chunk[None, :] <= q_chunk[:, None]
        s = jnp.where(mask[None, None], s, jnp.finfo(jnp.float32).min)
        p = jax.nn.softmax(s, axis=-1).astype(v.dtype)
        return jnp.einsum('bhqk,bkhd->bqhd', p, v)

    o = lax.map(attend, (qn_b, qp_b, starts))
    return o.swapaxes(0, 1).reshape(B, S, H * MLA_V)


def _gated_delta_rule(q, k, v, g, beta):
    B, S, H, dk = q.shape
    dv = v.shape[-1]
    N = S // CHUNK
    q = q * dk ** -0.5

    def chunks(t):
        t = t.reshape((B, N, CHUNK, H) + t.shape[3:])
        return jnp.moveaxis(t, 3, 2)

    q, k, v, g, beta = chunks(q), chunks(k), chunks(v), chunks(g), chunks(beta)
    g = jnp.cumsum(g, axis=-1)
    kb = k * beta[..., None]
    vb = v * beta[..., None]
    tril = jnp.tril(jnp.ones((CHUNK, CHUNK), dtype=bool))
    strict = jnp.tril(jnp.ones((CHUNK, CHUNK), dtype=bool), -1)
    diff = g[..., :, None] - g[..., None, :]
    decay = jnp.where(tril, jnp.exp(jnp.where(tril, diff, 0.0)), 0.0)
    m = jnp.where(strict, jnp.einsum('bnhcd,bnhed->bnhce', kb, k) * decay, 0.0)
    eye = jnp.eye(CHUNK, dtype=jnp.float32)
    t_inv = lax.linalg.triangular_solve(m + eye, jnp.broadcast_to(eye, m.shape),
                                        left_side=True, lower=True, unit_diagonal=True)
    u = jnp.einsum('bnhcs,bnhse->bnhce', t_inv, vb)
    w = jnp.einsum('bnhcs,bnhsd->bnhcd', t_inv, kb * jnp.exp(g)[..., None])
    a_qk = jnp.where(tril, jnp.einsum('bnhcd,bnhed->bnhce', q, k) * decay, 0.0)

    def step(state, inp):
        qi, ki, ui, wi, gi, ai = inp
        v_new = ui - jnp.einsum('bhcd,bhde->bhce', wi, state)
        o = (jnp.einsum('bhcd,bhde->bhce', qi * jnp.exp(gi)[..., None], state)
             + jnp.einsum('bhcs,bhse->bhce', ai, v_new))
        g_last = gi[..., -1:]
        state = (state * jnp.exp(g_last)[..., None]
                 + jnp.einsum('bhcd,bhce->bhde', ki * jnp.exp(g_last - gi)[..., None], v_new))
        return state, o

    xs = tuple(jnp.moveaxis(t, 1, 0) for t in (q, k, u, w, g, a_qk))
    s0 = jnp.zeros((B, H, dk, dv), jnp.float32)
    _, o = lax.scan(step, s0, xs)
    o = jnp.moveaxis(o, 0, 1)
    return jnp.moveaxis(o, 2, 3).reshape(B, S, H, dv)


def _gdn(qkv, z, b, a, conv_w, a_log, dt_bias, norm_w):
    B, S, C = qkv.shape
    y = lax.conv_general_dilated(qkv, conv_w[:, None, :].astype(qkv.dtype), window_strides=(1,),
                                 padding=[(GDN_CONV - 1, 0)],
                                 dimension_numbers=('NWC', 'WIO', 'NWC'),
                                 feature_group_count=C)
    y = jax.nn.silu(y).astype(jnp.float32)
    nqk = GDN_QK_HEADS * GDN_DK
    q = _l2norm(y[..., :nqk].reshape(B, S, GDN_QK_HEADS, GDN_DK))
    k = _l2norm(y[..., nqk:2 * nqk].reshape(B, S, GDN_QK_HEADS, GDN_DK))
    v = y[..., 2 * nqk:].reshape(B, S, GDN_V_HEADS, GDN_DV)
    rep = GDN_V_HEADS // GDN_QK_HEADS
    q = jnp.repeat(q, rep, axis=2)
    k = jnp.repeat(k, rep, axis=2)
    beta = jax.nn.sigmoid(b.astype(jnp.float32))
    g = -jnp.exp(a_log.astype(jnp.float32)) * jax.nn.softplus(
        a.astype(jnp.float32) + dt_bias.astype(jnp.float32))
    o = _gated_delta_rule(q, k, v, g, beta)
    o = o * lax.rsqrt(jnp.mean(o * o, axis=-1, keepdims=True) + RMS_EPS) * norm_w.astype(jnp.float32)
    o = o * jax.nn.silu(z.astype(jnp.float32).reshape(B, S, GDN_V_HEADS, GDN_DV))
    return o.reshape(B, S, GDN_V_HEADS * GDN_DV).astype(qkv.dtype)


def _memory_attn(x, mem, w_mq, w_mkv, w_mo):
    B, S, _ = x.shape
    M = mem.shape[1]
    q = (x @ w_mq).reshape(B, S, MEM_HEADS, MEM_HEAD_DIM)
    kv = (mem @ w_mkv).reshape(B, M, 2, MEM_HEADS, MEM_HEAD_DIM)
    k, v = kv[:, :, 0], kv[:, :, 1]
    s = jnp.einsum('bqhd,bkhd->bhqk', q, k).astype(jnp.float32) * MEM_HEAD_DIM ** -0.5
    p = jax.nn.softmax(s, axis=-1).astype(v.dtype)
    o = jnp.einsum('bhqk,bkhd->bqhd', p, v).reshape(B, S, MEM_HEADS * MEM_HEAD_DIM)
    return o @ w_mo


def _hier_moe(x, w_rg, b_rg, w_re, b_re, w_gate_e, w_up_e, w_down_e):
    B, S, D = x.shape
    T = B * S
    xf = x.reshape(T, D)
    p_grp = jax.nn.softmax((xf @ w_rg).astype(jnp.float32) + b_rg.astype(jnp.float32), axis=-1)
    p_top, g_idx = lax.top_k(p_grp, 1)
    e_logits = ((xf @ w_re).astype(jnp.float32) + b_re.astype(jnp.float32)).reshape(
        T, N_GROUPS, EXPERTS_PER_GROUP)
    in_grp = jnp.take_along_axis(e_logits, g_idx[:, :, None], axis=1)[:, 0]
    e_top, e_local = lax.top_k(jax.nn.softmax(in_grp, axis=-1), TOP_K)
    gate = p_top * e_top / jnp.sum(e_top, axis=-1, keepdims=True)
    e_global = g_idx * EXPERTS_PER_GROUP + e_local
    A = T * TOP_K
    e_flat = e_global.reshape(A)
    tok_flat = jnp.repeat(jnp.arange(T, dtype=jnp.int32), TOP_K)
    w_flat = gate.reshape(A)
    order = jnp.argsort(e_flat)
    e_s, tok_s, w_s = e_flat[order], tok_flat[order], w_flat[order]
    counts = jnp.bincount(e_flat, length=N_EXPERTS)
    padded = (counts + MOE_BLOCK - 1) // MOE_BLOCK * MOE_BLOCK
    starts = jnp.cumsum(counts) - counts
    p_ends = jnp.cumsum(padded)
    p_starts = p_ends - padded
    dest = p_starts[e_s] + jnp.arange(A, dtype=jnp.int32) - starts[e_s]
    n_blocks = -(-A // MOE_BLOCK) + N_EXPERTS
    P = n_blocks * MOE_BLOCK
    row_tok = jnp.full((P,), T, jnp.int32).at[dest].set(tok_s)
    row_w = jnp.zeros((P,), jnp.float32).at[dest].set(w_s)
    blk_exp = jnp.minimum(jnp.searchsorted(p_ends, jnp.arange(n_blocks) * MOE_BLOCK, side='right'),
                          N_EXPERTS - 1)
    x_pad = jnp.concatenate([xf, jnp.zeros((1, D), xf.dtype)], axis=0)
    xs = x_pad[row_tok].reshape(n_blocks, MOE_BLOCK, D)

    def expert_block(args):
        xb, e = args
        hb = jax.nn.silu(xb @ w_gate_e[e]) * (xb @ w_up_e[e])
        return hb @ w_down_e[e]

    ys = lax.map(expert_block, (xs, blk_exp)).reshape(P, D)
    y = jnp.zeros((T + 1, D), x.dtype).at[row_tok].add(ys * row_w[:, None].astype(ys.dtype))
    return y[:T].reshape(B, S, D)


def _layer(x, mem, positions, w_in, b_gate, mla_q_norm, w_uq, mla_kv_norm, w_ukv, w_o_mla,
           gdn_conv, gdn_a_log, gdn_dt_bias, gdn_norm, w_o_gdn, w_out, ln1_g, ln1_b,
           w_mq, w_mkv, w_mo, ln2_g, ln2_b, w_route_grp, b_route_grp, w_route_exp, b_route_exp,
           w_gate_e, w_up_e, w_down_e, ln3_g, ln3_b):
    h = x @ w_in
    cuts = np.cumsum(MIX_SPLITS)[:-1].tolist()
    q_down, kv_down, qkv, z, b, a, gates = jnp.split(h, cuts, axis=-1)
    y_mla = _mla(q_down, kv_down, positions, mla_q_norm, w_uq, mla_kv_norm, w_ukv) @ w_o_mla
    y_gdn = _gdn(qkv, z, b, a, gdn_conv, gdn_a_log, gdn_dt_bias, gdn_norm) @ w_o_gdn
    g = jax.nn.sigmoid((gates + b_gate).astype(jnp.float32)).astype(x.dtype)
    g_mla, g_gdn = g[..., :D_MODEL], g[..., D_MODEL:]
    mix = (g_mla * y_mla + g_gdn * y_gdn) @ w_out
    x = _layernorm(DN_ALPHA * x + mix, ln1_g, ln1_b)
    x = _layernorm(DN_ALPHA * x + _memory_attn(x, mem, w_mq, w_mkv, w_mo), ln2_g, ln2_b)
    moe = _hier_moe(x, w_route_grp, b_route_grp, w_route_exp, b_route_exp, w_gate_e, w_up_e, w_down_e)
    return _layernorm(DN_ALPHA * x + moe, ln3_g, ln3_b)


def setup_inputs(seed: int = 0) -> dict:
    key = jax.random.key(seed)
    ks = jax.random.split(key, 40)
    L = DEPTH
    f32 = jnp.float32

    def nrm(k, shape, scale):
        return jax.random.normal(k, shape, f32) * scale

    def gain(k, n):
        return 1.0 + 0.02 * jax.random.normal(k, (L, n), f32)

    x = nrm(ks[0], (BATCH, SEQ, D_MODEL), 1.0)
    mem = nrm(ks[1], (BATCH, MEM_TOKENS, D_MODEL), 1.0)
    offset = jax.random.randint(ks[2], (BATCH, 1), 0, 4096, jnp.int32)
    positions = offset + jnp.arange(SEQ, dtype=jnp.int32)[None, :]
    dt = jnp.exp(jax.random.uniform(ks[12], (L, GDN_V_HEADS), f32, np.log(1e-3), np.log(1e-1)))
    return {
        "x": x,
        "mem": mem,
        "positions": positions,
        "w_in": nrm(ks[3], (L, D_MODEL, D_IN), D_MODEL ** -0.5),
        "b_gate": nrm(ks[4], (L, 2 * D_MODEL), 0.02),
        "mla_q_norm": gain(ks[5], MLA_Q_LORA),
        "w_uq": nrm(ks[6], (L, MLA_Q_LORA, MLA_HEADS * (MLA_NOPE + MLA_ROPE)), MLA_Q_LORA ** -0.5),
        "mla_kv_norm": gain(ks[7], MLA_KV_LORA),
        "w_ukv": nrm(ks[8], (L, MLA_KV_LORA, MLA_HEADS * (MLA_NOPE + MLA_V)), MLA_KV_LORA ** -0.5),
        "w_o_mla": nrm(ks[9], (L, MLA_HEADS * MLA_V, D_MODEL), (MLA_HEADS * MLA_V) ** -0.5),
        "gdn_conv": nrm(ks[10], (L, GDN_CONV, GDN_QKV), GDN_CONV ** -0.5),
        "gdn_a_log": jnp.log(jax.random.uniform(ks[11], (L, GDN_V_HEADS), f32, 1.0, 16.0)),
        "gdn_dt_bias": dt + jnp.log(-jnp.expm1(-dt)),
        "gdn_norm": gain(ks[13], GDN_DV),
        "w_o_gdn": nrm(ks[14], (L, GDN_V_HEADS * GDN_DV, D_MODEL), (GDN_V_HEADS * GDN_DV) ** -0.5),
        "w_out": nrm(ks[15], (L, D_MODEL, D_MODEL), D_MODEL ** -0.5 * DN_BETA),
        "ln1_g": gain(ks[16], D_MODEL),
        "ln1_b": nrm(ks[17], (L, D_MODEL), 0.02),
        "w_mq": nrm(ks[18], (L, D_MODEL, MEM_HEADS * MEM_HEAD_DIM), D_MODEL ** -0.5),
        "w_mkv": nrm(ks[19], (L, D_MODEL, 2 * MEM_HEADS * MEM_HEAD_DIM), D_MODEL ** -0.5),
        "w_mo": nrm(ks[20], (L, MEM_HEADS * MEM_HEAD_DIM, D_MODEL), (MEM_HEADS * MEM_HEAD_DIM) ** -0.5 * DN_BETA),
        "ln2_g": gain(ks[21], D_MODEL),
        "ln2_b": nrm(ks[22], (L, D_MODEL), 0.02),
        "w_route_grp": nrm(ks[23], (L, D_MODEL, N_GROUPS), D_MODEL ** -0.5),
        "b_route_grp": nrm(ks[24], (L, N_GROUPS), 0.01),
        "w_route_exp": nrm(ks[25], (L, D_MODEL, N_EXPERTS), D_MODEL ** -0.5),
        "b_route_exp": nrm(ks[26], (L, N_EXPERTS), 0.01),
        "w_gate_e": nrm(ks[27], (L, N_EXPERTS, D_MODEL, D_EXPERT), D_MODEL ** -0.5),
        "w_up_e": nrm(ks[28], (L, N_EXPERTS, D_MODEL, D_EXPERT), D_MODEL ** -0.5),
        "w_down_e": nrm(ks[29], (L, N_EXPERTS, D_EXPERT, D_MODEL), D_EXPERT ** -0.5 * DN_BETA),
        "ln3_g": gain(ks[30], D_MODEL),
        "ln3_b": nrm(ks[31], (L, D_MODEL), 0.02),
    }


def reference(x, mem, positions, w_in, b_gate, mla_q_norm, w_uq, mla_kv_norm, w_ukv, w_o_mla,
              gdn_conv, gdn_a_log, gdn_dt_bias, gdn_norm, w_o_gdn, w_out, ln1_g, ln1_b,
              w_mq, w_mkv, w_mo, ln2_g, ln2_b, w_route_grp, b_route_grp, w_route_exp, b_route_exp,
              w_gate_e, w_up_e, w_down_e, ln3_g, ln3_b):
    for l in range(DEPTH):
        x = _layer(x, mem, positions, w_in[l], b_gate[l], mla_q_norm[l], w_uq[l], mla_kv_norm[l],
                   w_ukv[l], w_o_mla[l], gdn_conv[l], gdn_a_log[l], gdn_dt_bias[l], gdn_norm[l],
                   w_o_gdn[l], w_out[l], ln1_g[l], ln1_b[l], w_mq[l], w_mkv[l], w_mo[l],
                   ln2_g[l], ln2_b[l], w_route_grp[l], b_route_grp[l], w_route_exp[l],
                   b_route_exp[l], w_gate_e[l], w_up_e[l], w_down_e[l], ln3_g[l], ln3_b[l])
    return x
```

```python
import functools

import numpy as np
import jax
import jax.numpy as jnp
from jax import lax
from jax.experimental import pallas as pl
from jax.experimental.pallas import tpu as pltpu

F32 = jnp.float32
BF16 = jnp.bfloat16
HIGHEST = lax.Precision.HIGHEST

LANES = 128
VMEM_LIMIT = 56 * 1024 * 1024

CHUNK = 64
MLA_HEADS = 16
MLA_Q_LORA = 768
MLA_KV_LORA = 512
MLA_NOPE = 128
MLA_ROPE = 64
MLA_V = 128
MLA_QK_PAD = 256
ROPE_THETA = 10000.0
GDN_QK_HEADS = 16
GDN_V_HEADS = 32
GDN_DK = 128
GDN_DV = 128
GDN_CONV = 4
GDN_GROUP = 8
MEM_HEADS = 4
MEM_HEAD_DIM = 128
N_GROUPS = 8
EXPERTS_PER_GROUP = 8
N_EXPERTS = 64
D_EXPERT = 512
MOE_ROWS = 256
RMS_EPS = 1e-6
LN_EPS = 1e-5
DN_ALPHA = 2.0 ** 0.25


def _cparams(*sem):
    return pltpu.CompilerParams(dimension_semantics=sem, vmem_limit_bytes=VMEM_LIMIT)


def _dot(a, b, **kw):
    return jnp.dot(a, b, preferred_element_type=F32, **kw)


def _dot_nt(a, b):
    return lax.dot_general(a, b, (((1,), (1,)), ((), ())), preferred_element_type=F32)


def _dot_tn(a, b):
    return lax.dot_general(a, b, (((0,), (0,)), ((), ())), preferred_element_type=F32)


def _sigmoid(x):
    return 1.0 / (1.0 + jnp.exp(-x))


def _layernorm(y, g, b):
    mu = jnp.mean(y, axis=-1, keepdims=True)
    d = y - mu
    var = jnp.mean(d * d, axis=-1, keepdims=True)
    return d * lax.rsqrt(var + LN_EPS) * g + b


def _mm_kernel(x_ref, w_ref, o_ref):
    o_ref[...] = _dot(x_ref[...], w_ref[...]).astype(o_ref.dtype)


def _matmul(x, w, out_dtype, tm, tn):
    m, k = x.shape
    n = w.shape[1]
    tm, tn = min(tm, m), min(tn, n)
    return pl.pallas_call(
        _mm_kernel,
        grid=(n // tn, m // tm),
        in_specs=[pl.BlockSpec((tm, k), lambda j, i: (i, 0)),
                  pl.BlockSpec((k, tn), lambda j, i: (0, j))],
        out_specs=pl.BlockSpec((tm, tn), lambda j, i: (i, j)),
        out_shape=jax.ShapeDtypeStruct((m, n), out_dtype),
        compiler_params=_cparams("parallel", "parallel"),
        name="matmul",
    )(x, w)


def _rope_table_kernel(pos_ref, inv_ref, o_ref):
    ang = pos_ref[...].astype(F32) * inv_ref[...]
    lane = lax.broadcasted_iota(jnp.int32, ang.shape, 1)
    o_ref[...] = jnp.where(lane < MLA_ROPE, jnp.cos(ang), jnp.sin(ang))


def _rope_table(positions):
    s = positions.shape[0]
    inv = 1.0 / (ROPE_THETA ** (np.arange(0, MLA_ROPE, 2, dtype=np.float32) / MLA_ROPE))
    inv4 = jnp.asarray(np.tile(inv.astype(np.float32), 4)[None, :])
    tm = min(512, s)
    return pl.pallas_call(
        _rope_table_kernel,
        grid=(s // tm,),
        in_specs=[pl.BlockSpec((tm, 1), lambda i: (i, 0)),
                  pl.BlockSpec((1, LANES), lambda i: (0, 0))],
        out_specs=pl.BlockSpec((tm, LANES), lambda i: (i, 0)),
        out_shape=jax.ShapeDtypeStruct((s, LANES), F32),
        compiler_params=_cparams("parallel"),
        name="rope_table",
    )(positions.reshape(s, 1), inv4)


def _rope_pair(t, cs):
    a = t * cs
    return a + pltpu.roll(a, MLA_ROPE, axis=1)


def _mla_q_kernel(qd_ref, qn_ref, w_ref, cs_ref, o_ref, *, scale):
    x = qd_ref[...].astype(F32)
    cq = x * lax.rsqrt(jnp.mean(x * x, axis=-1, keepdims=True) + RMS_EPS) * qn_ref[...]
    cqb = cq.astype(BF16)
    cs = cs_ref[...]
    for h in range(MLA_HEADS):
        lo = h * MLA_QK_PAD
        p = _dot(cqb, w_ref[:, lo:lo + MLA_QK_PAD])
        o_ref[:, lo:lo + LANES] = (_rope_pair(p[:, :LANES], cs) * scale).astype(o_ref.dtype)
        o_ref[:, lo + LANES:lo + MLA_QK_PAD] = (p[:, LANES:] * scale).astype(o_ref.dtype)


def _mla_q_proj(qd, q_norm, wq, cs):
    s = qd.shape[0]
    tm = min(256, s)
    n = MLA_HEADS * MLA_QK_PAD
    scale = float((MLA_NOPE + MLA_ROPE) ** -0.5)
    return pl.pallas_call(
        functools.partial(_mla_q_kernel, scale=scale),
        grid=(s // tm,),
        in_specs=[pl.BlockSpec((tm, MLA_Q_LORA), lambda i: (i, 0)),
                  pl.BlockSpec((1, MLA_Q_LORA), lambda i: (0, 0)),
                  pl.BlockSpec((MLA_Q_LORA, n), lambda i: (0, 0)),
                  pl.BlockSpec((tm, LANES), lambda i: (i, 0))],
        out_specs=pl.BlockSpec((tm, n), lambda i: (i, 0)),
        out_shape=jax.ShapeDtypeStruct((s, n), BF16),
        compiler_params=_cparams("parallel"),
        name="mla_q_proj",
    )(qd, q_norm, wq, cs)


def _mla_kv_kernel(kv_ref, kn_ref, wk_ref, wv_ref, cs_ref, k_ref, v_ref):
    x = kv_ref[:, :MLA_KV_LORA].astype(F32)
    ckv = x * lax.rsqrt(jnp.mean(x * x, axis=-1, keepdims=True) + RMS_EPS) * kn_ref[...]
    cb = ckv.astype(BF16)
    pe = _rope_pair(kv_ref[:, MLA_KV_LORA:].astype(F32), cs_ref[...])
    lane = lax.broadcasted_iota(jnp.int32, pe.shape, 1)
    pe = jnp.where(lane < MLA_ROPE, pe, 0.0).astype(k_ref.dtype)
    kn = _dot(cb, wk_ref[...]).astype(k_ref.dtype)
    for h in range(MLA_HEADS):
        lo = h * MLA_QK_PAD
        k_ref[:, lo:lo + LANES] = pe
        k_ref[:, lo + LANES:lo + MLA_QK_PAD] = kn[:, h * MLA_NOPE:(h + 1) * MLA_NOPE]
    v_ref[...] = _dot(cb, wv_ref[...]).astype(v_ref.dtype)


def _mla_kv_proj(kvd, kv_norm, wk, wv, cs):
    s, w = kvd.shape
    tm = min(256, s)
    nk = MLA_HEADS * MLA_QK_PAD
    nv = MLA_HEADS * MLA_V
    return pl.pallas_call(
        _mla_kv_kernel,
        grid=(s // tm,),
        in_specs=[pl.BlockSpec((tm, w), lambda i: (i, 0)),
                  pl.BlockSpec((1, MLA_KV_LORA), lambda i: (0, 0)),
                  pl.BlockSpec(wk.shape, lambda i: (0, 0)),
                  pl.BlockSpec(wv.shape, lambda i: (0, 0)),
                  pl.BlockSpec((tm, LANES), lambda i: (i, 0))],
        out_specs=[pl.BlockSpec((tm, nk), lambda i: (i, 0)),
                   pl.BlockSpec((tm, nv), lambda i: (i, 0))],
        out_shape=[jax.ShapeDtypeStruct((s, nk), BF16),
                   jax.ShapeDtypeStruct((s, nv), BF16)],
        compiler_params=_cparams("parallel"),
        name="mla_kv_proj",
    )(kvd, kv_norm, wk, wv, cs)


def _flash_kernel(q_ref, k_ref, v_ref, o_ref, m_sc, l_sc, acc_sc, *, tq):
    i = pl.program_id(1)
    j = pl.program_id(2)

    @pl.when(j == 0)
    def _():
        m_sc[...] = jnp.full(m_sc.shape, -1e30, F32)
        l_sc[...] = jnp.zeros(l_sc.shape, F32)
        acc_sc[...] = jnp.zeros(acc_sc.shape, F32)

    def step(masked):
        s = _dot_nt(q_ref[...], k_ref[...])
        if masked:
            r = lax.broadcasted_iota(jnp.int32, s.shape, 0) // CHUNK
            c = lax.broadcasted_iota(jnp.int32, s.shape, 1) // CHUNK
            s = jnp.where(c <= r, s, -1e30)
        m_prev = m_sc[...]
        m_new = jnp.maximum(m_prev, jnp.max(s, axis=-1, keepdims=True))
        alpha = jnp.exp(m_prev - m_new)
        p = jnp.exp(s - m_new)
        l_sc[...] = alpha * l_sc[...] + jnp.sum(p, axis=-1, keepdims=True)
        acc_sc[...] = alpha * acc_sc[...] + _dot(p.astype(v_ref.dtype), v_ref[...])
        m_sc[...] = m_new

    @pl.when(j < i)
    def _():
        step(False)

    @pl.when(j == i)
    def _():
        step(True)
        o_ref[...] = (acc_sc[...] / l_sc[...]).astype(o_ref.dtype)


def _mla_attention(q, k, v):
    s = q.shape[0]
    t = min(512, s)
    n = s // t
    return pl.pallas_call(
        functools.partial(_flash_kernel, tq=t),
        grid=(MLA_HEADS, n, n),
        in_specs=[pl.BlockSpec((t, MLA_QK_PAD), lambda h, i, j: (i, h)),
                  pl.BlockSpec((t, MLA_QK_PAD), lambda h, i, j: (jnp.minimum(j, i), h)),
                  pl.BlockSpec((t, MLA_V), lambda h, i, j: (jnp.minimum(j, i), h))],
        out_specs=pl.BlockSpec((t, MLA_V), lambda h, i, j: (i, h)),
        out_shape=jax.ShapeDtypeStruct((s, MLA_HEADS * MLA_V), BF16),
        scratch_shapes=[pltpu.VMEM((t, 1), F32), pltpu.VMEM((t, 1), F32), pltpu.VMEM((t, MLA_V), F32)],
        compiler_params=_cparams("parallel", "parallel", "arbitrary"),
        name="mla_flash",
    )(q, k, v)


def _gdn_conv_kernel(x_ref, prev_ref, w_ref, o_ref, buf, *, tm, tc, halo, n_qk_blocks):
    c = pl.program_id(0)
    i = pl.program_id(1)
    prev = prev_ref[...].astype(F32)
    buf[0:halo, :] = jnp.where(i > 0, prev, 0.0)
    buf[halo:halo + tm, :] = x_ref[...].astype(F32)
    y = jnp.zeros((tm, tc), F32)
    for j in range(GDN_CONV):
        y = y + w_ref[j:j + 1, :] * buf[pl.ds(halo - (GDN_CONV - 1) + j, tm), :]
    y = y * _sigmoid(y)
    is_qk = c < n_qk_blocks
    for g in range(tc // LANES):
        seg = y[:, g * LANES:(g + 1) * LANES]
        nrm = seg * lax.rsqrt(jnp.sum(seg * seg, axis=-1, keepdims=True) + 1e-6)
        o_ref[:, g * LANES:(g + 1) * LANES] = jnp.where(is_qk, nrm, seg).astype(o_ref.dtype)


def _gdn_conv(xqkv, conv_w):
    s, c = xqkv.shape
    tm = min(512, s)
    tc = 512
    halo = 16
    n_qk_blocks = (2 * GDN_QK_HEADS * GDN_DK) // tc
    kern = functools.partial(_gdn_conv_kernel, tm=tm, tc=tc, halo=halo, n_qk_blocks=n_qk_blocks)
    return pl.pallas_call(
        kern,
        grid=(c // tc, s // tm),
        in_specs=[pl.BlockSpec((tm, tc), lambda cc, i: (i, cc)),
                  pl.BlockSpec((halo, tc), lambda cc, i: (jnp.maximum(i * (tm // halo) - 1, 0), cc)),
                  pl.BlockSpec((GDN_CONV, tc), lambda cc, i: (0, cc))],
        out_specs=pl.BlockSpec((tm, tc), lambda cc, i: (i, cc)),
        out_shape=jax.ShapeDtypeStruct((s, c), BF16),
        scratch_shapes=[pltpu.VMEM((tm + halo, tc), F32)],
        compiler_params=_cparams("parallel", "parallel"),
        name="gdn_conv",
    )(xqkv, xqkv, conv_w)


def _softplus(x):
    return jnp.maximum(x, 0.0) + jnp.log1p(jnp.exp(-jnp.abs(x)))


def _gdn_chunk_kernel(q_ref, k_ref, v_ref, z_ref, ac_ref, bc_ref, ar_ref, alc_ref, dtc_ref, alr_ref, dtr_ref,
                      nw_ref, o_ref, state):
    n = pl.program_id(1)
    g_heads = GDN_GROUP
    c = CHUNK

    @pl.when(n == 0)
    def _():
        state[...] = jnp.zeros(state.shape, F32)

    ri = lax.broadcasted_iota(jnp.int32, (c, c), 0)
    ci = lax.broadcasted_iota(jnp.int32, (c, c), 1)
    tril = ci <= ri
    strict = ci < ri
    ltri = tril.astype(F32)
    utri = (ri <= ci).astype(F32)
    eye = (ri == ci).astype(F32)

    g_col = -jnp.exp(alc_ref[...]) * _softplus(ac_ref[...] + dtc_ref[...])
    gc_col = _dot(ltri, g_col, precision=HIGHEST)
    g_row = -jnp.exp(alr_ref[...]) * _softplus(ar_ref[...] + dtr_ref[...])
    gc_row = _dot(g_row, utri, precision=HIGHEST)
    beta_col = _sigmoid(bc_ref[...])
    nw = nw_ref[...]
    scale = float(GDN_DK ** -0.5)

    for hh in range(g_heads):
        qk = hh // 2
        qh = q_ref[:, qk * GDN_DK:(qk + 1) * GDN_DK]
        kh = k_ref[:, qk * GDN_DK:(qk + 1) * GDN_DK]
        vh = v_ref[:, hh * GDN_DV:(hh + 1) * GDN_DV].astype(F32)
        kf = kh.astype(F32)
        bcol = beta_col[:, hh:hh + 1]
        gcol = gc_col[:, hh:hh + 1]
        grow = gc_row[hh:hh + 1, :]
        glast = gc_col[c - 1:c, hh:hh + 1]
        decay = jnp.where(tril, jnp.exp(jnp.where(tril, gcol - grow, 0.0)), 0.0)
        kb = kf * bcol
        kq = _dot_nt(jnp.concatenate([kb.astype(BF16), qh], axis=0), kh)
        m = jnp.where(strict, kq[:c] * decay, 0.0)
        a_qk = jnp.where(tril, kq[c:] * decay, 0.0) * scale
        t_inv = eye - m
        xp = m
        for _ in range(5):
            xb = xp.astype(BF16)
            xp = _dot(xb, xb)
            t_inv = t_inv + _dot(t_inv.astype(BF16), xp.astype(BF16))
        rhs = jnp.concatenate([vh * bcol, kb * jnp.exp(gcol)], axis=1).astype(BF16)
        uw = _dot(t_inv.astype(BF16), rhs)
        u = uw[:, :GDN_DV]
        w = uw[:, GDN_DV:]
        st = state[hh]
        qg = qh.astype(F32) * jnp.exp(gcol)
        ws = _dot(jnp.concatenate([w, qg], axis=0).astype(BF16), st.astype(BF16))
        v_new = u - ws[:c]
        o = ws[c:] * scale + _dot(a_qk.astype(BF16), v_new.astype(BF16))
        kdec = (kf * jnp.exp(glast - gcol)).astype(BF16)
        state[hh] = st * jnp.exp(glast) + _dot_tn(kdec, v_new.astype(BF16))
        o = o * lax.rsqrt(jnp.mean(o * o, axis=-1, keepdims=True) + RMS_EPS) * nw
        zz = z_ref[:, hh * GDN_DV:(hh + 1) * GDN_DV].astype(F32)
        o_ref[:, hh * GDN_DV:(hh + 1) * GDN_DV] = (o * (zz * _sigmoid(zz))).astype(o_ref.dtype)


def _gdn_delta(qkv, z, a, b, a_log, dt_bias, norm_w):
    s = qkv.shape[0]
    g = GDN_GROUP
    ng = GDN_V_HEADS // g
    nc = s // CHUNK
    gq = g // 2
    wq = gq * GDN_DK
    wv = g * GDN_DV
    k_off = (GDN_QK_HEADS * GDN_DK) // wq
    v_off = (2 * GDN_QK_HEADS * GDN_DK) // wv
    a_col = a.reshape(s, ng, g).transpose(1, 0, 2)
    b_col = b.reshape(s, ng, g).transpose(1, 0, 2)
    a_row = a.reshape(nc, CHUNK, ng, g).transpose(2, 0, 3, 1)
    al_c = a_log.reshape(ng, 1, g)
    dt_c = dt_bias.reshape(ng, 1, g)
    al_r = a_log.reshape(ng, g, 1)
    dt_r = dt_bias.reshape(ng, g, 1)
    return pl.pallas_call(
        _gdn_chunk_kernel,
        grid=(ng, nc),
        in_specs=[pl.BlockSpec((CHUNK, wq), lambda hg, n: (n, hg)),
                  pl.BlockSpec((CHUNK, wq), lambda hg, n: (n, k_off + hg)),
                  pl.BlockSpec((CHUNK, wv), lambda hg, n: (n, v_off + hg)),
                  pl.BlockSpec((CHUNK, wv), lambda hg, n: (n, hg)),
                  pl.BlockSpec((None, CHUNK, g), lambda hg, n: (hg, n, 0)),
                  pl.BlockSpec((None, CHUNK, g), lambda hg, n: (hg, n, 0)),
                  pl.BlockSpec((None, None, g, CHUNK), lambda hg, n: (hg, n, 0, 0)),
                  pl.BlockSpec((None, 1, g), lambda hg, n: (hg, 0, 0)),
                  pl.BlockSpec((None, 1, g), lambda hg, n: (hg, 0, 0)),
                  pl.BlockSpec((None, g, 1), lambda hg, n: (hg, 0, 0)),
                  pl.BlockSpec((None, g, 1), lambda hg, n: (hg, 0, 0)),
                  pl.BlockSpec((1, GDN_DV), lambda hg, n: (0, 0))],
        out_specs=pl.BlockSpec((CHUNK, wv), lambda hg, n: (n, hg)),
        out_shape=jax.ShapeDtypeStruct((s, GDN_V_HEADS * GDN_DV), BF16),
        scratch_shapes=[pltpu.VMEM((g, GDN_DK, GDN_DV), F32)],
        compiler_params=_cparams("parallel", "arbitrary"),
        name="gdn_delta",
    )(qkv, qkv, qkv, z, a_col, b_col, a_row, al_c, dt_c, al_r, dt_r, norm_w)


def _branch_kernel(om_ref, og_ref, wm_ref, wg_ref, gm_ref, gg_ref, bm_ref, bg_ref, o_ref):
    ym = _dot(om_ref[...], wm_ref[...])
    yg = _dot(og_ref[...], wg_ref[...])
    sm = _sigmoid(gm_ref[...].astype(F32) + bm_ref[...])
    sg = _sigmoid(gg_ref[...].astype(F32) + bg_ref[...])
    o_ref[...] = (sm * ym + sg * yg).astype(o_ref.dtype)


def _branch_merge(o_mla, o_gdn, w_o_mla, w_o_gdn, gates, b_gate):
    s = o_mla.shape[0]
    d = w_o_mla.shape[1]
    tm = min(512, s)
    tn = min(512, d)
    nb = d // tn
    return pl.pallas_call(
        _branch_kernel,
        grid=(nb, s // tm),
        in_specs=[pl.BlockSpec((tm, o_mla.shape[1]), lambda j, i: (i, 0)),
                  pl.BlockSpec((tm, o_gdn.shape[1]), lambda j, i: (i, 0)),
                  pl.BlockSpec((w_o_mla.shape[0], tn), lambda j, i: (0, j)),
                  pl.BlockSpec((w_o_gdn.shape[0], tn), lambda j, i: (0, j)),
                  pl.BlockSpec((tm, tn), lambda j, i: (i, j)),
                  pl.BlockSpec((tm, tn), lambda j, i: (i, nb + j)),
                  pl.BlockSpec((1, tn), lambda j, i: (0, j)),
                  pl.BlockSpec((1, tn), lambda j, i: (0, nb + j))],
        out_specs=pl.BlockSpec((tm, tn), lambda j, i: (i, j)),
        out_shape=jax.ShapeDtypeStruct((s, d), BF16),
        compiler_params=_cparams("parallel", "parallel"),
        name="branch_merge",
    )(o_mla, o_gdn, w_o_mla, w_o_gdn, gates, gates, b_gate, b_gate)


def _out_ln_kernel(m_ref, w_ref, x_ref, g_ref, b_ref, o_ref):
    y = DN_ALPHA * x_ref[...] + _dot(m_ref[...], w_ref[...])
    o_ref[...] = _layernorm(y, g_ref[...], b_ref[...])


def _out_ln(mixed, w_out, x, g, b):
    s, d = x.shape
    tm = min(256, s)
    return pl.pallas_call(
        _out_ln_kernel,
        grid=(s // tm,),
        in_specs=[pl.BlockSpec((tm, d), lambda i: (i, 0)),
                  pl.BlockSpec((d, d), lambda i: (0, 0)),
                  pl.BlockSpec((tm, d), lambda i: (i, 0)),
                  pl.BlockSpec((1, d), lambda i: (0, 0)),
                  pl.BlockSpec((1, d), lambda i: (0, 0))],
        out_specs=pl.BlockSpec((tm, d), lambda i: (i, 0)),
        out_shape=jax.ShapeDtypeStruct((s, d), F32),
        compiler_params=_cparams("parallel"),
        name="out_ln1",
    )(mixed, w_out, x, g, b)


def _mem_kernel(x_ref, wq_ref, kv_ref, wo_ref, g_ref, b_ref, wr_ref, br_ref, x2_ref, lg_ref):
    x1 = x_ref[...]
    hd = MEM_HEAD_DIM
    nh = MEM_HEADS
    q = (_dot(x1.astype(BF16), wq_ref[...]) * float(hd ** -0.5)).astype(BF16)
    outs = []
    for h in range(nh):
        kh = kv_ref[:, h * hd:(h + 1) * hd]
        vh = kv_ref[:, (nh + h) * hd:(nh + h + 1) * hd]
        s = _dot_nt(q[:, h * hd:(h + 1) * hd], kh)
        p = jnp.exp(s - jnp.max(s, axis=-1, keepdims=True))
        o = _dot(p.astype(BF16), vh) / jnp.sum(p, axis=-1, keepdims=True)
        outs.append(o.astype(BF16))
    o = jnp.concatenate(outs, axis=1)
    y = DN_ALPHA * x1 + _dot(o, wo_ref[...])
    x2 = _layernorm(y, g_ref[...], b_ref[...])
    x2_ref[...] = x2
    lg_ref[...] = _dot(x2, wr_ref[...], precision=HIGHEST) + br_ref[...]


def _mem_attn_ln(x1, w_mq, kvm, w_mo, g, b, w_r, b_r):
    s, d = x1.shape
    tm = min(256, s)
    return pl.pallas_call(
        _mem_kernel,
        grid=(s // tm,),
        in_specs=[pl.BlockSpec((tm, d), lambda i: (i, 0)),
                  pl.BlockSpec(w_mq.shape, lambda i: (0, 0)),
                  pl.BlockSpec(kvm.shape, lambda i: (0, 0)),
                  pl.BlockSpec(w_mo.shape, lambda i: (0, 0)),
                  pl.BlockSpec((1, d), lambda i: (0, 0)),
                  pl.BlockSpec((1, d), lambda i: (0, 0)),
                  pl.BlockSpec(w_r.shape, lambda i: (0, 0)),
                  pl.BlockSpec((1, LANES), lambda i: (0, 0))],
        out_specs=[pl.BlockSpec((tm, d), lambda i: (i, 0)),
                   pl.BlockSpec((tm, LANES), lambda i: (i, 0))],
        out_shape=[jax.ShapeDtypeStruct((s, d), F32),
                   jax.ShapeDtypeStruct((s, LANES), F32)],
        compiler_params=_cparams("parallel"),
        name="mem_attn_ln2",
    )(x1, w_mq, kvm, w_mo, g, b, w_r, b_r)


SEL_E1, SEL_E2, SEL_R1, SEL_R2, SEL_G1, SEL_G2 = range(6)
GRP_LANE0 = N_EXPERTS


def _route_kernel(lg_ref, sel_ref, cnt_ref, carry):
    i = pl.program_id(0)

    @pl.when(i == 0)
    def _():
        carry[...] = jnp.zeros(carry.shape, F32)

    lg = lg_ref[...]
    tm = lg.shape[0]
    lane = lax.broadcasted_iota(jnp.int32, lg.shape, 1)
    big = jnp.int32(4 * LANES)
    neg = jnp.float32(-jnp.inf)

    def first_max(vals):
        mx = jnp.max(vals, axis=-1, keepdims=True)
        idx = jnp.min(jnp.where(vals == mx, lane, big), axis=-1, keepdims=True)
        return mx, idx

    is_grp = (lane >= GRP_LANE0) & (lane < GRP_LANE0 + N_GROUPS)
    gl = jnp.where(is_grp, lg, neg)
    gmax, gidx = first_max(gl)
    p_top = 1.0 / jnp.sum(jnp.where(is_grp, jnp.exp(gl - gmax), 0.0), axis=-1, keepdims=True)
    lo = (gidx - GRP_LANE0) * EXPERTS_PER_GROUP
    in_grp = (lane >= lo) & (lane < lo + EXPERTS_PER_GROUP)
    el = jnp.where(in_grp, lg, neg)
    m1, i1 = first_max(el)
    m2, i2 = first_max(jnp.where(lane == i1, neg, el))
    r = jnp.exp(m2 - m1)
    g1 = p_top / (1.0 + r)
    g2 = p_top * r / (1.0 + r)

    hot1 = lane == i1
    hot2 = lane == i2
    onehot = jnp.where(hot1, 1.0, 0.0) + jnp.where(hot2, 1.0, 0.0)
    ri = lax.broadcasted_iota(jnp.int32, (tm, tm), 0)
    ci = lax.broadcasted_iota(jnp.int32, (tm, tm), 1)
    before = jnp.where(ci < ri, 1.0, 0.0).astype(BF16)
    rank = _dot(before, onehot.astype(BF16)) + carry[...]
    r1 = jnp.sum(jnp.where(hot1, rank, 0.0), axis=-1, keepdims=True)
    r2 = jnp.sum(jnp.where(hot2, rank, 0.0), axis=-1, keepdims=True)
    carry[...] = carry[...] + jnp.sum(onehot, axis=0, keepdims=True)
    cnt_ref[...] = carry[...]

    out = jnp.zeros(lg.shape, F32)
    for ln, val in ((SEL_E1, i1.astype(F32)), (SEL_E2, i2.astype(F32)), (SEL_R1, r1), (SEL_R2, r2),
                    (SEL_G1, g1), (SEL_G2, g2)):
        out = jnp.where(lane == ln, val, out)
    sel_ref[...] = out


def _route(logits):
    t = logits.shape[0]
    tm = min(512, t)
    return pl.pallas_call(
        _route_kernel,
        grid=(t // tm,),
        in_specs=[pl.BlockSpec((tm, LANES), lambda i: (i, 0))],
        out_specs=[pl.BlockSpec((tm, LANES), lambda i: (i, 0)),
                   pl.BlockSpec((1, LANES), lambda i: (0, 0))],
        out_shape=[jax.ShapeDtypeStruct((t, LANES), F32),
                   jax.ShapeDtypeStruct((1, LANES), F32)],
        scratch_shapes=[pltpu.VMEM((1, LANES), F32)],
        compiler_params=_cparams("arbitrary"),
        name="moe_route",
    )(logits)


def _dispatch_kernel(cnt_ref, e1_ref, e2_ref, r1_ref, r2_ref, rowa_ref, blk_ref, nblk_ref, pstart, *, t, n_blocks):
    def seg(e, start):
        pstart[e] = start
        c = cnt_ref[e]
        nb = (c + MOE_ROWS - 1) // MOE_ROWS
        end = start + nb * MOE_ROWS

        def mark(p, carry):
            rowa_ref[p] = -1
            return carry
        lax.fori_loop(start + c, end, mark, 0)

        def blk(bi, carry):
            blk_ref[bi] = e
            return carry
        lax.fori_loop(start // MOE_ROWS, end // MOE_ROWS, blk, 0)
        return end

    total = lax.fori_loop(0, N_EXPERTS, seg, 0)
    used = total // MOE_ROWS
    nblk_ref[0] = used
    last = blk_ref[jnp.maximum(used - 1, 0)]

    def tail(bi, carry):
        blk_ref[bi] = last
        return carry
    lax.fori_loop(used, n_blocks, tail, 0)

    def unused(p, carry):
        rowa_ref[p] = -1
        return carry
    lax.fori_loop(total, n_blocks * MOE_ROWS, unused, 0)

    def place(tok, carry):
        rowa_ref[pstart[e1_ref[tok]] + r1_ref[tok]] = 2 * tok
        rowa_ref[pstart[e2_ref[tok]] + r2_ref[tok]] = 2 * tok + 1
        return carry
    lax.fori_loop(0, t, place, 0)


def _dispatch(cnt, e1, e2, r1, r2, n_blocks):
    t = e1.shape[0]
    smem = pl.BlockSpec(memory_space=pltpu.SMEM)
    return pl.pallas_call(
        functools.partial(_dispatch_kernel, t=t, n_blocks=n_blocks),
        in_specs=[smem] * 5,
        out_specs=[smem] * 3,
        out_shape=[jax.ShapeDtypeStruct((n_blocks * MOE_ROWS,), jnp.int32),
                   jax.ShapeDtypeStruct((n_blocks,), jnp.int32),
                   jax.ShapeDtypeStruct((1,), jnp.int32)],
        scratch_shapes=[pltpu.SMEM((N_EXPERTS,), jnp.int32)],
        name="moe_dispatch",
    )(cnt, e1, e2, r1, r2)


def _expert_kernel(blk_ref, rowa_ref, nblk_ref, x_hbm, wg_ref, wu_ref, wd_ref, out_hbm, xbuf, ybuf, gsem, ssem):
    b = pl.program_id(0)
    base = b * MOE_ROWS

    def gather_copy(r, tok):
        return pltpu.make_async_copy(x_hbm.at[pl.ds(tok, 1), :], xbuf.at[pl.ds(r, 1), :], gsem)

    def scatter_copy(r, dst):
        return pltpu.make_async_copy(ybuf.at[pl.ds(r, 1), :], out_hbm.at[pl.ds(dst, 1), :], ssem)

    @pl.when(b < nblk_ref[0])
    def _():
        def issue_gather(r, carry):
            a = rowa_ref[base + r]
            gather_copy(r, jnp.maximum(a, 0) // 2).start()
            return carry
        lax.fori_loop(0, MOE_ROWS, issue_gather, 0)

        def wait_gather(r, carry):
            gather_copy(r, 0).wait()
            return carry
        lax.fori_loop(0, MOE_ROWS, wait_gather, 0)

        xb = xbuf[...].astype(BF16)
        hg = _dot(xb, wg_ref[...].astype(BF16))
        hu = _dot(xb, wu_ref[...].astype(BF16))
        hb = (hg * _sigmoid(hg) * hu).astype(BF16)
        ybuf[...] = _dot(hb, wd_ref[...].astype(BF16))

        def issue_scatter(r, carry):
            a = rowa_ref[base + r]

            @pl.when(a >= 0)
            def _():
                scatter_copy(r, a).start()
            return carry
        lax.fori_loop(0, MOE_ROWS, issue_scatter, 0)

        def wait_scatter(r, carry):
            @pl.when(rowa_ref[base + r] >= 0)
            def _():
                scatter_copy(r, 0).wait()
            return carry
        lax.fori_loop(0, MOE_ROWS, wait_scatter, 0)


def _experts(x2, row_a, blk_exp, nblk, w_gate_e, w_up_e, w_down_e):
    t, d = x2.shape
    n_blocks = blk_exp.shape[0]
    de = w_gate_e.shape[2]
    grid_spec = pltpu.PrefetchScalarGridSpec(
        num_scalar_prefetch=3,
        grid=(n_blocks,),
        in_specs=[pl.BlockSpec(memory_space=pl.ANY),
                  pl.BlockSpec((None, d, de), lambda b, blk, rowa, nb: (blk[b], 0, 0)),
                  pl.BlockSpec((None, d, de), lambda b, blk, rowa, nb: (blk[b], 0, 0)),
                  pl.BlockSpec((None, de, d), lambda b, blk, rowa, nb: (blk[b], 0, 0))],
        out_specs=pl.BlockSpec(memory_space=pl.ANY),
        scratch_shapes=[pltpu.VMEM((MOE_ROWS, d), F32), pltpu.VMEM((MOE_ROWS, d), F32),
                        pltpu.SemaphoreType.DMA(()), pltpu.SemaphoreType.DMA(())],
    )
    return pl.pallas_call(
        _expert_kernel,
        grid_spec=grid_spec,
        out_shape=jax.ShapeDtypeStruct((2 * t, d), F32),
        compiler_params=_cparams("arbitrary"),
        name="moe_experts",
    )(blk_exp, row_a, nblk, x2, w_gate_e, w_up_e, w_down_e)


def _combine_kernel(x_ref, y_ref, sel_ref, g_ref, b_ref, o_ref):
    d = x_ref.shape[1]
    sel = sel_ref[...]
    g1 = sel[:, SEL_G1:SEL_G1 + 1]
    g2 = sel[:, SEL_G2:SEL_G2 + 1]
    y = DN_ALPHA * x_ref[...] + (g1 * y_ref[:, :d] + g2 * y_ref[:, d:])
    o_ref[...] = _layernorm(y, g_ref[...], b_ref[...])


def _combine_ln(x2, ys2, sel, g, b):
    t, d = x2.shape
    tm = min(256, t)
    return pl.pallas_call(
        _combine_kernel,
        grid=(t // tm,),
        in_specs=[pl.BlockSpec((tm, d), lambda i: (i, 0)),
                  pl.BlockSpec((tm, 2 * d), lambda i: (i, 0)),
                  pl.BlockSpec((tm, LANES), lambda i: (i, 0)),
                  pl.BlockSpec((1, d), lambda i: (0, 0)),
                  pl.BlockSpec((1, d), lambda i: (0, 0))],
        out_specs=pl.BlockSpec((tm, d), lambda i: (i, 0)),
        out_shape=jax.ShapeDtypeStruct((t, d), F32),
        compiler_params=_cparams("parallel"),
        name="combine_ln3",
    )(x2, ys2, sel, g, b)


def _mixer(xb, positions, w_in, b_gate, mla_q_norm, w_uq, mla_kv_norm, w_ukv, w_o_mla,
           gdn_conv, gdn_a_log, gdn_dt_bias, gdn_norm, w_o_gdn):
    s, d = xb.shape
    nqkv = 2 * GDN_QK_HEADS * GDN_DK + GDN_V_HEADS * GDN_DV
    nz = GDN_V_HEADS * GDN_DV
    o0 = 0
    o1 = o0 + MLA_Q_LORA
    o2 = o1 + MLA_KV_LORA + MLA_ROPE
    o3 = o2 + nqkv
    o4 = o3 + nz
    o5 = o4 + GDN_V_HEADS
    o6 = o5 + GDN_V_HEADS

    def rot_cols(w):
        half = MLA_ROPE // 2
        return jnp.concatenate([-w[..., half:], w[..., :half]], axis=-1)

    w_q = w_in[:, o0:o1].astype(BF16)
    w_kpe = w_in[:, o1 + MLA_KV_LORA:o2]
    w_kv = jnp.concatenate([w_in[:, o1:o1 + MLA_KV_LORA], w_kpe, rot_cols(w_kpe)], axis=1).astype(BF16)
    w_ba = jnp.concatenate([w_in[:, o4:o6], jnp.zeros((d, LANES - 2 * GDN_V_HEADS), F32)], axis=1).astype(BF16)
    qd = _matmul(xb, w_q, BF16, 512, MLA_Q_LORA)
    kvd = _matmul(xb, w_kv, BF16, 512, w_kv.shape[1])
    hqkv = _matmul(xb, w_in[:, o2:o3].astype(BF16), BF16, 512, 1024)
    hz = _matmul(xb, w_in[:, o3:o4].astype(BF16), BF16, 512, 1024)
    hba = _matmul(xb, w_ba, F32, 512, LANES)
    hgate = _matmul(xb, w_in[:, o6:].astype(BF16), BF16, 512, 1024)

    cs = _rope_table(positions)
    hq = MLA_NOPE + MLA_ROPE
    wq3 = w_uq.reshape(MLA_Q_LORA, MLA_HEADS, hq)
    pe = wq3[..., MLA_NOPE:]
    wq = jnp.concatenate([pe, rot_cols(pe), wq3[..., :MLA_NOPE]], axis=-1)
    wq = wq.reshape(MLA_Q_LORA, MLA_HEADS * MLA_QK_PAD).astype(BF16)
    wkv3 = w_ukv.reshape(MLA_KV_LORA, MLA_HEADS, MLA_NOPE + MLA_V)
    wk = wkv3[..., :MLA_NOPE].reshape(MLA_KV_LORA, MLA_HEADS * MLA_NOPE).astype(BF16)
    wv = wkv3[..., MLA_NOPE:].reshape(MLA_KV_LORA, MLA_HEADS * MLA_V).astype(BF16)
    q = _mla_q_proj(qd, mla_q_norm.reshape(1, -1), wq, cs)
    k, v = _mla_kv_proj(kvd, mla_kv_norm.reshape(1, -1), wk, wv, cs)
    o_mla = _mla_attention(q, k, v)

    qkv_c = _gdn_conv(hqkv, gdn_conv)
    o_gdn = _gdn_delta(qkv_c, hz, hba[:, GDN_V_HEADS:2 * GDN_V_HEADS], hba[:, :GDN_V_HEADS],
                       gdn_a_log, gdn_dt_bias, gdn_norm.reshape(1, -1))

    return _branch_merge(o_mla, o_gdn, w_o_mla.astype(BF16), w_o_gdn.astype(BF16), hgate, b_gate.reshape(1, -1))


def _moe(x2, logits, w_gate_e, w_up_e, w_down_e, ln_g, ln_b):
    t, d = x2.shape
    n_blocks = (2 * t) // MOE_ROWS + N_EXPERTS
    sel, cnt = _route(logits)
    as_i32 = lambda col: sel[:, col].astype(jnp.int32)
    row_a, blk_exp, nblk = _dispatch(cnt[0, :N_EXPERTS].astype(jnp.int32), as_i32(SEL_E1), as_i32(SEL_E2),
                                     as_i32(SEL_R1), as_i32(SEL_R2), n_blocks)
    ys = _experts(x2, row_a, blk_exp, nblk, w_gate_e, w_up_e, w_down_e)
    ys2 = ys.reshape(t, 2 * d)
    return _combine_ln(x2, ys2, sel, ln_g, ln_b)


def _layer(x, mem, positions, w_in, b_gate, mla_q_norm, w_uq, mla_kv_norm, w_ukv, w_o_mla,
           gdn_conv, gdn_a_log, gdn_dt_bias, gdn_norm, w_o_gdn, w_out, ln1_g, ln1_b,
           w_mq, w_mkv, w_mo, ln2_g, ln2_b, w_route_grp, b_route_grp, w_route_exp, b_route_exp,
           w_gate_e, w_up_e, w_down_e, ln3_g, ln3_b):
    d = x.shape[1]
    row = lambda p: p.reshape(1, -1)
    mixed = _mixer(x.astype(BF16), positions, w_in, b_gate, mla_q_norm, w_uq, mla_kv_norm, w_ukv, w_o_mla,
                   gdn_conv, gdn_a_log, gdn_dt_bias, gdn_norm, w_o_gdn)
    x1 = _out_ln(mixed, w_out.astype(BF16), x, row(ln1_g), row(ln1_b))

    kvm = _matmul(mem.astype(BF16), w_mkv.astype(BF16), BF16, 256, 512)
    pad = LANES - N_EXPERTS - N_GROUPS
    w_r = jnp.concatenate([w_route_exp, w_route_grp, jnp.zeros((d, pad), F32)], axis=1)
    b_r = jnp.concatenate([b_route_exp, b_route_grp, jnp.zeros((pad,), F32)]).reshape(1, LANES)
    x2, logits = _mem_attn_ln(x1, w_mq.astype(BF16), kvm, w_mo.astype(BF16), row(ln2_g), row(ln2_b), w_r, b_r)

    return _moe(x2, logits, w_gate_e, w_up_e, w_down_e, row(ln3_g), row(ln3_b))


def kernel(x, mem, positions, w_in, b_gate, mla_q_norm, w_uq, mla_kv_norm, w_ukv, w_o_mla, gdn_conv, gdn_a_log,
           gdn_dt_bias, gdn_norm, w_o_gdn, w_out, ln1_g, ln1_b, w_mq, w_mkv, w_mo, ln2_g, ln2_b, w_route_grp,
           b_route_grp, w_route_exp, b_route_exp, w_gate_e, w_up_e, w_down_e, ln3_g, ln3_b):
    outs = []
    for bi in range(x.shape[0]):
        h = x[bi]
        for l in range(w_in.shape[0]):
            h = _layer(h, mem[bi], positions[bi], w_in[l], b_gate[l], mla_q_norm[l], w_uq[l], mla_kv_norm[l],
                       w_ukv[l], w_o_mla[l], gdn_conv[l], gdn_a_log[l], gdn_dt_bias[l], gdn_norm[l], w_o_gdn[l],
                       w_out[l], ln1_g[l], ln1_b[l], w_mq[l], w_mkv[l], w_mo[l], ln2_g[l], ln2_b[l],
                       w_route_grp[l], b_route_grp[l], w_route_exp[l], b_route_exp[l], w_gate_e[l], w_up_e[l],
                       w_down_e[l], ln3_g[l], ln3_b[l])
        outs.append(h)
    return jnp.stack(outs, axis=0)
```

```python
import functools

import numpy as np
import jax
import jax.numpy as jnp
from jax import lax
from jax.experimental import pallas as pl
from jax.experimental.pallas import tpu as pltpu

F32 = jnp.float32
BF16 = jnp.bfloat16
HIGHEST = lax.Precision.HIGHEST

LANES = 128
VMEM_LIMIT = 56 * 1024 * 1024

CHUNK = 64
MLA_HEADS = 16
MLA_Q_LORA = 768
MLA_KV_LORA = 512
MLA_NOPE = 128
MLA_ROPE = 64
MLA_V = 128
MLA_QK_PAD = 256
ROPE_THETA = 10000.0
GDN_QK_HEADS = 16
GDN_V_HEADS = 32
GDN_DK = 128
GDN_DV = 128
GDN_CONV = 4
GDN_GROUP = 8
MEM_HEADS = 4
MEM_HEAD_DIM = 128
N_GROUPS = 8
EXPERTS_PER_GROUP = 8
N_EXPERTS = 64
D_EXPERT = 512
MOE_ROWS = 256
RMS_EPS = 1e-6
LN_EPS = 1e-5
DN_ALPHA = 2.0 ** 0.25


def _cparams(*sem):
    return pltpu.CompilerParams(dimension_semantics=sem, vmem_limit_bytes=VMEM_LIMIT)


def _dot(a, b, **kw):
    return jnp.dot(a, b, preferred_element_type=F32, **kw)


def _dot_nt(a, b):
    return lax.dot_general(a, b, (((1,), (1,)), ((), ())), preferred_element_type=F32)


def _dot_tn(a, b):
    return lax.dot_general(a, b, (((0,), (0,)), ((), ())), preferred_element_type=F32)


def _sigmoid(x):
    return 1.0 / (1.0 + jnp.exp(-x))


def _layernorm(y, g, b):
    mu = jnp.mean(y, axis=-1, keepdims=True)
    d = y - mu
    var = jnp.mean(d * d, axis=-1, keepdims=True)
    return d * lax.rsqrt(var + LN_EPS) * g + b


def _mm_kernel(x_ref, w_ref, o_ref):
    o_ref[...] = _dot(x_ref[...], w_ref[...]).astype(o_ref.dtype)


def _matmul(x, w, out_dtype, tm, tn):
    m, k = x.shape
    n = w.shape[1]
    tm, tn = min(tm, m), min(tn, n)
    return pl.pallas_call(
        _mm_kernel,
        grid=(n // tn, m // tm),
        in_specs=[pl.BlockSpec((tm, k), lambda j, i: (i, 0)),
                  pl.BlockSpec((k, tn), lambda j, i: (0, j))],
        out_specs=pl.BlockSpec((tm, tn), lambda j, i: (i, j)),
        out_shape=jax.ShapeDtypeStruct((m, n), out_dtype),
        compiler_params=_cparams("parallel", "parallel"),
        name="matmul",
    )(x, w)


def _rope_table_kernel(pos_ref, inv_ref, o_ref):
    ang = pos_ref[...].astype(F32) * inv_ref[...]
    lane = lax.broadcasted_iota(jnp.int32, ang.shape, 1)
    o_ref[...] = jnp.where(lane < MLA_ROPE, jnp.cos(ang), jnp.sin(ang))


def _rope_table(positions):
    s = positions.shape[0]
    inv = 1.0 / (ROPE_THETA ** (np.arange(0, MLA_ROPE, 2, dtype=np.float32) / MLA_ROPE))
    inv4 = jnp.asarray(np.tile(inv.astype(np.float32), 4)[None, :])
    tm = min(512, s)
    return pl.pallas_call(
        _rope_table_kernel,
        grid=(s // tm,),
        in_specs=[pl.BlockSpec((tm, 1), lambda i: (i, 0)),
                  pl.BlockSpec((1, LANES), lambda i: (0, 0))],
        out_specs=pl.BlockSpec((tm, LANES), lambda i: (i, 0)),
        out_shape=jax.ShapeDtypeStruct((s, LANES), F32),
        compiler_params=_cparams("parallel"),
        name="rope_table",
    )(positions.reshape(s, 1), inv4)


def _rope_pair(t, cs):
    a = t * cs
    return a + pltpu.roll(a, MLA_ROPE, axis=1)


def _mla_q_kernel(qd_ref, qn_ref, w_ref, cs_ref, o_ref, *, scale):
    x = qd_ref[...].astype(F32)
    cq = x * lax.rsqrt(jnp.mean(x * x, axis=-1, keepdims=True) + RMS_EPS) * qn_ref[...]
    cqb = cq.astype(BF16)
    cs = cs_ref[...]
    for h in range(MLA_HEADS):
        lo = h * MLA_QK_PAD
        p = _dot(cqb, w_ref[:, lo:lo + MLA_QK_PAD])
        o_ref[:, lo:lo + LANES] = (_rope_pair(p[:, :LANES], cs) * scale).astype(o_ref.dtype)
        o_ref[:, lo + LANES:lo + MLA_QK_PAD] = (p[:, LANES:] * scale).astype(o_ref.dtype)


def _mla_q_proj(qd, q_norm, wq, cs):
    s = qd.shape[0]
    tm = min(256, s)
    n = MLA_HEADS * MLA_QK_PAD
    scale = float((MLA_NOPE + MLA_ROPE) ** -0.5)
    return pl.pallas_call(
        functools.partial(_mla_q_kernel, scale=scale),
        grid=(s // tm,),
        in_specs=[pl.BlockSpec((tm, MLA_Q_LORA), lambda i: (i, 0)),
                  pl.BlockSpec((1, MLA_Q_LORA), lambda i: (0, 0)),
                  pl.BlockSpec((MLA_Q_LORA, n), lambda i: (0, 0)),
                  pl.BlockSpec((tm, LANES), lambda i: (i, 0))],
        out_specs=pl.BlockSpec((tm, n), lambda i: (i, 0)),
        out_shape=jax.ShapeDtypeStruct((s, n), BF16),
        compiler_params=_cparams("parallel"),
        name="mla_q_proj",
    )(qd, q_norm, wq, cs)


def _mla_kv_kernel(kv_ref, kn_ref, wk_ref, wv_ref, cs_ref, k_ref, v_ref):
    x = kv_ref[:, :MLA_KV_LORA].astype(F32)
    ckv = x * lax.rsqrt(jnp.mean(x * x, axis=-1, keepdims=True) + RMS_EPS) * kn_ref[...]
    cb = ckv.astype(BF16)
    pe = _rope_pair(kv_ref[:, MLA_KV_LORA:].astype(F32), cs_ref[...])
    lane = lax.broadcasted_iota(jnp.int32, pe.shape, 1)
    pe = jnp.where(lane < MLA_ROPE, pe, 0.0).astype(k_ref.dtype)
    kn = _dot(cb, wk_ref[...]).astype(k_ref.dtype)
    for h in range(MLA_HEADS):
        lo = h * MLA_QK_PAD
        k_ref[:, lo:lo + LANES] = pe
        k_ref[:, lo + LANES:lo + MLA_QK_PAD] = kn[:, h * MLA_NOPE:(h + 1) * MLA_NOPE]
    v_ref[...] = _dot(cb, wv_ref[...]).astype(v_ref.dtype)


def _mla_kv_proj(kvd, kv_norm, wk, wv, cs):
    s, w = kvd.shape
    tm = min(256, s)
    nk = MLA_HEADS * MLA_QK_PAD
    nv = MLA_HEADS * MLA_V
    return pl.pallas_call(
        _mla_kv_kernel,
        grid=(s // tm,),
        in_specs=[pl.BlockSpec((tm, w), lambda i: (i, 0)),
                  pl.BlockSpec((1, MLA_KV_LORA), lambda i: (0, 0)),
                  pl.BlockSpec(wk.shape, lambda i: (0, 0)),
                  pl.BlockSpec(wv.shape, lambda i: (0, 0)),
                  pl.BlockSpec((tm, LANES), lambda i: (i, 0))],
        out_specs=[pl.BlockSpec((tm, nk), lambda i: (i, 0)),
                   pl.BlockSpec((tm, nv), lambda i: (i, 0))],
        out_shape=[jax.ShapeDtypeStruct((s, nk), BF16),
                   jax.ShapeDtypeStruct((s, nv), BF16)],
        compiler_params=_cparams("parallel"),
        name="mla_kv_proj",
    )(kvd, kv_norm, wk, wv, cs)


FLASH_HEADS = 2


def _lane_repeat(x, n):
    return jnp.concatenate([x] * n, axis=1)


def _flash_kernel(q_ref, k_ref, v_ref, o_ref, m_sc, acc_sc):
    i = pl.program_id(1)
    j = pl.program_id(2)
    hps = m_sc.shape[0]
    tk = k_ref.shape[0]
    heads = range(hps)

    @pl.when(j == 0)
    def _():
        m_sc[...] = jnp.full(m_sc.shape, -1e30, F32)
        acc_sc[...] = jnp.zeros(acc_sc.shape, F32)

    def step(masked):
        ss = [_dot_nt(q_ref[:, h * MLA_QK_PAD:(h + 1) * MLA_QK_PAD], k_ref[:, h * MLA_QK_PAD:(h + 1) * MLA_QK_PAD])
              for h in heads]
        if masked:
            r = lax.broadcasted_iota(jnp.int32, ss[0].shape, 0) // CHUNK
            c = lax.broadcasted_iota(jnp.int32, ss[0].shape, 1) // CHUNK
            keep = c <= r
            ss = [jnp.where(keep, s, -1e30) for s in ss]
        m_prev = [m_sc[h] for h in heads]
        m_new = [jnp.maximum(mp, jnp.max(s, axis=-1, keepdims=True)) for mp, s in zip(m_prev, ss)]
        alpha = [jnp.exp(mp - mn) for mp, mn in zip(m_prev, m_new)]
        ps = [jnp.exp(s - _lane_repeat(mn, tk // LANES)).astype(BF16) for s, mn in zip(ss, m_new)]
        ones = jnp.ones((tk, MLA_V), BF16)
        pv = [_dot(p, jnp.concatenate([v_ref[:, h * MLA_V:(h + 1) * MLA_V], ones], axis=1))
              for h, p in zip(heads, ps)]
        for h in heads:
            acc_sc[h] = _lane_repeat(alpha[h], 2) * acc_sc[h] + pv[h]
            m_sc[h] = m_new[h]

    @pl.when(j < i)
    def _():
        step(False)

    @pl.when(j == i)
    def _():
        step(True)
        for h in heads:
            acc = acc_sc[h]
            o_ref[:, h * MLA_V:(h + 1) * MLA_V] = (acc[:, :MLA_V] / acc[:, MLA_V:]).astype(o_ref.dtype)


def _mla_attention(q, k, v):
    s = q.shape[0]
    t = min(512, s)
    n = s // t
    hps = FLASH_HEADS
    return pl.pallas_call(
        _flash_kernel,
        grid=(MLA_HEADS // hps, n, n),
        in_specs=[pl.BlockSpec((t, hps * MLA_QK_PAD), lambda h, i, j: (i, h)),
                  pl.BlockSpec((t, hps * MLA_QK_PAD), lambda h, i, j: (jnp.minimum(j, i), h)),
                  pl.BlockSpec((t, hps * MLA_V), lambda h, i, j: (jnp.minimum(j, i), h))],
        out_specs=pl.BlockSpec((t, hps * MLA_V), lambda h, i, j: (i, h)),
        out_shape=jax.ShapeDtypeStruct((s, MLA_HEADS * MLA_V), BF16),
        scratch_shapes=[pltpu.VMEM((hps, t, LANES), F32), pltpu.VMEM((hps, t, 2 * MLA_V), F32)],
        compiler_params=_cparams("parallel", "parallel", "arbitrary"),
        name="mla_flash",
    )(q, k, v)


def _gdn_conv_kernel(x_ref, prev_ref, w_ref, o_ref, buf, *, tm, tc, halo, n_qk_blocks):
    c = pl.program_id(0)
    i = pl.program_id(1)
    prev = prev_ref[...].astype(F32)
    buf[0:halo, :] = jnp.where(i > 0, prev, 0.0)
    buf[halo:halo + tm, :] = x_ref[...].astype(F32)
    y = jnp.zeros((tm, tc), F32)
    for j in range(GDN_CONV):
        y = y + w_ref[j:j + 1, :] * buf[pl.ds(halo - (GDN_CONV - 1) + j, tm), :]
    y = y * _sigmoid(y)
    is_qk = c < n_qk_blocks
    for g in range(tc // LANES):
        seg = y[:, g * LANES:(g + 1) * LANES]
        nrm = seg * lax.rsqrt(jnp.sum(seg * seg, axis=-1, keepdims=True) + 1e-6)
        o_ref[:, g * LANES:(g + 1) * LANES] = jnp.where(is_qk, nrm, seg).astype(o_ref.dtype)


def _gdn_conv(xqkv, conv_w):
    s, c = xqkv.shape
    tm = min(512, s)
    tc = 512
    halo = 16
    n_qk_blocks = (2 * GDN_QK_HEADS * GDN_DK) // tc
    kern = functools.partial(_gdn_conv_kernel, tm=tm, tc=tc, halo=halo, n_qk_blocks=n_qk_blocks)
    return pl.pallas_call(
        kern,
        grid=(c // tc, s // tm),
        in_specs=[pl.BlockSpec((tm, tc), lambda cc, i: (i, cc)),
                  pl.BlockSpec((halo, tc), lambda cc, i: (jnp.maximum(i * (tm // halo) - 1, 0), cc)),
                  pl.BlockSpec((GDN_CONV, tc), lambda cc, i: (0, cc))],
        out_specs=pl.BlockSpec((tm, tc), lambda cc, i: (i, cc)),
        out_shape=jax.ShapeDtypeStruct((s, c), BF16),
        scratch_shapes=[pltpu.VMEM((tm + halo, tc), F32)],
        compiler_params=_cparams("parallel", "parallel"),
        name="gdn_conv",
    )(xqkv, xqkv, conv_w)


def _softplus(x):
    return jnp.maximum(x, 0.0) + jnp.log1p(jnp.exp(-jnp.abs(x)))


def _gdn_chunk_kernel(q_ref, k_ref, v_ref, z_ref, ac_ref, bc_ref, ar_ref, alc_ref, dtc_ref, alr_ref, dtr_ref,
                      nw_ref, o_ref, state):
    n = pl.program_id(1)
    g_heads = GDN_GROUP
    c = CHUNK

    @pl.when(n == 0)
    def _():
        state[...] = jnp.zeros(state.shape, F32)

    ri = lax.broadcasted_iota(jnp.int32, (c, c), 0)
    ci = lax.broadcasted_iota(jnp.int32, (c, c), 1)
    tril = ci <= ri
    strict = ci < ri
    ltri = tril.astype(F32)
    utri = (ri <= ci).astype(F32)
    eye = (ri == ci).astype(F32)

    g_col = -jnp.exp(alc_ref[...]) * _softplus(ac_ref[...] + dtc_ref[...])
    gc_col = _dot(ltri, g_col, precision=HIGHEST)
    g_row = -jnp.exp(alr_ref[...]) * _softplus(ar_ref[...] + dtr_ref[...])
    gc_row = _dot(g_row, utri, precision=HIGHEST)
    beta_col = _sigmoid(bc_ref[...])
    nw = nw_ref[...]
    scale = float(GDN_DK ** -0.5)

    vheads = range(g_heads)
    qkheads = range(g_heads // 2)
    qs = [q_ref[:, h * GDN_DK:(h + 1) * GDN_DK] for h in qkheads]
    ks = [k_ref[:, h * GDN_DK:(h + 1) * GDN_DK] for h in qkheads]
    kqs = [_dot_nt(jnp.concatenate([ks[h], qs[h]], axis=0), ks[h]) for h in qkheads]
    bcol = [beta_col[:, h:h + 1] for h in vheads]
    gcol = [gc_col[:, h:h + 1] for h in vheads]
    glast = [gc_col[c - 1:c, h:h + 1] for h in vheads]
    decay = [jnp.where(tril, jnp.exp(jnp.where(tril, gcol[h] - gc_row[h:h + 1, :], 0.0)), 0.0) for h in vheads]
    m = [jnp.where(strict, kqs[h // 2][:c] * bcol[h] * decay[h], 0.0) for h in vheads]
    a_qk = [(jnp.where(tril, kqs[h // 2][c:] * decay[h], 0.0) * scale).astype(BF16) for h in vheads]
    t_inv = [eye - m[h] for h in vheads]
    xp = [m[h].astype(BF16) for h in vheads]
    for _ in range(5):
        xp = [_dot(x, x) for x in xp]
        xp = [x.astype(BF16) for x in xp]
        t_inv = [t + _dot(t.astype(BF16), x) for t, x in zip(t_inv, xp)]
    egc = [jnp.exp(gcol[h]) for h in vheads]
    kf = [ks[h].astype(F32) for h in qkheads]
    rhs = [jnp.concatenate([v_ref[:, h * GDN_DV:(h + 1) * GDN_DV].astype(F32) * bcol[h],
                            kf[h // 2] * (bcol[h] * egc[h])], axis=1).astype(BF16) for h in vheads]
    uw = [_dot(t_inv[h].astype(BF16), rhs[h]) for h in vheads]
    st = [state[h] for h in vheads]
    lhs = [jnp.concatenate([uw[h][:, GDN_DV:], qs[h // 2].astype(F32) * egc[h]], axis=0).astype(BF16)
           for h in vheads]
    ws = [_dot(lhs[h], st[h].astype(BF16)) for h in vheads]
    v_new = [(uw[h][:, :GDN_DV] - ws[h][:c]).astype(BF16) for h in vheads]
    kdec = [(kf[h // 2] * jnp.exp(glast[h] - gcol[h])).astype(BF16) for h in vheads]
    o = [ws[h][c:] * scale + _dot(a_qk[h], v_new[h]) for h in vheads]
    for h in vheads:
        state[h] = st[h] * jnp.exp(glast[h]) + _dot_tn(kdec[h], v_new[h])
    for h in vheads:
        on = o[h] * lax.rsqrt(jnp.mean(o[h] * o[h], axis=-1, keepdims=True) + RMS_EPS) * nw
        zz = z_ref[:, h * GDN_DV:(h + 1) * GDN_DV].astype(F32)
        o_ref[:, h * GDN_DV:(h + 1) * GDN_DV] = (on * (zz * _sigmoid(zz))).astype(o_ref.dtype)


def _gdn_delta(qkv, z, a, b, a_log, dt_bias, norm_w):
    s = qkv.shape[0]
    g = GDN_GROUP
    ng = GDN_V_HEADS // g
    nc = s // CHUNK
    gq = g // 2
    wq = gq * GDN_DK
    wv = g * GDN_DV
    k_off = (GDN_QK_HEADS * GDN_DK) // wq
    v_off = (2 * GDN_QK_HEADS * GDN_DK) // wv
    a_col = a.reshape(s, ng, g).transpose(1, 0, 2)
    b_col = b.reshape(s, ng, g).transpose(1, 0, 2)
    a_row = a.reshape(nc, CHUNK, ng, g).transpose(2, 0, 3, 1)
    al_c = a_log.reshape(ng, 1, g)
    dt_c = dt_bias.reshape(ng, 1, g)
    al_r = a_log.reshape(ng, g, 1)
    dt_r = dt_bias.reshape(ng, g, 1)
    return pl.pallas_call(
        _gdn_chunk_kernel,
        grid=(ng, nc),
        in_specs=[pl.BlockSpec((CHUNK, wq), lambda hg, n: (n, hg)),
                  pl.BlockSpec((CHUNK, wq), lambda hg, n: (n, k_off + hg)),
                  pl.BlockSpec((CHUNK, wv), lambda hg, n: (n, v_off + hg)),
                  pl.BlockSpec((CHUNK, wv), lambda hg, n: (n, hg)),
                  pl.BlockSpec((None, CHUNK, g), lambda hg, n: (hg, n, 0)),
                  pl.BlockSpec((None, CHUNK, g), lambda hg, n: (hg, n, 0)),
                  pl.BlockSpec((None, None, g, CHUNK), lambda hg, n: (hg, n, 0, 0)),
                  pl.BlockSpec((None, 1, g), lambda hg, n: (hg, 0, 0)),
                  pl.BlockSpec((None, 1, g), lambda hg, n: (hg, 0, 0)),
                  pl.BlockSpec((None, g, 1), lambda hg, n: (hg, 0, 0)),
                  pl.BlockSpec((None, g, 1), lambda hg, n: (hg, 0, 0)),
                  pl.BlockSpec((1, GDN_DV), lambda hg, n: (0, 0))],
        out_specs=pl.BlockSpec((CHUNK, wv), lambda hg, n: (n, hg)),
        out_shape=jax.ShapeDtypeStruct((s, GDN_V_HEADS * GDN_DV), BF16),
        scratch_shapes=[pltpu.VMEM((g, GDN_DK, GDN_DV), F32)],
        compiler_params=_cparams("parallel", "arbitrary"),
        name="gdn_delta",
    )(qkv, qkv, qkv, z, a_col, b_col, a_row, al_c, dt_c, al_r, dt_r, norm_w)


def _branch_kernel(om_ref, og_ref, wm_ref, wg_ref, gm_ref, gg_ref, bm_ref, bg_ref, o_ref):
    ym = _dot(om_ref[...], wm_ref[...])
    yg = _dot(og_ref[...], wg_ref[...])
    sm = _sigmoid(gm_ref[...].astype(F32) + bm_ref[...])
    sg = _sigmoid(gg_ref[...].astype(F32) + bg_ref[...])
    o_ref[...] = (sm * ym + sg * yg).astype(o_ref.dtype)


def _branch_merge(o_mla, o_gdn, w_o_mla, w_o_gdn, gates, b_gate):
    s = o_mla.shape[0]
    d = w_o_mla.shape[1]
    tm = min(512, s)
    tn = min(512, d)
    nb = d // tn
    return pl.pallas_call(
        _branch_kernel,
        grid=(nb, s // tm),
        in_specs=[pl.BlockSpec((tm, o_mla.shape[1]), lambda j, i: (i, 0)),
                  pl.BlockSpec((tm, o_gdn.shape[1]), lambda j, i: (i, 0)),
                  pl.BlockSpec((w_o_mla.shape[0], tn), lambda j, i: (0, j)),
                  pl.BlockSpec((w_o_gdn.shape[0], tn), lambda j, i: (0, j)),
                  pl.BlockSpec((tm, tn), lambda j, i: (i, j)),
                  pl.BlockSpec((tm, tn), lambda j, i: (i, nb + j)),
                  pl.BlockSpec((1, tn), lambda j, i: (0, j)),
                  pl.BlockSpec((1, tn), lambda j, i: (0, nb + j))],
        out_specs=pl.BlockSpec((tm, tn), lambda j, i: (i, j)),
        out_shape=jax.ShapeDtypeStruct((s, d), BF16),
        compiler_params=_cparams("parallel", "parallel"),
        name="branch_merge",
    )(o_mla, o_gdn, w_o_mla, w_o_gdn, gates, gates, b_gate, b_gate)


def _out_ln_kernel(m_ref, w_ref, x_ref, g_ref, b_ref, o_ref):
    y = DN_ALPHA * x_ref[...] + _dot(m_ref[...], w_ref[...])
    o_ref[...] = _layernorm(y, g_ref[...], b_ref[...])


def _out_ln(mixed, w_out, x, g, b):
    s, d = x.shape
    tm = min(256, s)
    return pl.pallas_call(
        _out_ln_kernel,
        grid=(s // tm,),
        in_specs=[pl.BlockSpec((tm, d), lambda i: (i, 0)),
                  pl.BlockSpec((d, d), lambda i: (0, 0)),
                  pl.BlockSpec((tm, d), lambda i: (i, 0)),
                  pl.BlockSpec((1, d), lambda i: (0, 0)),
                  pl.BlockSpec((1, d), lambda i: (0, 0))],
        out_specs=pl.BlockSpec((tm, d), lambda i: (i, 0)),
        out_shape=jax.ShapeDtypeStruct((s, d), F32),
        compiler_params=_cparams("parallel"),
        name="out_ln1",
    )(mixed, w_out, x, g, b)


def _mem_kernel(x_ref, wq_ref, kv_ref, wo_ref, g_ref, b_ref, wr_ref, br_ref, x2_ref, lg_ref):
    x1 = x_ref[...]
    hd = MEM_HEAD_DIM
    nh = MEM_HEADS
    q = (_dot(x1.astype(BF16), wq_ref[...]) * float(hd ** -0.5)).astype(BF16)
    outs = []
    for h in range(nh):
        kh = kv_ref[:, h * hd:(h + 1) * hd]
        vh = kv_ref[:, (nh + h) * hd:(nh + h + 1) * hd]
        s = _dot_nt(q[:, h * hd:(h + 1) * hd], kh)
        p = jnp.exp(s - jnp.max(s, axis=-1, keepdims=True))
        o = _dot(p.astype(BF16), vh) / jnp.sum(p, axis=-1, keepdims=True)
        outs.append(o.astype(BF16))
    o = jnp.concatenate(outs, axis=1)
    y = DN_ALPHA * x1 + _dot(o, wo_ref[...])
    x2 = _layernorm(y, g_ref[...], b_ref[...])
    x2_ref[...] = x2
    lg_ref[...] = _dot(x2, wr_ref[...], precision=HIGHEST) + br_ref[...]


def _mem_attn_ln(x1, w_mq, kvm, w_mo, g, b, w_r, b_r):
    s, d = x1.shape
    tm = min(256, s)
    return pl.pallas_call(
        _mem_kernel,
        grid=(s // tm,),
        in_specs=[pl.BlockSpec((tm, d), lambda i: (i, 0)),
                  pl.BlockSpec(w_mq.shape, lambda i: (0, 0)),
                  pl.BlockSpec(kvm.shape, lambda i: (0, 0)),
                  pl.BlockSpec(w_mo.shape, lambda i: (0, 0)),
                  pl.BlockSpec((1, d), lambda i: (0, 0)),
                  pl.BlockSpec((1, d), lambda i: (0, 0)),
                  pl.BlockSpec(w_r.shape, lambda i: (0, 0)),
                  pl.BlockSpec((1, LANES), lambda i: (0, 0))],
        out_specs=[pl.BlockSpec((tm, d), lambda i: (i, 0)),
                   pl.BlockSpec((tm, LANES), lambda i: (i, 0))],
        out_shape=[jax.ShapeDtypeStruct((s, d), F32),
                   jax.ShapeDtypeStruct((s, LANES), F32)],
        compiler_params=_cparams("parallel"),
        name="mem_attn_ln2",
    )(x1, w_mq, kvm, w_mo, g, b, w_r, b_r)


SEL_E1, SEL_E2, SEL_R1, SEL_R2, SEL_G1, SEL_G2 = range(6)
GRP_LANE0 = N_EXPERTS


def _route_kernel(lg_ref, sel_ref, cnt_ref, carry):
    i = pl.program_id(0)

    @pl.when(i == 0)
    def _():
        carry[...] = jnp.zeros(carry.shape, F32)

    lg = lg_ref[...]
    tm = lg.shape[0]
    lane = lax.broadcasted_iota(jnp.int32, lg.shape, 1)
    big = jnp.int32(4 * LANES)
    neg = jnp.float32(-jnp.inf)

    def first_max(vals):
        mx = jnp.max(vals, axis=-1, keepdims=True)
        idx = jnp.min(jnp.where(vals == mx, lane, big), axis=-1, keepdims=True)
        return mx, idx

    is_grp = (lane >= GRP_LANE0) & (lane < GRP_LANE0 + N_GROUPS)
    gl = jnp.where(is_grp, lg, neg)
    gmax, gidx = first_max(gl)
    p_top = 1.0 / jnp.sum(jnp.where(is_grp, jnp.exp(gl - gmax), 0.0), axis=-1, keepdims=True)
    lo = (gidx - GRP_LANE0) * EXPERTS_PER_GROUP
    in_grp = (lane >= lo) & (lane < lo + EXPERTS_PER_GROUP)
    el = jnp.where(in_grp, lg, neg)
    m1, i1 = first_max(el)
    m2, i2 = first_max(jnp.where(lane == i1, neg, el))
    r = jnp.exp(m2 - m1)
    g1 = p_top / (1.0 + r)
    g2 = p_top * r / (1.0 + r)

    hot1 = lane == i1
    hot2 = lane == i2
    onehot = jnp.where(hot1, 1.0, 0.0) + jnp.where(hot2, 1.0, 0.0)
    ri = lax.broadcasted_iota(jnp.int32, (tm, tm), 0)
    ci = lax.broadcasted_iota(jnp.int32, (tm, tm), 1)
    before = jnp.where(ci < ri, 1.0, 0.0).astype(BF16)
    rank = _dot(before, onehot.astype(BF16)) + carry[...]
    r1 = jnp.sum(jnp.where(hot1, rank, 0.0), axis=-1, keepdims=True)
    r2 = jnp.sum(jnp.where(hot2, rank, 0.0), axis=-1, keepdims=True)
    carry[...] = carry[...] + jnp.sum(onehot, axis=0, keepdims=True)
    cnt_ref[...] = carry[...]

    out = jnp.zeros(lg.shape, F32)
    for ln, val in ((SEL_E1, i1.astype(F32)), (SEL_E2, i2.astype(F32)), (SEL_R1, r1), (SEL_R2, r2),
                    (SEL_G1, g1), (SEL_G2, g2)):
        out = jnp.where(lane == ln, val, out)
    sel_ref[...] = out


def _route(logits):
    t = logits.shape[0]
    tm = min(512, t)
    return pl.pallas_call(
        _route_kernel,
        grid=(t // tm,),
        in_specs=[pl.BlockSpec((tm, LANES), lambda i: (i, 0))],
        out_specs=[pl.BlockSpec((tm, LANES), lambda i: (i, 0)),
                   pl.BlockSpec((1, LANES), lambda i: (0, 0))],
        out_shape=[jax.ShapeDtypeStruct((t, LANES), F32),
                   jax.ShapeDtypeStruct((1, LANES), F32)],
        scratch_shapes=[pltpu.VMEM((1, LANES), F32)],
        compiler_params=_cparams("arbitrary"),
        name="moe_route",
    )(logits)


def _dispatch_kernel(cnt_ref, e1_ref, e2_ref, r1_ref, r2_ref, rowa_ref, blk_ref, rows_ref, pstart, *, t, n_blocks):
    def seg(e, start):
        pstart[e] = start
        c = cnt_ref[e]
        nb = (c + MOE_ROWS - 1) // MOE_ROWS
        end = start + nb * MOE_ROWS

        def mark(p, carry):
            rowa_ref[p] = -1
            return carry
        lax.fori_loop(start + c, end, mark, 0)

        def blk(bi, carry):
            blk_ref[bi] = e
            rows_ref[bi] = jnp.minimum(start + c - bi * MOE_ROWS, MOE_ROWS)
            return carry
        lax.fori_loop(start // MOE_ROWS, end // MOE_ROWS, blk, 0)
        return end

    total = lax.fori_loop(0, N_EXPERTS, seg, 0)
    used = total // MOE_ROWS
    last = blk_ref[jnp.maximum(used - 1, 0)]

    def tail(bi, carry):
        blk_ref[bi] = last
        rows_ref[bi] = 0
        return carry
    lax.fori_loop(used, n_blocks, tail, 0)

    def unused(p, carry):
        rowa_ref[p] = -1
        return carry
    lax.fori_loop(total, n_blocks * MOE_ROWS, unused, 0)

    def place(tok, carry):
        rowa_ref[pstart[e1_ref[tok]] + r1_ref[tok]] = 2 * tok
        rowa_ref[pstart[e2_ref[tok]] + r2_ref[tok]] = 2 * tok + 1
        return carry
    lax.fori_loop(0, t, place, 0, unroll=8)


def _dispatch(cnt, e1, e2, r1, r2, n_blocks):
    t = e1.shape[0]
    smem = pl.BlockSpec(memory_space=pltpu.SMEM)
    return pl.pallas_call(
        functools.partial(_dispatch_kernel, t=t, n_blocks=n_blocks),
        in_specs=[smem] * 5,
        out_specs=[smem] * 3,
        out_shape=[jax.ShapeDtypeStruct((n_blocks * MOE_ROWS,), jnp.int32),
                   jax.ShapeDtypeStruct((n_blocks,), jnp.int32),
                   jax.ShapeDtypeStruct((n_blocks,), jnp.int32)],
        scratch_shapes=[pltpu.SMEM((N_EXPERTS,), jnp.int32)],
        name="moe_dispatch",
    )(cnt, e1, e2, r1, r2)


def _expert_kernel(blk_ref, rowa_ref, rows_ref, x_hbm, wg_ref, wu_ref, wd_ref, out_hbm, xbuf, ybuf, gsem, ssem):
    b = pl.program_id(0)
    base = b * MOE_ROWS
    n_rows = rows_ref[b]

    def gather_copy(r, tok):
        return pltpu.make_async_copy(x_hbm.at[pl.ds(tok, 1), :], xbuf.at[pl.ds(r, 1), :], gsem)

    def scatter_copy(r, dst):
        return pltpu.make_async_copy(ybuf.at[pl.ds(r, 1), :], out_hbm.at[pl.ds(dst, 1), :], ssem)

    @pl.when(n_rows > 0)
    def _():
        def issue_gather(r, carry):
            a = rowa_ref[base + r]
            gather_copy(r, jnp.maximum(a, 0) // 2).start()
            return carry
        lax.fori_loop(0, MOE_ROWS, issue_gather, 0, unroll=8)
        pltpu.make_async_copy(x_hbm.at[pl.ds(0, MOE_ROWS), :], xbuf, gsem).wait()

        xb = xbuf[...].astype(BF16)
        hg = _dot(xb, wg_ref[...].astype(BF16))
        hu = _dot(xb, wu_ref[...].astype(BF16))
        hb = (hg * _sigmoid(hg) * hu).astype(BF16)
        ybuf[...] = _dot(hb, wd_ref[...].astype(BF16))

        def issue_scatter(r, carry):
            scatter_copy(r, rowa_ref[base + r]).start()
            return carry
        lax.fori_loop(0, n_rows, issue_scatter, 0)
        n_tiled = pl.multiple_of((n_rows // 8) * 8, 8)

        @pl.when(n_tiled > 0)
        def _():
            pltpu.make_async_copy(ybuf.at[pl.ds(0, n_tiled), :], out_hbm.at[pl.ds(0, n_tiled), :], ssem).wait()

        def wait_scatter(r, carry):
            scatter_copy(r, 0).wait()
            return carry
        lax.fori_loop(n_tiled, n_rows, wait_scatter, 0)


def _experts(x2, row_a, blk_exp, blk_rows, w_gate_e, w_up_e, w_down_e):
    t, d = x2.shape
    n_blocks = blk_exp.shape[0]
    de = w_gate_e.shape[2]
    grid_spec = pltpu.PrefetchScalarGridSpec(
        num_scalar_prefetch=3,
        grid=(n_blocks,),
        in_specs=[pl.BlockSpec(memory_space=pl.ANY),
                  pl.BlockSpec((None, d, de), lambda b, blk, rowa, nb: (blk[b], 0, 0)),
                  pl.BlockSpec((None, d, de), lambda b, blk, rowa, nb: (blk[b], 0, 0)),
                  pl.BlockSpec((None, de, d), lambda b, blk, rowa, nb: (blk[b], 0, 0))],
        out_specs=pl.BlockSpec(memory_space=pl.ANY),
        scratch_shapes=[pltpu.VMEM((MOE_ROWS, d), F32), pltpu.VMEM((MOE_ROWS, d), F32),
                        pltpu.SemaphoreType.DMA(()), pltpu.SemaphoreType.DMA(())],
    )
    return pl.pallas_call(
        _expert_kernel,
        grid_spec=grid_spec,
        out_shape=jax.ShapeDtypeStruct((2 * t, d), F32),
        compiler_params=_cparams("arbitrary"),
        name="moe_experts",
    )(blk_exp, row_a, blk_rows, x2, w_gate_e, w_up_e, w_down_e)


def _combine_kernel(x_ref, y_ref, sel_ref, g_ref, b_ref, o_ref):
    d = x_ref.shape[1]
    sel = sel_ref[...]
    g1 = sel[:, SEL_G1:SEL_G1 + 1]
    g2 = sel[:, SEL_G2:SEL_G2 + 1]
    y = DN_ALPHA * x_ref[...] + (g1 * y_ref[:, :d] + g2 * y_ref[:, d:])
    o_ref[...] = _layernorm(y, g_ref[...], b_ref[...])


def _combine_ln(x2, ys2, sel, g, b):
    t, d = x2.shape
    tm = min(256, t)
    return pl.pallas_call(
        _combine_kernel,
        grid=(t // tm,),
        in_specs=[pl.BlockSpec((tm, d), lambda i: (i, 0)),
                  pl.BlockSpec((tm, 2 * d), lambda i: (i, 0)),
                  pl.BlockSpec((tm, LANES), lambda i: (i, 0)),
                  pl.BlockSpec((1, d), lambda i: (0, 0)),
                  pl.BlockSpec((1, d), lambda i: (0, 0))],
        out_specs=pl.BlockSpec((tm, d), lambda i: (i, 0)),
        out_shape=jax.ShapeDtypeStruct((t, d), F32),
        compiler_params=_cparams("parallel"),
        name="combine_ln3",
    )(x2, ys2, sel, g, b)


def _mixer(xb, positions, w_in, b_gate, mla_q_norm, w_uq, mla_kv_norm, w_ukv, w_o_mla,
           gdn_conv, gdn_a_log, gdn_dt_bias, gdn_norm, w_o_gdn):
    s, d = xb.shape
    nqkv = 2 * GDN_QK_HEADS * GDN_DK + GDN_V_HEADS * GDN_DV
    nz = GDN_V_HEADS * GDN_DV
    o0 = 0
    o1 = o0 + MLA_Q_LORA
    o2 = o1 + MLA_KV_LORA + MLA_ROPE
    o3 = o2 + nqkv
    o4 = o3 + nz
    o5 = o4 + GDN_V_HEADS
    o6 = o5 + GDN_V_HEADS

    def rot_cols(w):
        half = MLA_ROPE // 2
        return jnp.concatenate([-w[..., half:], w[..., :half]], axis=-1)

    w_q = w_in[:, o0:o1].astype(BF16)
    w_kpe = w_in[:, o1 + MLA_KV_LORA:o2]
    w_kv = jnp.concatenate([w_in[:, o1:o1 + MLA_KV_LORA], w_kpe, rot_cols(w_kpe)], axis=1).astype(BF16)
    w_ba = jnp.concatenate([w_in[:, o4:o6], jnp.zeros((d, LANES - 2 * GDN_V_HEADS), F32)], axis=1).astype(BF16)
    qd = _matmul(xb, w_q, BF16, 512, MLA_Q_LORA)
    kvd = _matmul(xb, w_kv, BF16, 512, w_kv.shape[1])
    hqkv = _matmul(xb, w_in[:, o2:o3].astype(BF16), BF16, 512, 1024)
    hz = _matmul(xb, w_in[:, o3:o4].astype(BF16), BF16, 512, 1024)
    hba = _matmul(xb, w_ba, F32, 512, LANES)
    hgate = _matmul(xb, w_in[:, o6:].astype(BF16), BF16, 512, 1024)

    cs = _rope_table(positions)
    hq = MLA_NOPE + MLA_ROPE
    wq3 = w_uq.reshape(MLA_Q_LORA, MLA_HEADS, hq)
    pe = wq3[..., MLA_NOPE:]
    wq = jnp.concatenate([pe, rot_cols(pe), wq3[..., :MLA_NOPE]], axis=-1)
    wq = wq.reshape(MLA_Q_LORA, MLA_HEADS * MLA_QK_PAD).astype(BF16)
    wkv3 = w_ukv.reshape(MLA_KV_LORA, MLA_HEADS, MLA_NOPE + MLA_V)
    wk = wkv3[..., :MLA_NOPE].reshape(MLA_KV_LORA, MLA_HEADS * MLA_NOPE).astype(BF16)
    wv = wkv3[..., MLA_NOPE:].reshape(MLA_KV_LORA, MLA_HEADS * MLA_V).astype(BF16)
    q = _mla_q_proj(qd, mla_q_norm.reshape(1, -1), wq, cs)
    k, v = _mla_kv_proj(kvd, mla_kv_norm.reshape(1, -1), wk, wv, cs)
    o_mla = _mla_attention(q, k, v)

    qkv_c = _gdn_conv(hqkv, gdn_conv)
    o_gdn = _gdn_delta(qkv_c, hz, hba[:, GDN_V_HEADS:2 * GDN_V_HEADS], hba[:, :GDN_V_HEADS],
                       gdn_a_log, gdn_dt_bias, gdn_norm.reshape(1, -1))

    return _branch_merge(o_mla, o_gdn, w_o_mla.astype(BF16), w_o_gdn.astype(BF16), hgate, b_gate.reshape(1, -1))


def _moe(x2, logits, w_gate_e, w_up_e, w_down_e, ln_g, ln_b):
    t, d = x2.shape
    n_blocks = (2 * t) // MOE_ROWS + N_EXPERTS
    sel, cnt = _route(logits)
    as_i32 = lambda col: sel[:, col].astype(jnp.int32)
    row_a, blk_exp, blk_rows = _dispatch(cnt[0, :N_EXPERTS].astype(jnp.int32), as_i32(SEL_E1), as_i32(SEL_E2),
                                     as_i32(SEL_R1), as_i32(SEL_R2), n_blocks)
    ys = _experts(x2, row_a, blk_exp, blk_rows, w_gate_e, w_up_e, w_down_e)
    ys2 = ys.reshape(t, 2 * d)
    return _combine_ln(x2, ys2, sel, ln_g, ln_b)


def _layer(x, mem, positions, w_in, b_gate, mla_q_norm, w_uq, mla_kv_norm, w_ukv, w_o_mla,
           gdn_conv, gdn_a_log, gdn_dt_bias, gdn_norm, w_o_gdn, w_out, ln1_g, ln1_b,
           w_mq, w_mkv, w_mo, ln2_g, ln2_b, w_route_grp, b_route_grp, w_route_exp, b_route_exp,
           w_gate_e, w_up_e, w_down_e, ln3_g, ln3_b):
    d = x.shape[1]
    row = lambda p: p.reshape(1, -1)
    mixed = _mixer(x.astype(BF16), positions, w_in, b_gate, mla_q_norm, w_uq, mla_kv_norm, w_ukv, w_o_mla,
                   gdn_conv, gdn_a_log, gdn_dt_bias, gdn_norm, w_o_gdn)
    x1 = _out_ln(mixed, w_out.astype(BF16), x, row(ln1_g), row(ln1_b))

    kvm = _matmul(mem.astype(BF16), w_mkv.astype(BF16), BF16, 256, 512)
    pad = LANES - N_EXPERTS - N_GROUPS
    w_r = jnp.concatenate([w_route_exp, w_route_grp, jnp.zeros((d, pad), F32)], axis=1)
    b_r = jnp.concatenate([b_route_exp, b_route_grp, jnp.zeros((pad,), F32)]).reshape(1, LANES)
    x2, logits = _mem_attn_ln(x1, w_mq.astype(BF16), kvm, w_mo.astype(BF16), row(ln2_g), row(ln2_b), w_r, b_r)

    return _moe(x2, logits, w_gate_e, w_up_e, w_down_e, row(ln3_g), row(ln3_b))


def kernel(x, mem, positions, w_in, b_gate, mla_q_norm, w_uq, mla_kv_norm, w_ukv, w_o_mla, gdn_conv, gdn_a_log,
           gdn_dt_bias, gdn_norm, w_o_gdn, w_out, ln1_g, ln1_b, w_mq, w_mkv, w_mo, ln2_g, ln2_b, w_route_grp,
           b_route_grp, w_route_exp, b_route_exp, w_gate_e, w_up_e, w_down_e, ln3_g, ln3_b):
    outs = []
    for bi in range(x.shape[0]):
        h = x[bi]
        for l in range(w_in.shape[0]):
            h = _layer(h, mem[bi], positions[bi], w_in[l], b_gate[l], mla_q_norm[l], w_uq[l], mla_kv_norm[l],
                       w_ukv[l], w_o_mla[l], gdn_conv[l], gdn_a_log[l], gdn_dt_bias[l], gdn_norm[l], w_o_gdn[l],
                       w_out[l], ln1_g[l], ln1_b[l], w_mq[l], w_mkv[l], w_mo[l], ln2_g[l], ln2_b[l],
                       w_route_grp[l], b_route_grp[l], w_route_exp[l], b_route_exp[l], w_gate_e[l], w_up_e[l],
                       w_down_e[l], ln3_g[l], ln3_b[l])
        outs.append(h)
    return jnp.stack(outs, axis=0)
```

```python
import functools

import numpy as np
import jax
import jax.numpy as jnp
from jax import lax
from jax.experimental import pallas as pl
from jax.experimental.pallas import tpu as pltpu

F32 = jnp.float32
BF16 = jnp.bfloat16
HIGHEST = lax.Precision.HIGHEST

LANES = 128
VMEM_LIMIT = 56 * 1024 * 1024

CHUNK = 64
MLA_HEADS = 16
MLA_Q_LORA = 768
MLA_KV_LORA = 512
MLA_NOPE = 128
MLA_ROPE = 64
MLA_V = 128
MLA_QK_PAD = 256
ROPE_THETA = 10000.0
GDN_QK_HEADS = 16
GDN_V_HEADS = 32
GDN_DK = 128
GDN_DV = 128
GDN_CONV = 4
GDN_GROUP = 8
MEM_HEADS = 4
MEM_HEAD_DIM = 128
N_GROUPS = 8
EXPERTS_PER_GROUP = 8
N_EXPERTS = 64
D_EXPERT = 512
MOE_ROWS = 256
RMS_EPS = 1e-6
LN_EPS = 1e-5
DN_ALPHA = 2.0 ** 0.25


def _cparams(*sem):
    return pltpu.CompilerParams(dimension_semantics=sem, vmem_limit_bytes=VMEM_LIMIT)


def _dot(a, b, **kw):
    return jnp.dot(a, b, preferred_element_type=F32, **kw)


def _dot_nt(a, b):
    return lax.dot_general(a, b, (((1,), (1,)), ((), ())), preferred_element_type=F32)


def _dot_tn(a, b):
    return lax.dot_general(a, b, (((0,), (0,)), ((), ())), preferred_element_type=F32)


def _sigmoid(x):
    return 1.0 / (1.0 + jnp.exp(-x))


def _layernorm(y, g, b):
    mu = jnp.mean(y, axis=-1, keepdims=True)
    d = y - mu
    var = jnp.mean(d * d, axis=-1, keepdims=True)
    return d * lax.rsqrt(var + LN_EPS) * g + b


def _mm_kernel(x_ref, w_ref, o_ref):
    o_ref[...] = _dot(x_ref[...], w_ref[...]).astype(o_ref.dtype)


def _matmul(x, w, out_dtype, tm, tn):
    m, k = x.shape
    n = w.shape[1]
    tm, tn = min(tm, m), min(tn, n)
    return pl.pallas_call(
        _mm_kernel,
        grid=(n // tn, m // tm),
        in_specs=[pl.BlockSpec((tm, k), lambda j, i: (i, 0)),
                  pl.BlockSpec((k, tn), lambda j, i: (0, j))],
        out_specs=pl.BlockSpec((tm, tn), lambda j, i: (i, j)),
        out_shape=jax.ShapeDtypeStruct((m, n), out_dtype),
        compiler_params=_cparams("parallel", "parallel"),
        name="matmul",
    )(x, w)


def _rope_table_kernel(pos_ref, inv_ref, o_ref):
    ang = pos_ref[...].astype(F32) * inv_ref[...]
    lane = lax.broadcasted_iota(jnp.int32, ang.shape, 1)
    o_ref[...] = jnp.where(lane < MLA_ROPE, jnp.cos(ang), jnp.sin(ang))


def _rope_table(positions):
    s = positions.shape[0]
    inv = 1.0 / (ROPE_THETA ** (np.arange(0, MLA_ROPE, 2, dtype=np.float32) / MLA_ROPE))
    inv4 = jnp.asarray(np.tile(inv.astype(np.float32), 4)[None, :])
    tm = min(512, s)
    return pl.pallas_call(
        _rope_table_kernel,
        grid=(s // tm,),
        in_specs=[pl.BlockSpec((tm, 1), lambda i: (i, 0)),
                  pl.BlockSpec((1, LANES), lambda i: (0, 0))],
        out_specs=pl.BlockSpec((tm, LANES), lambda i: (i, 0)),
        out_shape=jax.ShapeDtypeStruct((s, LANES), F32),
        compiler_params=_cparams("parallel"),
        name="rope_table",
    )(positions.reshape(s, 1), inv4)


def _rope_pair(t, cs):
    a = t * cs
    return a + pltpu.roll(a, MLA_ROPE, axis=1)


def _mla_q_kernel(qd_ref, qn_ref, w_ref, cs_ref, o_ref, *, scale):
    x = qd_ref[...].astype(F32)
    cq = x * lax.rsqrt(jnp.mean(x * x, axis=-1, keepdims=True) + RMS_EPS) * qn_ref[...]
    cqb = cq.astype(BF16)
    cs = cs_ref[...]
    for h in range(MLA_HEADS):
        lo = h * MLA_QK_PAD
        p = _dot(cqb, w_ref[:, lo:lo + MLA_QK_PAD])
        o_ref[:, lo:lo + LANES] = (_rope_pair(p[:, :LANES], cs) * scale).astype(o_ref.dtype)
        o_ref[:, lo + LANES:lo + MLA_QK_PAD] = (p[:, LANES:] * scale).astype(o_ref.dtype)


def _mla_q_proj(qd, q_norm, wq, cs):
    s = qd.shape[0]
    tm = min(256, s)
    n = MLA_HEADS * MLA_QK_PAD
    scale = float((MLA_NOPE + MLA_ROPE) ** -0.5)
    return pl.pallas_call(
        functools.partial(_mla_q_kernel, scale=scale),
        grid=(s // tm,),
        in_specs=[pl.BlockSpec((tm, MLA_Q_LORA), lambda i: (i, 0)),
                  pl.BlockSpec((1, MLA_Q_LORA), lambda i: (0, 0)),
                  pl.BlockSpec((MLA_Q_LORA, n), lambda i: (0, 0)),
                  pl.BlockSpec((tm, LANES), lambda i: (i, 0))],
        out_specs=pl.BlockSpec((tm, n), lambda i: (i, 0)),
        out_shape=jax.ShapeDtypeStruct((s, n), BF16),
        compiler_params=_cparams("parallel"),
        name="mla_q_proj",
    )(qd, q_norm, wq, cs)


def _mla_kv_kernel(kv_ref, kn_ref, wk_ref, wv_ref, cs_ref, k_ref, v_ref):
    x = kv_ref[:, :MLA_KV_LORA].astype(F32)
    ckv = x * lax.rsqrt(jnp.mean(x * x, axis=-1, keepdims=True) + RMS_EPS) * kn_ref[...]
    cb = ckv.astype(BF16)
    pe = _rope_pair(kv_ref[:, MLA_KV_LORA:].astype(F32), cs_ref[...])
    lane = lax.broadcasted_iota(jnp.int32, pe.shape, 1)
    pe = jnp.where(lane < MLA_ROPE, pe, 0.0).astype(k_ref.dtype)
    kn = _dot(cb, wk_ref[...]).astype(k_ref.dtype)
    for h in range(MLA_HEADS):
        lo = h * MLA_QK_PAD
        k_ref[:, lo:lo + LANES] = pe
        k_ref[:, lo + LANES:lo + MLA_QK_PAD] = kn[:, h * MLA_NOPE:(h + 1) * MLA_NOPE]
    v_ref[...] = _dot(cb, wv_ref[...]).astype(v_ref.dtype)


def _mla_kv_proj(kvd, kv_norm, wk, wv, cs):
    s, w = kvd.shape
    tm = min(256, s)
    nk = MLA_HEADS * MLA_QK_PAD
    nv = MLA_HEADS * MLA_V
    return pl.pallas_call(
        _mla_kv_kernel,
        grid=(s // tm,),
        in_specs=[pl.BlockSpec((tm, w), lambda i: (i, 0)),
                  pl.BlockSpec((1, MLA_KV_LORA), lambda i: (0, 0)),
                  pl.BlockSpec(wk.shape, lambda i: (0, 0)),
                  pl.BlockSpec(wv.shape, lambda i: (0, 0)),
                  pl.BlockSpec((tm, LANES), lambda i: (i, 0))],
        out_specs=[pl.BlockSpec((tm, nk), lambda i: (i, 0)),
                   pl.BlockSpec((tm, nv), lambda i: (i, 0))],
        out_shape=[jax.ShapeDtypeStruct((s, nk), BF16),
                   jax.ShapeDtypeStruct((s, nv), BF16)],
        compiler_params=_cparams("parallel"),
        name="mla_kv_proj",
    )(kvd, kv_norm, wk, wv, cs)


FLASH_HEADS = 2


def _lane_repeat(x, n):
    return jnp.concatenate([x] * n, axis=1)


def _flash_kernel(qi_ref, kj_ref, q_ref, k_ref, v_ref, o_ref, m_sc, acc_sc):
    t = pl.program_id(1)
    i = qi_ref[t]
    j = kj_ref[t]
    hps = m_sc.shape[0]
    tk = k_ref.shape[0]
    heads = range(hps)

    @pl.when(j == 0)
    def _():
        m_sc[...] = jnp.full(m_sc.shape, -1e30, F32)
        acc_sc[...] = jnp.zeros(acc_sc.shape, F32)

    def step(masked):
        ss = [_dot_nt(q_ref[:, h * MLA_QK_PAD:(h + 1) * MLA_QK_PAD], k_ref[:, h * MLA_QK_PAD:(h + 1) * MLA_QK_PAD])
              for h in heads]
        if masked:
            r = lax.broadcasted_iota(jnp.int32, ss[0].shape, 0) // CHUNK
            c = lax.broadcasted_iota(jnp.int32, ss[0].shape, 1) // CHUNK
            keep = c <= r
            ss = [jnp.where(keep, s, -1e30) for s in ss]
        m_prev = [m_sc[h] for h in heads]
        m_new = [jnp.maximum(mp, jnp.max(s, axis=-1, keepdims=True)) for mp, s in zip(m_prev, ss)]
        alpha = [jnp.exp(mp - mn) for mp, mn in zip(m_prev, m_new)]
        ps = [jnp.exp(s - _lane_repeat(mn, tk // LANES)).astype(BF16) for s, mn in zip(ss, m_new)]
        ones = jnp.ones((tk, MLA_V), BF16)
        pv = [_dot(p, jnp.concatenate([v_ref[:, h * MLA_V:(h + 1) * MLA_V], ones], axis=1))
              for h, p in zip(heads, ps)]
        for h in heads:
            acc_sc[h] = _lane_repeat(alpha[h], 2) * acc_sc[h] + pv[h]
            m_sc[h] = m_new[h]

    @pl.when(j < i)
    def _():
        step(False)

    @pl.when(j == i)
    def _():
        step(True)
        for h in heads:
            acc = acc_sc[h]
            o_ref[:, h * MLA_V:(h + 1) * MLA_V] = (acc[:, :MLA_V] / acc[:, MLA_V:]).astype(o_ref.dtype)


def _mla_attention(q, k, v):
    s = q.shape[0]
    t = min(512, s)
    n = s // t
    hps = FLASH_HEADS
    qi = np.array([i for i in range(n) for _ in range(i + 1)], np.int32)
    kj = np.array([j for i in range(n) for j in range(i + 1)], np.int32)
    grid_spec = pltpu.PrefetchScalarGridSpec(
        num_scalar_prefetch=2,
        grid=(MLA_HEADS // hps, qi.shape[0]),
        in_specs=[pl.BlockSpec((t, hps * MLA_QK_PAD), lambda h, st, qi_r, kj_r: (qi_r[st], h)),
                  pl.BlockSpec((t, hps * MLA_QK_PAD), lambda h, st, qi_r, kj_r: (kj_r[st], h)),
                  pl.BlockSpec((t, hps * MLA_V), lambda h, st, qi_r, kj_r: (kj_r[st], h))],
        out_specs=pl.BlockSpec((t, hps * MLA_V), lambda h, st, qi_r, kj_r: (qi_r[st], h)),
        scratch_shapes=[pltpu.VMEM((hps, t, LANES), F32), pltpu.VMEM((hps, t, 2 * MLA_V), F32)],
    )
    return pl.pallas_call(
        _flash_kernel,
        grid_spec=grid_spec,
        out_shape=jax.ShapeDtypeStruct((s, MLA_HEADS * MLA_V), BF16),
        compiler_params=_cparams("parallel", "arbitrary"),
        name="mla_flash",
    )(jnp.asarray(qi), jnp.asarray(kj), q, k, v)


def _gdn_conv_kernel(x_ref, prev_ref, w_ref, o_ref, buf, *, tm, tc, halo, n_qk_blocks):
    c = pl.program_id(0)
    i = pl.program_id(1)
    prev = prev_ref[...].astype(F32)
    buf[0:halo, :] = jnp.where(i > 0, prev, 0.0)
    buf[halo:halo + tm, :] = x_ref[...].astype(F32)
    y = jnp.zeros((tm, tc), F32)
    for j in range(GDN_CONV):
        y = y + w_ref[j:j + 1, :] * buf[pl.ds(halo - (GDN_CONV - 1) + j, tm), :]
    y = y * _sigmoid(y)
    is_qk = c < n_qk_blocks
    for g in range(tc // LANES):
        seg = y[:, g * LANES:(g + 1) * LANES]
        nrm = seg * lax.rsqrt(jnp.sum(seg * seg, axis=-1, keepdims=True) + 1e-6)
        o_ref[:, g * LANES:(g + 1) * LANES] = jnp.where(is_qk, nrm, seg).astype(o_ref.dtype)


def _gdn_conv(xqkv, conv_w):
    s, c = xqkv.shape
    tm = min(512, s)
    tc = 512
    halo = 16
    n_qk_blocks = (2 * GDN_QK_HEADS * GDN_DK) // tc
    kern = functools.partial(_gdn_conv_kernel, tm=tm, tc=tc, halo=halo, n_qk_blocks=n_qk_blocks)
    return pl.pallas_call(
        kern,
        grid=(c // tc, s // tm),
        in_specs=[pl.BlockSpec((tm, tc), lambda cc, i: (i, cc)),
                  pl.BlockSpec((halo, tc), lambda cc, i: (jnp.maximum(i * (tm // halo) - 1, 0), cc)),
                  pl.BlockSpec((GDN_CONV, tc), lambda cc, i: (0, cc))],
        out_specs=pl.BlockSpec((tm, tc), lambda cc, i: (i, cc)),
        out_shape=jax.ShapeDtypeStruct((s, c), BF16),
        scratch_shapes=[pltpu.VMEM((tm + halo, tc), F32)],
        compiler_params=_cparams("parallel", "parallel"),
        name="gdn_conv",
    )(xqkv, xqkv, conv_w)


def _softplus(x):
    return jnp.maximum(x, 0.0) + jnp.log1p(jnp.exp(-jnp.abs(x)))


def _gdn_chunk_kernel(q_ref, k_ref, v_ref, z_ref, ac_ref, bc_ref, ar_ref, alc_ref, dtc_ref, alr_ref, dtr_ref,
                      nw_ref, o_ref, state):
    n = pl.program_id(1)
    g_heads = GDN_GROUP
    c = CHUNK

    @pl.when(n == 0)
    def _():
        state[...] = jnp.zeros(state.shape, F32)

    ri = lax.broadcasted_iota(jnp.int32, (c, c), 0)
    ci = lax.broadcasted_iota(jnp.int32, (c, c), 1)
    tril = ci <= ri
    strict = ci < ri
    ltri = tril.astype(F32)
    utri = (ri <= ci).astype(F32)
    eye = (ri == ci).astype(F32)

    g_col = -jnp.exp(alc_ref[...]) * _softplus(ac_ref[...] + dtc_ref[...])
    gc_col = _dot(ltri, g_col, precision=HIGHEST)
    g_row = -jnp.exp(alr_ref[...]) * _softplus(ar_ref[...] + dtr_ref[...])
    gc_row = _dot(g_row, utri, precision=HIGHEST)
    beta_col = _sigmoid(bc_ref[...])
    nw = nw_ref[...]
    scale = float(GDN_DK ** -0.5)

    vheads = range(g_heads)
    qkheads = range(g_heads // 2)
    qs = [q_ref[:, h * GDN_DK:(h + 1) * GDN_DK] for h in qkheads]
    ks = [k_ref[:, h * GDN_DK:(h + 1) * GDN_DK] for h in qkheads]
    kqs = [_dot_nt(jnp.concatenate([ks[h], qs[h]], axis=0), ks[h]) for h in qkheads]
    bcol = [beta_col[:, h:h + 1] for h in vheads]
    gcol = [gc_col[:, h:h + 1] for h in vheads]
    glast = [gc_col[c - 1:c, h:h + 1] for h in vheads]
    decay = [jnp.where(tril, jnp.exp(jnp.where(tril, gcol[h] - gc_row[h:h + 1, :], 0.0)), 0.0) for h in vheads]
    m = [jnp.where(strict, kqs[h // 2][:c] * bcol[h] * decay[h], 0.0) for h in vheads]
    a_qk = [(jnp.where(tril, kqs[h // 2][c:] * decay[h], 0.0) * scale).astype(BF16) for h in vheads]
    t_inv = [eye - m[h] for h in vheads]
    xp = [m[h].astype(BF16) for h in vheads]
    for _ in range(5):
        xp = [_dot(x, x) for x in xp]
        xp = [x.astype(BF16) for x in xp]
        t_inv = [t + _dot(t.astype(BF16), x) for t, x in zip(t_inv, xp)]
    egc = [jnp.exp(gcol[h]) for h in vheads]
    kf = [ks[h].astype(F32) for h in qkheads]
    rhs = [jnp.concatenate([v_ref[:, h * GDN_DV:(h + 1) * GDN_DV].astype(F32) * bcol[h],
                            kf[h // 2] * (bcol[h] * egc[h])], axis=1).astype(BF16) for h in vheads]
    uw = [_dot(t_inv[h].astype(BF16), rhs[h]) for h in vheads]
    st = [state[h] for h in vheads]
    lhs = [jnp.concatenate([uw[h][:, GDN_DV:], qs[h // 2].astype(F32) * egc[h]], axis=0).astype(BF16)
           for h in vheads]
    ws = [_dot(lhs[h], st[h].astype(BF16)) for h in vheads]
    v_new = [(uw[h][:, :GDN_DV] - ws[h][:c]).astype(BF16) for h in vheads]
    kdec = [(kf[h // 2] * jnp.exp(glast[h] - gcol[h])).astype(BF16) for h in vheads]
    o = [ws[h][c:] * scale + _dot(a_qk[h], v_new[h]) for h in vheads]
    for h in vheads:
        state[h] = st[h] * jnp.exp(glast[h]) + _dot_tn(kdec[h], v_new[h])
    for h in vheads:
        on = o[h] * lax.rsqrt(jnp.mean(o[h] * o[h], axis=-1, keepdims=True) + RMS_EPS) * nw
        zz = z_ref[:, h * GDN_DV:(h + 1) * GDN_DV].astype(F32)
        o_ref[:, h * GDN_DV:(h + 1) * GDN_DV] = (on * (zz * _sigmoid(zz))).astype(o_ref.dtype)


def _gdn_delta(qkv, z, a, b, a_log, dt_bias, norm_w):
    s = qkv.shape[0]
    g = GDN_GROUP
    ng = GDN_V_HEADS // g
    nc = s // CHUNK
    gq = g // 2
    wq = gq * GDN_DK
    wv = g * GDN_DV
    k_off = (GDN_QK_HEADS * GDN_DK) // wq
    v_off = (2 * GDN_QK_HEADS * GDN_DK) // wv
    a_col = a.reshape(s, ng, g).transpose(1, 0, 2)
    b_col = b.reshape(s, ng, g).transpose(1, 0, 2)
    a_row = a.reshape(nc, CHUNK, ng, g).transpose(2, 0, 3, 1)
    al_c = a_log.reshape(ng, 1, g)
    dt_c = dt_bias.reshape(ng, 1, g)
    al_r = a_log.reshape(ng, g, 1)
    dt_r = dt_bias.reshape(ng, g, 1)
    return pl.pallas_call(
        _gdn_chunk_kernel,
        grid=(ng, nc),
        in_specs=[pl.BlockSpec((CHUNK, wq), lambda hg, n: (n, hg)),
                  pl.BlockSpec((CHUNK, wq), lambda hg, n: (n, k_off + hg)),
                  pl.BlockSpec((CHUNK, wv), lambda hg, n: (n, v_off + hg)),
                  pl.BlockSpec((CHUNK, wv), lambda hg, n: (n, hg)),
                  pl.BlockSpec((None, CHUNK, g), lambda hg, n: (hg, n, 0)),
                  pl.BlockSpec((None, CHUNK, g), lambda hg, n: (hg, n, 0)),
                  pl.BlockSpec((None, None, g, CHUNK), lambda hg, n: (hg, n, 0, 0)),
                  pl.BlockSpec((None, 1, g), lambda hg, n: (hg, 0, 0)),
                  pl.BlockSpec((None, 1, g), lambda hg, n: (hg, 0, 0)),
                  pl.BlockSpec((None, g, 1), lambda hg, n: (hg, 0, 0)),
                  pl.BlockSpec((None, g, 1), lambda hg, n: (hg, 0, 0)),
                  pl.BlockSpec((1, GDN_DV), lambda hg, n: (0, 0))],
        out_specs=pl.BlockSpec((CHUNK, wv), lambda hg, n: (n, hg)),
        out_shape=jax.ShapeDtypeStruct((s, GDN_V_HEADS * GDN_DV), BF16),
        scratch_shapes=[pltpu.VMEM((g, GDN_DK, GDN_DV), F32)],
        compiler_params=_cparams("parallel", "arbitrary"),
        name="gdn_delta",
    )(qkv, qkv, qkv, z, a_col, b_col, a_row, al_c, dt_c, al_r, dt_r, norm_w)


def _branch_kernel(om_ref, og_ref, wm_ref, wg_ref, gm_ref, gg_ref, bm_ref, bg_ref, o_ref):
    ym = _dot(om_ref[...], wm_ref[...])
    yg = _dot(og_ref[...], wg_ref[...])
    sm = _sigmoid(gm_ref[...].astype(F32) + bm_ref[...])
    sg = _sigmoid(gg_ref[...].astype(F32) + bg_ref[...])
    o_ref[...] = (sm * ym + sg * yg).astype(o_ref.dtype)


def _branch_merge(o_mla, o_gdn, w_o_mla, w_o_gdn, gates, b_gate):
    s = o_mla.shape[0]
    d = w_o_mla.shape[1]
    tm = min(512, s)
    tn = min(512, d)
    nb = d // tn
    return pl.pallas_call(
        _branch_kernel,
        grid=(nb, s // tm),
        in_specs=[pl.BlockSpec((tm, o_mla.shape[1]), lambda j, i: (i, 0)),
                  pl.BlockSpec((tm, o_gdn.shape[1]), lambda j, i: (i, 0)),
                  pl.BlockSpec((w_o_mla.shape[0], tn), lambda j, i: (0, j)),
                  pl.BlockSpec((w_o_gdn.shape[0], tn), lambda j, i: (0, j)),
                  pl.BlockSpec((tm, tn), lambda j, i: (i, j)),
                  pl.BlockSpec((tm, tn), lambda j, i: (i, nb + j)),
                  pl.BlockSpec((1, tn), lambda j, i: (0, j)),
                  pl.BlockSpec((1, tn), lambda j, i: (0, nb + j))],
        out_specs=pl.BlockSpec((tm, tn), lambda j, i: (i, j)),
        out_shape=jax.ShapeDtypeStruct((s, d), BF16),
        compiler_params=_cparams("parallel", "parallel"),
        name="branch_merge",
    )(o_mla, o_gdn, w_o_mla, w_o_gdn, gates, gates, b_gate, b_gate)


def _out_ln_kernel(m_ref, w_ref, x_ref, g_ref, b_ref, o_ref):
    y = DN_ALPHA * x_ref[...] + _dot(m_ref[...], w_ref[...])
    o_ref[...] = _layernorm(y, g_ref[...], b_ref[...])


def _out_ln(mixed, w_out, x, g, b):
    s, d = x.shape
    tm = min(256, s)
    return pl.pallas_call(
        _out_ln_kernel,
        grid=(s // tm,),
        in_specs=[pl.BlockSpec((tm, d), lambda i: (i, 0)),
                  pl.BlockSpec((d, d), lambda i: (0, 0)),
                  pl.BlockSpec((tm, d), lambda i: (i, 0)),
                  pl.BlockSpec((1, d), lambda i: (0, 0)),
                  pl.BlockSpec((1, d), lambda i: (0, 0))],
        out_specs=pl.BlockSpec((tm, d), lambda i: (i, 0)),
        out_shape=jax.ShapeDtypeStruct((s, d), F32),
        compiler_params=_cparams("parallel"),
        name="out_ln1",
    )(mixed, w_out, x, g, b)


def _store_rows(ref3, val, row0=0):
    for s in range(val.shape[1] // LANES):
        ref3[:, row0 + s, :] = val[:, s * LANES:(s + 1) * LANES]


def _load_rows(ref3, row0, k):
    return jnp.concatenate([ref3[:, row0 + s, :] for s in range(k)], axis=1)


def _mem_kernel(x_ref, wq_ref, kv_ref, wo_ref, g_ref, b_ref, wr_ref, br_ref, x2_ref, x2r_ref, lg_ref):
    x1 = x_ref[...]
    hd = MEM_HEAD_DIM
    nh = MEM_HEADS
    q = (_dot(x1.astype(BF16), wq_ref[...]) * float(hd ** -0.5)).astype(BF16)
    outs = []
    for h in range(nh):
        kh = kv_ref[:, h * hd:(h + 1) * hd]
        vh = kv_ref[:, (nh + h) * hd:(nh + h + 1) * hd]
        s = _dot_nt(q[:, h * hd:(h + 1) * hd], kh)
        p = jnp.exp(s - jnp.max(s, axis=-1, keepdims=True))
        o = _dot(p.astype(BF16), vh) / jnp.sum(p, axis=-1, keepdims=True)
        outs.append(o.astype(BF16))
    o = jnp.concatenate(outs, axis=1)
    y = DN_ALPHA * x1 + _dot(o, wo_ref[...])
    x2 = _layernorm(y, g_ref[...], b_ref[...])
    x2_ref[...] = x2
    _store_rows(x2r_ref, x2)
    lg_ref[...] = _dot(x2, wr_ref[...], precision=HIGHEST) + br_ref[...]


def _mem_attn_ln(x1, w_mq, kvm, w_mo, g, b, w_r, b_r):
    s, d = x1.shape
    tm = min(256, s)
    return pl.pallas_call(
        _mem_kernel,
        grid=(s // tm,),
        in_specs=[pl.BlockSpec((tm, d), lambda i: (i, 0)),
                  pl.BlockSpec(w_mq.shape, lambda i: (0, 0)),
                  pl.BlockSpec(kvm.shape, lambda i: (0, 0)),
                  pl.BlockSpec(w_mo.shape, lambda i: (0, 0)),
                  pl.BlockSpec((1, d), lambda i: (0, 0)),
                  pl.BlockSpec((1, d), lambda i: (0, 0)),
                  pl.BlockSpec(w_r.shape, lambda i: (0, 0)),
                  pl.BlockSpec((1, LANES), lambda i: (0, 0))],
        out_specs=[pl.BlockSpec((tm, d), lambda i: (i, 0)),
                   pl.BlockSpec((tm, d // LANES, LANES), lambda i: (i, 0, 0)),
                   pl.BlockSpec((tm, LANES), lambda i: (i, 0))],
        out_shape=[jax.ShapeDtypeStruct((s, d), F32),
                   jax.ShapeDtypeStruct((s, d // LANES, LANES), F32),
                   jax.ShapeDtypeStruct((s, LANES), F32)],
        compiler_params=_cparams("parallel"),
        name="mem_attn_ln2",
    )(x1, w_mq, kvm, w_mo, g, b, w_r, b_r)


SEL_E1, SEL_E2, SEL_R1, SEL_R2, SEL_G1, SEL_G2 = range(6)
GRP_LANE0 = N_EXPERTS


def _route_kernel(lg_ref, sel_ref, cnt_ref, carry):
    i = pl.program_id(0)

    @pl.when(i == 0)
    def _():
        carry[...] = jnp.zeros(carry.shape, F32)

    lg = lg_ref[...]
    tm = lg.shape[0]
    lane = lax.broadcasted_iota(jnp.int32, lg.shape, 1)
    big = jnp.int32(4 * LANES)
    neg = jnp.float32(-jnp.inf)

    def first_max(vals):
        mx = jnp.max(vals, axis=-1, keepdims=True)
        idx = jnp.min(jnp.where(vals == mx, lane, big), axis=-1, keepdims=True)
        return mx, idx

    is_grp = (lane >= GRP_LANE0) & (lane < GRP_LANE0 + N_GROUPS)
    gl = jnp.where(is_grp, lg, neg)
    gmax, gidx = first_max(gl)
    p_top = 1.0 / jnp.sum(jnp.where(is_grp, jnp.exp(gl - gmax), 0.0), axis=-1, keepdims=True)
    lo = (gidx - GRP_LANE0) * EXPERTS_PER_GROUP
    in_grp = (lane >= lo) & (lane < lo + EXPERTS_PER_GROUP)
    el = jnp.where(in_grp, lg, neg)
    m1, i1 = first_max(el)
    m2, i2 = first_max(jnp.where(lane == i1, neg, el))
    r = jnp.exp(m2 - m1)
    g1 = p_top / (1.0 + r)
    g2 = p_top * r / (1.0 + r)

    hot1 = lane == i1
    hot2 = lane == i2
    onehot = jnp.where(hot1, 1.0, 0.0) + jnp.where(hot2, 1.0, 0.0)
    ri = lax.broadcasted_iota(jnp.int32, (tm, tm), 0)
    ci = lax.broadcasted_iota(jnp.int32, (tm, tm), 1)
    before = jnp.where(ci < ri, 1.0, 0.0).astype(BF16)
    rank = _dot(before, onehot.astype(BF16)) + carry[...]
    r1 = jnp.sum(jnp.where(hot1, rank, 0.0), axis=-1, keepdims=True)
    r2 = jnp.sum(jnp.where(hot2, rank, 0.0), axis=-1, keepdims=True)
    carry[...] = carry[...] + jnp.sum(onehot, axis=0, keepdims=True)
    cnt_ref[...] = carry[...]

    out = jnp.zeros(lg.shape, F32)
    for ln, val in ((SEL_E1, i1.astype(F32)), (SEL_E2, i2.astype(F32)), (SEL_R1, r1), (SEL_R2, r2),
                    (SEL_G1, g1), (SEL_G2, g2)):
        out = jnp.where(lane == ln, val, out)
    sel_ref[...] = out


def _route(logits):
    t = logits.shape[0]
    tm = min(512, t)
    return pl.pallas_call(
        _route_kernel,
        grid=(t // tm,),
        in_specs=[pl.BlockSpec((tm, LANES), lambda i: (i, 0))],
        out_specs=[pl.BlockSpec((tm, LANES), lambda i: (i, 0)),
                   pl.BlockSpec((1, LANES), lambda i: (0, 0))],
        out_shape=[jax.ShapeDtypeStruct((t, LANES), F32),
                   jax.ShapeDtypeStruct((1, LANES), F32)],
        scratch_shapes=[pltpu.VMEM((1, LANES), F32)],
        compiler_params=_cparams("arbitrary"),
        name="moe_route",
    )(logits)


def _dispatch_kernel(cnt_ref, e1_ref, e2_ref, r1_ref, r2_ref, rowa_ref, blk_ref, rows_ref, pstart, *, t, n_blocks):
    def seg(e, start):
        pstart[e] = start
        c = cnt_ref[e]
        nb = (c + MOE_ROWS - 1) // MOE_ROWS
        end = start + nb * MOE_ROWS

        def mark(p, carry):
            rowa_ref[p] = -1
            return carry
        lax.fori_loop(start + c, end, mark, 0)

        def blk(bi, carry):
            blk_ref[bi] = e
            rows_ref[bi] = jnp.minimum(start + c - bi * MOE_ROWS, MOE_ROWS)
            return carry
        lax.fori_loop(start // MOE_ROWS, end // MOE_ROWS, blk, 0)
        return end

    total = lax.fori_loop(0, N_EXPERTS, seg, 0)
    used = total // MOE_ROWS
    last = blk_ref[jnp.maximum(used - 1, 0)]

    def tail(bi, carry):
        blk_ref[bi] = last
        rows_ref[bi] = 0
        return carry
    lax.fori_loop(used, n_blocks, tail, 0)

    def unused(p, carry):
        rowa_ref[p] = -1
        return carry
    lax.fori_loop(total, n_blocks * MOE_ROWS, unused, 0)

    def place(tok, carry):
        rowa_ref[pstart[e1_ref[tok]] + r1_ref[tok]] = 2 * tok
        rowa_ref[pstart[e2_ref[tok]] + r2_ref[tok]] = 2 * tok + 1
        return carry
    lax.fori_loop(0, t, place, 0, unroll=8)


def _dispatch(cnt, e1, e2, r1, r2, n_blocks):
    t = e1.shape[0]
    smem = pl.BlockSpec(memory_space=pltpu.SMEM)
    return pl.pallas_call(
        functools.partial(_dispatch_kernel, t=t, n_blocks=n_blocks),
        in_specs=[smem] * 5,
        out_specs=[smem] * 3,
        out_shape=[jax.ShapeDtypeStruct((n_blocks * MOE_ROWS,), jnp.int32),
                   jax.ShapeDtypeStruct((n_blocks,), jnp.int32),
                   jax.ShapeDtypeStruct((n_blocks,), jnp.int32)],
        scratch_shapes=[pltpu.SMEM((N_EXPERTS,), jnp.int32)],
        name="moe_dispatch",
    )(cnt, e1, e2, r1, r2)


def _expert_kernel(blk_ref, rowa_ref, rows_ref, x_hbm, wg_ref, wu_ref, wd_ref, out_hbm, xbuf, ybuf, gsem, ssem):
    b = pl.program_id(0)
    n_blocks = pl.num_programs(0)
    slot = b % 2
    k = xbuf.shape[2]

    def gather_copy(blk_slot, r, tok):
        return pltpu.make_async_copy(x_hbm.at[tok], xbuf.at[blk_slot, r], gsem.at[blk_slot])

    def scatter_copy(blk_slot, r, dst):
        return pltpu.make_async_copy(ybuf.at[blk_slot, r], out_hbm.at[dst], ssem.at[blk_slot])

    def issue_gathers(blk, blk_slot):
        def body(r, carry):
            a = rowa_ref[blk * MOE_ROWS + r]
            gather_copy(blk_slot, r, jnp.maximum(a, 0) // 2).start()
            return carry
        lax.fori_loop(0, MOE_ROWS, body, 0, unroll=8)

    def wait_scatters(blk, blk_slot):
        n = rows_ref[blk]

        @pl.when(n > 0)
        def _():
            pltpu.make_async_copy(ybuf.at[blk_slot, pl.ds(0, n)], out_hbm.at[pl.ds(0, n)], ssem.at[blk_slot]).wait()

    @pl.when((b == 0) & (rows_ref[0] > 0))
    def _():
        issue_gathers(0, 0)

    nxt = jnp.minimum(b + 1, n_blocks - 1)

    @pl.when((b + 1 < n_blocks) & (rows_ref[nxt] > 0))
    def _():
        issue_gathers(nxt, 1 - slot)

    @pl.when(b >= 2)
    def _():
        wait_scatters(jnp.maximum(b - 2, 0), slot)

    n_rows = rows_ref[b]

    @pl.when(n_rows > 0)
    def _():
        pltpu.make_async_copy(x_hbm.at[pl.ds(0, MOE_ROWS)], xbuf.at[slot], gsem.at[slot]).wait()
        xb = _load_rows(xbuf.at[slot], 0, k).astype(BF16)
        hg = _dot(xb, wg_ref[...].astype(BF16))
        hu = _dot(xb, wu_ref[...].astype(BF16))
        hb = (hg * _sigmoid(hg) * hu).astype(BF16)
        _store_rows(ybuf.at[slot], _dot(hb, wd_ref[...].astype(BF16)))

        def issue_scatter(r, carry):
            scatter_copy(slot, r, rowa_ref[b * MOE_ROWS + r]).start()
            return carry
        lax.fori_loop(0, n_rows, issue_scatter, 0)

    @pl.when(b == n_blocks - 1)
    def _():
        @pl.when(b >= 1)
        def _():
            wait_scatters(jnp.maximum(b - 1, 0), 1 - slot)
        wait_scatters(b, slot)


def _experts(x2r, row_a, blk_exp, blk_rows, w_gate_e, w_up_e, w_down_e):
    t, k, _ = x2r.shape
    d = k * LANES
    n_blocks = blk_exp.shape[0]
    de = w_gate_e.shape[2]
    grid_spec = pltpu.PrefetchScalarGridSpec(
        num_scalar_prefetch=3,
        grid=(n_blocks,),
        in_specs=[pl.BlockSpec(memory_space=pl.ANY),
                  pl.BlockSpec((None, d, de), lambda b, blk, rowa, nb: (blk[b], 0, 0)),
                  pl.BlockSpec((None, d, de), lambda b, blk, rowa, nb: (blk[b], 0, 0)),
                  pl.BlockSpec((None, de, d), lambda b, blk, rowa, nb: (blk[b], 0, 0))],
        out_specs=pl.BlockSpec(memory_space=pl.ANY),
        scratch_shapes=[pltpu.VMEM((2, MOE_ROWS, k, LANES), F32), pltpu.VMEM((2, MOE_ROWS, k, LANES), F32),
                        pltpu.SemaphoreType.DMA((2,)), pltpu.SemaphoreType.DMA((2,))],
    )
    return pl.pallas_call(
        _expert_kernel,
        grid_spec=grid_spec,
        out_shape=jax.ShapeDtypeStruct((2 * t, k, LANES), F32),
        compiler_params=_cparams("arbitrary"),
        name="moe_experts",
    )(blk_exp, row_a, blk_rows, x2r, w_gate_e, w_up_e, w_down_e)


def _combine_kernel(x_ref, y_ref, sel_ref, g_ref, b_ref, o_ref):
    k = x_ref.shape[1] // LANES
    sel = sel_ref[...]
    g1 = sel[:, SEL_G1:SEL_G1 + 1]
    g2 = sel[:, SEL_G2:SEL_G2 + 1]
    y = DN_ALPHA * x_ref[...] + (g1 * _load_rows(y_ref, 0, k) + g2 * _load_rows(y_ref, k, k))
    o_ref[...] = _layernorm(y, g_ref[...], b_ref[...])


def _combine_ln(x2, ys2, sel, g, b):
    t, d = x2.shape
    tm = min(256, t)
    return pl.pallas_call(
        _combine_kernel,
        grid=(t // tm,),
        in_specs=[pl.BlockSpec((tm, d), lambda i: (i, 0)),
                  pl.BlockSpec((tm, 2 * d // LANES, LANES), lambda i: (i, 0, 0)),
                  pl.BlockSpec((tm, LANES), lambda i: (i, 0)),
                  pl.BlockSpec((1, d), lambda i: (0, 0)),
                  pl.BlockSpec((1, d), lambda i: (0, 0))],
        out_specs=pl.BlockSpec((tm, d), lambda i: (i, 0)),
        out_shape=jax.ShapeDtypeStruct((t, d), F32),
        compiler_params=_cparams("parallel"),
        name="combine_ln3",
    )(x2, ys2, sel, g, b)


def _mixer(xb, positions, w_in, b_gate, mla_q_norm, w_uq, mla_kv_norm, w_ukv, w_o_mla,
           gdn_conv, gdn_a_log, gdn_dt_bias, gdn_norm, w_o_gdn):
    s, d = xb.shape
    nqkv = 2 * GDN_QK_HEADS * GDN_DK + GDN_V_HEADS * GDN_DV
    nz = GDN_V_HEADS * GDN_DV
    o0 = 0
    o1 = o0 + MLA_Q_LORA
    o2 = o1 + MLA_KV_LORA + MLA_ROPE
    o3 = o2 + nqkv
    o4 = o3 + nz
    o5 = o4 + GDN_V_HEADS
    o6 = o5 + GDN_V_HEADS

    def rot_cols(w):
        half = MLA_ROPE // 2
        return jnp.concatenate([-w[..., half:], w[..., :half]], axis=-1)

    w_q = w_in[:, o0:o1].astype(BF16)
    w_kpe = w_in[:, o1 + MLA_KV_LORA:o2]
    w_kv = jnp.concatenate([w_in[:, o1:o1 + MLA_KV_LORA], w_kpe, rot_cols(w_kpe)], axis=1).astype(BF16)
    w_ba = jnp.concatenate([w_in[:, o4:o6], jnp.zeros((d, LANES - 2 * GDN_V_HEADS), F32)], axis=1).astype(BF16)
    qd = _matmul(xb, w_q, BF16, 512, MLA_Q_LORA)
    kvd = _matmul(xb, w_kv, BF16, 512, w_kv.shape[1])
    hqkv = _matmul(xb, w_in[:, o2:o3].astype(BF16), BF16, 512, 1024)
    hz = _matmul(xb, w_in[:, o3:o4].astype(BF16), BF16, 512, 1024)
    hba = _matmul(xb, w_ba, F32, 512, LANES)
    hgate = _matmul(xb, w_in[:, o6:].astype(BF16), BF16, 512, 1024)

    cs = _rope_table(positions)
    hq = MLA_NOPE + MLA_ROPE
    wq3 = w_uq.reshape(MLA_Q_LORA, MLA_HEADS, hq)
    pe = wq3[..., MLA_NOPE:]
    wq = jnp.concatenate([pe, rot_cols(pe), wq3[..., :MLA_NOPE]], axis=-1)
    wq = wq.reshape(MLA_Q_LORA, MLA_HEADS * MLA_QK_PAD).astype(BF16)
    wkv3 = w_ukv.reshape(MLA_KV_LORA, MLA_HEADS, MLA_NOPE + MLA_V)
    wk = wkv3[..., :MLA_NOPE].reshape(MLA_KV_LORA, MLA_HEADS * MLA_NOPE).astype(BF16)
    wv = wkv3[..., MLA_NOPE:].reshape(MLA_KV_LORA, MLA_HEADS * MLA_V).astype(BF16)
    q = _mla_q_proj(qd, mla_q_norm.reshape(1, -1), wq, cs)
    k, v = _mla_kv_proj(kvd, mla_kv_norm.reshape(1, -1), wk, wv, cs)
    o_mla = _mla_attention(q, k, v)

    qkv_c = _gdn_conv(hqkv, gdn_conv)
    o_gdn = _gdn_delta(qkv_c, hz, hba[:, GDN_V_HEADS:2 * GDN_V_HEADS], hba[:, :GDN_V_HEADS],
                       gdn_a_log, gdn_dt_bias, gdn_norm.reshape(1, -1))

    return _branch_merge(o_mla, o_gdn, w_o_mla.astype(BF16), w_o_gdn.astype(BF16), hgate, b_gate.reshape(1, -1))


def _moe(x2, x2r, logits, w_gate_e, w_up_e, w_down_e, ln_g, ln_b):
    t, d = x2.shape
    n_blocks = (2 * t) // MOE_ROWS + N_EXPERTS
    sel, cnt = _route(logits)
    as_i32 = lambda col: sel[:, col].astype(jnp.int32)
    row_a, blk_exp, blk_rows = _dispatch(cnt[0, :N_EXPERTS].astype(jnp.int32), as_i32(SEL_E1), as_i32(SEL_E2),
                                     as_i32(SEL_R1), as_i32(SEL_R2), n_blocks)
    ys = _experts(x2r, row_a, blk_exp, blk_rows, w_gate_e, w_up_e, w_down_e)
    ys2 = ys.reshape(t, 2 * d // LANES, LANES)
    return _combine_ln(x2, ys2, sel, ln_g, ln_b)


def _layer(x, mem, positions, w_in, b_gate, mla_q_norm, w_uq, mla_kv_norm, w_ukv, w_o_mla,
           gdn_conv, gdn_a_log, gdn_dt_bias, gdn_norm, w_o_gdn, w_out, ln1_g, ln1_b,
           w_mq, w_mkv, w_mo, ln2_g, ln2_b, w_route_grp, b_route_grp, w_route_exp, b_route_exp,
           w_gate_e, w_up_e, w_down_e, ln3_g, ln3_b):
    d = x.shape[1]
    row = lambda p: p.reshape(1, -1)
    mixed = _mixer(x.astype(BF16), positions, w_in, b_gate, mla_q_norm, w_uq, mla_kv_norm, w_ukv, w_o_mla,
                   gdn_conv, gdn_a_log, gdn_dt_bias, gdn_norm, w_o_gdn)
    x1 = _out_ln(mixed, w_out.astype(BF16), x, row(ln1_g), row(ln1_b))

    kvm = _matmul(mem.astype(BF16), w_mkv.astype(BF16), BF16, 256, 512)
    pad = LANES - N_EXPERTS - N_GROUPS
    w_r = jnp.concatenate([w_route_exp, w_route_grp, jnp.zeros((d, pad), F32)], axis=1)
    b_r = jnp.concatenate([b_route_exp, b_route_grp, jnp.zeros((pad,), F32)]).reshape(1, LANES)
    x2, x2r, logits = _mem_attn_ln(x1, w_mq.astype(BF16), kvm, w_mo.astype(BF16), row(ln2_g), row(ln2_b), w_r, b_r)

    return _moe(x2, x2r, logits, w_gate_e, w_up_e, w_down_e, row(ln3_g), row(ln3_b))


def kernel(x, mem, positions, w_in, b_gate, mla_q_norm, w_uq, mla_kv_norm, w_ukv, w_o_mla, gdn_conv, gdn_a_log,
           gdn_dt_bias, gdn_norm, w_o_gdn, w_out, ln1_g, ln1_b, w_mq, w_mkv, w_mo, ln2_g, ln2_b, w_route_grp,
           b_route_grp, w_route_exp, b_route_exp, w_gate_e, w_up_e, w_down_e, ln3_g, ln3_b):
    outs = []
    for bi in range(x.shape[0]):
        h = x[bi]
        for l in range(w_in.shape[0]):
            h = _layer(h, mem[bi], positions[bi], w_in[l], b_gate[l], mla_q_norm[l], w_uq[l], mla_kv_norm[l],
                       w_ukv[l], w_o_mla[l], gdn_conv[l], gdn_a_log[l], gdn_dt_bias[l], gdn_norm[l], w_o_gdn[l],
                       w_out[l], ln1_g[l], ln1_b[l], w_mq[l], w_mkv[l], w_mo[l], ln2_g[l], ln2_b[l],
                       w_route_grp[l], b_route_grp[l], w_route_exp[l], b_route_exp[l], w_gate_e[l], w_up_e[l],
                       w_down_e[l], ln3_g[l], ln3_b[l])
        outs.append(h)
    return jnp.stack(outs, axis=0)
```

```python
import functools

import numpy as np
import jax
import jax.numpy as jnp
from jax import lax
from jax.experimental import pallas as pl
from jax.experimental.pallas import tpu as pltpu

F32 = jnp.float32
BF16 = jnp.bfloat16
HIGHEST = lax.Precision.HIGHEST

LANES = 128
VMEM_LIMIT = 56 * 1024 * 1024

CHUNK = 64
MLA_HEADS = 16
MLA_Q_LORA = 768
MLA_KV_LORA = 512
MLA_NOPE = 128
MLA_ROPE = 64
MLA_V = 128
MLA_QK_PAD = 256
ROPE_THETA = 10000.0
GDN_QK_HEADS = 16
GDN_V_HEADS = 32
GDN_DK = 128
GDN_DV = 128
GDN_CONV = 4
GDN_GROUP = 32
MEM_HEADS = 4
MEM_HEAD_DIM = 128
N_GROUPS = 8
EXPERTS_PER_GROUP = 8
N_EXPERTS = 64
D_EXPERT = 512
MOE_ROWS = 256
RMS_EPS = 1e-6
LN_EPS = 1e-5
DN_ALPHA = 2.0 ** 0.25


def _cparams(*sem):
    return pltpu.CompilerParams(dimension_semantics=sem, vmem_limit_bytes=VMEM_LIMIT)


def _dot(a, b, **kw):
    return jnp.dot(a, b, preferred_element_type=F32, **kw)


def _dot_nt(a, b):
    return lax.dot_general(a, b, (((1,), (1,)), ((), ())), preferred_element_type=F32)


def _dot_tn(a, b):
    return lax.dot_general(a, b, (((0,), (0,)), ((), ())), preferred_element_type=F32)


def _sigmoid(x):
    return 1.0 / (1.0 + jnp.exp(-x))


def _layernorm(y, g, b):
    mu = jnp.mean(y, axis=-1, keepdims=True)
    d = y - mu
    var = jnp.mean(d * d, axis=-1, keepdims=True)
    return d * lax.rsqrt(var + LN_EPS) * g + b


def _mm_kernel(x_ref, w_ref, o_ref):
    o_ref[...] = _dot(x_ref[...], w_ref[...]).astype(o_ref.dtype)


def _matmul(x, w, out_dtype, tm, tn):
    m, k = x.shape
    n = w.shape[1]
    tm, tn = min(tm, m), min(tn, n)
    return pl.pallas_call(
        _mm_kernel,
        grid=(n // tn, m // tm),
        in_specs=[pl.BlockSpec((tm, k), lambda j, i: (i, 0)),
                  pl.BlockSpec((k, tn), lambda j, i: (0, j))],
        out_specs=pl.BlockSpec((tm, tn), lambda j, i: (i, j)),
        out_shape=jax.ShapeDtypeStruct((m, n), out_dtype),
        compiler_params=_cparams("parallel", "parallel"),
        name="matmul",
    )(x, w)


def _rope_table_kernel(pos_ref, inv_ref, o_ref):
    ang = pos_ref[...].astype(F32) * inv_ref[...]
    lane = lax.broadcasted_iota(jnp.int32, ang.shape, 1)
    o_ref[...] = jnp.where(lane < MLA_ROPE, jnp.cos(ang), jnp.sin(ang))


def _rope_table(positions):
    s = positions.shape[0]
    inv = 1.0 / (ROPE_THETA ** (np.arange(0, MLA_ROPE, 2, dtype=np.float32) / MLA_ROPE))
    inv4 = jnp.asarray(np.tile(inv.astype(np.float32), 4)[None, :])
    tm = min(512, s)
    return pl.pallas_call(
        _rope_table_kernel,
        grid=(s // tm,),
        in_specs=[pl.BlockSpec((tm, 1), lambda i: (i, 0)),
                  pl.BlockSpec((1, LANES), lambda i: (0, 0))],
        out_specs=pl.BlockSpec((tm, LANES), lambda i: (i, 0)),
        out_shape=jax.ShapeDtypeStruct((s, LANES), F32),
        compiler_params=_cparams("parallel"),
        name="rope_table",
    )(positions.reshape(s, 1), inv4)


def _rope_pair(t, cs):
    a = t * cs
    return a + pltpu.roll(a, MLA_ROPE, axis=1)


def _mla_q_kernel(qd_ref, qn_ref, w_ref, cs_ref, o_ref, *, scale):
    x = qd_ref[...].astype(F32)
    cq = x * lax.rsqrt(jnp.mean(x * x, axis=-1, keepdims=True) + RMS_EPS) * qn_ref[...]
    cqb = cq.astype(BF16)
    cs = cs_ref[...]
    for h in range(MLA_HEADS):
        lo = h * MLA_QK_PAD
        p = _dot(cqb, w_ref[:, lo:lo + MLA_QK_PAD])
        o_ref[:, lo:lo + LANES] = (_rope_pair(p[:, :LANES], cs) * scale).astype(o_ref.dtype)
        o_ref[:, lo + LANES:lo + MLA_QK_PAD] = (p[:, LANES:] * scale).astype(o_ref.dtype)


def _mla_q_proj(qd, q_norm, wq, cs):
    s = qd.shape[0]
    tm = min(256, s)
    n = MLA_HEADS * MLA_QK_PAD
    scale = float((MLA_NOPE + MLA_ROPE) ** -0.5)
    return pl.pallas_call(
        functools.partial(_mla_q_kernel, scale=scale),
        grid=(s // tm,),
        in_specs=[pl.BlockSpec((tm, MLA_Q_LORA), lambda i: (i, 0)),
                  pl.BlockSpec((1, MLA_Q_LORA), lambda i: (0, 0)),
                  pl.BlockSpec((MLA_Q_LORA, n), lambda i: (0, 0)),
                  pl.BlockSpec((tm, LANES), lambda i: (i, 0))],
        out_specs=pl.BlockSpec((tm, n), lambda i: (i, 0)),
        out_shape=jax.ShapeDtypeStruct((s, n), BF16),
        compiler_params=_cparams("parallel"),
        name="mla_q_proj",
    )(qd, q_norm, wq, cs)


def _mla_kv_kernel(kv_ref, kn_ref, wk_ref, wv_ref, cs_ref, k_ref, v_ref):
    x = kv_ref[:, :MLA_KV_LORA].astype(F32)
    ckv = x * lax.rsqrt(jnp.mean(x * x, axis=-1, keepdims=True) + RMS_EPS) * kn_ref[...]
    cb = ckv.astype(BF16)
    pe = _rope_pair(kv_ref[:, MLA_KV_LORA:].astype(F32), cs_ref[...])
    lane = lax.broadcasted_iota(jnp.int32, pe.shape, 1)
    pe = jnp.where(lane < MLA_ROPE, pe, 0.0).astype(k_ref.dtype)
    kn = _dot(cb, wk_ref[...]).astype(k_ref.dtype)
    for h in range(MLA_HEADS):
        lo = h * MLA_QK_PAD
        k_ref[:, lo:lo + LANES] = pe
        k_ref[:, lo + LANES:lo + MLA_QK_PAD] = kn[:, h * MLA_NOPE:(h + 1) * MLA_NOPE]
    v_ref[...] = _dot(cb, wv_ref[...]).astype(v_ref.dtype)


def _mla_kv_proj(kvd, kv_norm, wk, wv, cs):
    s, w = kvd.shape
    tm = min(256, s)
    nk = MLA_HEADS * MLA_QK_PAD
    nv = MLA_HEADS * MLA_V
    return pl.pallas_call(
        _mla_kv_kernel,
        grid=(s // tm,),
        in_specs=[pl.BlockSpec((tm, w), lambda i: (i, 0)),
                  pl.BlockSpec((1, MLA_KV_LORA), lambda i: (0, 0)),
                  pl.BlockSpec(wk.shape, lambda i: (0, 0)),
                  pl.BlockSpec(wv.shape, lambda i: (0, 0)),
                  pl.BlockSpec((tm, LANES), lambda i: (i, 0))],
        out_specs=[pl.BlockSpec((tm, nk), lambda i: (i, 0)),
                   pl.BlockSpec((tm, nv), lambda i: (i, 0))],
        out_shape=[jax.ShapeDtypeStruct((s, nk), BF16),
                   jax.ShapeDtypeStruct((s, nv), BF16)],
        compiler_params=_cparams("parallel"),
        name="mla_kv_proj",
    )(kvd, kv_norm, wk, wv, cs)


FLASH_HEADS = 2
FLASH_TILE = 1024


def _lane_repeat(x, n):
    return jnp.concatenate([x] * n, axis=1)


def _flash_kernel(qi_ref, kj_ref, q_ref, k_ref, v_ref, o_ref, m_sc, acc_sc):
    t = pl.program_id(1)
    i = qi_ref[t]
    j = kj_ref[t]
    hps = m_sc.shape[0]
    tk = k_ref.shape[0]
    heads = range(hps)

    @pl.when(j == 0)
    def _():
        m_sc[...] = jnp.full(m_sc.shape, -1e30, F32)
        acc_sc[...] = jnp.zeros(acc_sc.shape, F32)

    def step(masked):
        ss = [_dot_nt(q_ref[:, h * MLA_QK_PAD:(h + 1) * MLA_QK_PAD], k_ref[:, h * MLA_QK_PAD:(h + 1) * MLA_QK_PAD])
              for h in heads]
        if masked:
            r = lax.broadcasted_iota(jnp.int32, ss[0].shape, 0) // CHUNK
            c = lax.broadcasted_iota(jnp.int32, ss[0].shape, 1) // CHUNK
            keep = c <= r
            ss = [jnp.where(keep, s, -1e30) for s in ss]
        m_prev = [m_sc[h] for h in heads]
        m_new = [jnp.maximum(mp, jnp.max(s, axis=-1, keepdims=True)) for mp, s in zip(m_prev, ss)]
        alpha = [jnp.exp(mp - mn) for mp, mn in zip(m_prev, m_new)]
        ps = [jnp.exp(s - _lane_repeat(mn, tk // LANES)).astype(BF16) for s, mn in zip(ss, m_new)]
        ones = jnp.ones((tk, MLA_V), BF16)
        pv = [_dot(p, jnp.concatenate([v_ref[:, h * MLA_V:(h + 1) * MLA_V], ones], axis=1))
              for h, p in zip(heads, ps)]
        for h in heads:
            acc_sc[h] = _lane_repeat(alpha[h], 2) * acc_sc[h] + pv[h]
            m_sc[h] = m_new[h]

    @pl.when(j < i)
    def _():
        step(False)

    @pl.when(j == i)
    def _():
        step(True)
        for h in heads:
            acc = acc_sc[h]
            o_ref[:, h * MLA_V:(h + 1) * MLA_V] = (acc[:, :MLA_V] / acc[:, MLA_V:]).astype(o_ref.dtype)


def _mla_attention(q, k, v):
    s = q.shape[0]
    t = min(FLASH_TILE, s)
    n = s // t
    hps = FLASH_HEADS
    qi = np.array([i for i in range(n) for _ in range(i + 1)], np.int32)
    kj = np.array([j for i in range(n) for j in range(i + 1)], np.int32)
    grid_spec = pltpu.PrefetchScalarGridSpec(
        num_scalar_prefetch=2,
        grid=(MLA_HEADS // hps, qi.shape[0]),
        in_specs=[pl.BlockSpec((t, hps * MLA_QK_PAD), lambda h, st, qi_r, kj_r: (qi_r[st], h)),
                  pl.BlockSpec((t, hps * MLA_QK_PAD), lambda h, st, qi_r, kj_r: (kj_r[st], h)),
                  pl.BlockSpec((t, hps * MLA_V), lambda h, st, qi_r, kj_r: (kj_r[st], h))],
        out_specs=pl.BlockSpec((t, hps * MLA_V), lambda h, st, qi_r, kj_r: (qi_r[st], h)),
        scratch_shapes=[pltpu.VMEM((hps, t, LANES), F32), pltpu.VMEM((hps, t, 2 * MLA_V), F32)],
    )
    return pl.pallas_call(
        _flash_kernel,
        grid_spec=grid_spec,
        out_shape=jax.ShapeDtypeStruct((s, MLA_HEADS * MLA_V), BF16),
        compiler_params=_cparams("parallel", "arbitrary"),
        name="mla_flash",
    )(jnp.asarray(qi), jnp.asarray(kj), q, k, v)


def _gdn_conv_kernel(x_ref, prev_ref, w_ref, o_ref, buf, *, tm, tc, halo, n_qk_blocks):
    c = pl.program_id(0)
    i = pl.program_id(1)
    prev = prev_ref[...].astype(F32)
    buf[0:halo, :] = jnp.where(i > 0, prev, 0.0)
    buf[halo:halo + tm, :] = x_ref[...].astype(F32)
    y = jnp.zeros((tm, tc), F32)
    for j in range(GDN_CONV):
        y = y + w_ref[j:j + 1, :] * buf[pl.ds(halo - (GDN_CONV - 1) + j, tm), :]
    y = y * _sigmoid(y)
    is_qk = c < n_qk_blocks
    for g in range(tc // LANES):
        seg = y[:, g * LANES:(g + 1) * LANES]
        nrm = seg * lax.rsqrt(jnp.sum(seg * seg, axis=-1, keepdims=True) + 1e-6)
        o_ref[:, g * LANES:(g + 1) * LANES] = jnp.where(is_qk, nrm, seg).astype(o_ref.dtype)


def _gdn_conv(xqkv, conv_w):
    s, c = xqkv.shape
    tm = min(512, s)
    tc = 512
    halo = 16
    n_qk_blocks = (2 * GDN_QK_HEADS * GDN_DK) // tc
    kern = functools.partial(_gdn_conv_kernel, tm=tm, tc=tc, halo=halo, n_qk_blocks=n_qk_blocks)
    return pl.pallas_call(
        kern,
        grid=(c // tc, s // tm),
        in_specs=[pl.BlockSpec((tm, tc), lambda cc, i: (i, cc)),
                  pl.BlockSpec((halo, tc), lambda cc, i: (jnp.maximum(i * (tm // halo) - 1, 0), cc)),
                  pl.BlockSpec((GDN_CONV, tc), lambda cc, i: (0, cc))],
        out_specs=pl.BlockSpec((tm, tc), lambda cc, i: (i, cc)),
        out_shape=jax.ShapeDtypeStruct((s, c), BF16),
        scratch_shapes=[pltpu.VMEM((tm + halo, tc), F32)],
        compiler_params=_cparams("parallel", "parallel"),
        name="gdn_conv",
    )(xqkv, xqkv, conv_w)


def _softplus(x):
    return jnp.maximum(x, 0.0) + jnp.log1p(jnp.exp(-jnp.abs(x)))


def _gdn_chunk_kernel(q_ref, k_ref, v_ref, z_ref, ac_ref, bc_ref, ar_ref, alc_ref, dtc_ref, alr_ref, dtr_ref,
                      nw_ref, o_ref, state):
    n = pl.program_id(1)
    g_heads = GDN_GROUP
    c = CHUNK

    @pl.when(n == 0)
    def _():
        state[...] = jnp.zeros(state.shape, F32)

    ri = lax.broadcasted_iota(jnp.int32, (c, c), 0)
    ci = lax.broadcasted_iota(jnp.int32, (c, c), 1)
    tril = ci <= ri
    strict = ci < ri
    ltri = tril.astype(F32)
    utri = (ri <= ci).astype(F32)
    eye = (ri == ci).astype(F32)

    g_col = -jnp.exp(alc_ref[...]) * _softplus(ac_ref[...] + dtc_ref[...])
    gc_col = _dot(ltri, g_col, precision=HIGHEST)
    g_row = -jnp.exp(alr_ref[...]) * _softplus(ar_ref[...] + dtr_ref[...])
    gc_row = _dot(g_row, utri, precision=HIGHEST)
    beta_col = _sigmoid(bc_ref[...])
    nw = nw_ref[...]
    scale = float(GDN_DK ** -0.5)

    vheads = range(g_heads)
    qkheads = range(g_heads // 2)
    qs = [q_ref[:, h * GDN_DK:(h + 1) * GDN_DK] for h in qkheads]
    ks = [k_ref[:, h * GDN_DK:(h + 1) * GDN_DK] for h in qkheads]
    kqs = [_dot_nt(jnp.concatenate([ks[h], qs[h]], axis=0), ks[h]) for h in qkheads]
    bcol = [beta_col[:, h:h + 1] for h in vheads]
    gcol = [gc_col[:, h:h + 1] for h in vheads]
    glast = [gc_col[c - 1:c, h:h + 1] for h in vheads]
    decay = [jnp.where(tril, jnp.exp(jnp.where(tril, gcol[h] - gc_row[h:h + 1, :], 0.0)), 0.0) for h in vheads]
    m = [jnp.where(strict, kqs[h // 2][:c] * bcol[h] * decay[h], 0.0) for h in vheads]
    a_qk = [(jnp.where(tril, kqs[h // 2][c:] * decay[h], 0.0) * scale).astype(BF16) for h in vheads]
    t_inv = [eye - m[h] for h in vheads]
    xp = [m[h].astype(BF16) for h in vheads]
    for _ in range(5):
        xp = [_dot(x, x) for x in xp]
        xp = [x.astype(BF16) for x in xp]
        t_inv = [t + _dot(t.astype(BF16), x) for t, x in zip(t_inv, xp)]
    egc = [jnp.exp(gcol[h]) for h in vheads]
    kf = [ks[h].astype(F32) for h in qkheads]
    rhs = [jnp.concatenate([v_ref[:, h * GDN_DV:(h + 1) * GDN_DV].astype(F32) * bcol[h],
                            kf[h // 2] * (bcol[h] * egc[h])], axis=1).astype(BF16) for h in vheads]
    uw = [_dot(t_inv[h].astype(BF16), rhs[h]) for h in vheads]
    st = [state[h] for h in vheads]
    lhs = [jnp.concatenate([uw[h][:, GDN_DV:], qs[h // 2].astype(F32) * egc[h]], axis=0).astype(BF16)
           for h in vheads]
    ws = [_dot(lhs[h], st[h].astype(BF16)) for h in vheads]
    v_new = [(uw[h][:, :GDN_DV] - ws[h][:c]).astype(BF16) for h in vheads]
    kdec = [(kf[h // 2] * jnp.exp(glast[h] - gcol[h])).astype(BF16) for h in vheads]
    o = [ws[h][c:] * scale + _dot(a_qk[h], v_new[h]) for h in vheads]
    for h in vheads:
        state[h] = st[h] * jnp.exp(glast[h]) + _dot_tn(kdec[h], v_new[h])
    for h in vheads:
        on = o[h] * lax.rsqrt(jnp.mean(o[h] * o[h], axis=-1, keepdims=True) + RMS_EPS) * nw
        zz = z_ref[:, h * GDN_DV:(h + 1) * GDN_DV].astype(F32)
        o_ref[:, h * GDN_DV:(h + 1) * GDN_DV] = (on * (zz * _sigmoid(zz))).astype(o_ref.dtype)


def _gdn_delta(qkv, z, a, b, a_log, dt_bias, norm_w):
    s = qkv.shape[0]
    g = GDN_GROUP
    ng = GDN_V_HEADS // g
    nc = s // CHUNK
    gq = g // 2
    wq = gq * GDN_DK
    wv = g * GDN_DV
    k_off = (GDN_QK_HEADS * GDN_DK) // wq
    v_off = (2 * GDN_QK_HEADS * GDN_DK) // wv
    a_col = a.reshape(s, ng, g).transpose(1, 0, 2)
    b_col = b.reshape(s, ng, g).transpose(1, 0, 2)
    a_row = a.reshape(nc, CHUNK, ng, g).transpose(2, 0, 3, 1)
    al_c = a_log.reshape(ng, 1, g)
    dt_c = dt_bias.reshape(ng, 1, g)
    al_r = a_log.reshape(ng, g, 1)
    dt_r = dt_bias.reshape(ng, g, 1)
    return pl.pallas_call(
        _gdn_chunk_kernel,
        grid=(ng, nc),
        in_specs=[pl.BlockSpec((CHUNK, wq), lambda hg, n: (n, hg)),
                  pl.BlockSpec((CHUNK, wq), lambda hg, n: (n, k_off + hg)),
                  pl.BlockSpec((CHUNK, wv), lambda hg, n: (n, v_off + hg)),
                  pl.BlockSpec((CHUNK, wv), lambda hg, n: (n, hg)),
                  pl.BlockSpec((None, CHUNK, g), lambda hg, n: (hg, n, 0)),
                  pl.BlockSpec((None, CHUNK, g), lambda hg, n: (hg, n, 0)),
                  pl.BlockSpec((None, None, g, CHUNK), lambda hg, n: (hg, n, 0, 0)),
                  pl.BlockSpec((None, 1, g), lambda hg, n: (hg, 0, 0)),
                  pl.BlockSpec((None, 1, g), lambda hg, n: (hg, 0, 0)),
                  pl.BlockSpec((None, g, 1), lambda hg, n: (hg, 0, 0)),
                  pl.BlockSpec((None, g, 1), lambda hg, n: (hg, 0, 0)),
                  pl.BlockSpec((1, GDN_DV), lambda hg, n: (0, 0))],
        out_specs=pl.BlockSpec((CHUNK, wv), lambda hg, n: (n, hg)),
        out_shape=jax.ShapeDtypeStruct((s, GDN_V_HEADS * GDN_DV), BF16),
        scratch_shapes=[pltpu.VMEM((g, GDN_DK, GDN_DV), F32)],
        compiler_params=_cparams("parallel", "arbitrary"),
        name="gdn_delta",
    )(qkv, qkv, qkv, z, a_col, b_col, a_row, al_c, dt_c, al_r, dt_r, norm_w)


def _branch_kernel(om_ref, og_ref, wm_ref, wg_ref, gm_ref, gg_ref, bm_ref, bg_ref, o_ref):
    ym = _dot(om_ref[...], wm_ref[...])
    yg = _dot(og_ref[...], wg_ref[...])
    sm = _sigmoid(gm_ref[...].astype(F32) + bm_ref[...])
    sg = _sigmoid(gg_ref[...].astype(F32) + bg_ref[...])
    o_ref[...] = (sm * ym + sg * yg).astype(o_ref.dtype)


def _branch_merge(o_mla, o_gdn, w_o_mla, w_o_gdn, gates, b_gate):
    s = o_mla.shape[0]
    d = w_o_mla.shape[1]
    tm = min(512, s)
    tn = min(512, d)
    nb = d // tn
    return pl.pallas_call(
        _branch_kernel,
        grid=(nb, s // tm),
        in_specs=[pl.BlockSpec((tm, o_mla.shape[1]), lambda j, i: (i, 0)),
                  pl.BlockSpec((tm, o_gdn.shape[1]), lambda j, i: (i, 0)),
                  pl.BlockSpec((w_o_mla.shape[0], tn), lambda j, i: (0, j)),
                  pl.BlockSpec((w_o_gdn.shape[0], tn), lambda j, i: (0, j)),
                  pl.BlockSpec((tm, tn), lambda j, i: (i, j)),
                  pl.BlockSpec((tm, tn), lambda j, i: (i, nb + j)),
                  pl.BlockSpec((1, tn), lambda j, i: (0, j)),
                  pl.BlockSpec((1, tn), lambda j, i: (0, nb + j))],
        out_specs=pl.BlockSpec((tm, tn), lambda j, i: (i, j)),
        out_shape=jax.ShapeDtypeStruct((s, d), BF16),
        compiler_params=_cparams("parallel", "parallel"),
        name="branch_merge",
    )(o_mla, o_gdn, w_o_mla, w_o_gdn, gates, gates, b_gate, b_gate)


def _out_ln_kernel(m_ref, w_ref, x_ref, g_ref, b_ref, o_ref):
    y = DN_ALPHA * x_ref[...] + _dot(m_ref[...], w_ref[...])
    o_ref[...] = _layernorm(y, g_ref[...], b_ref[...])


def _out_ln(mixed, w_out, x, g, b):
    s, d = x.shape
    tm = min(256, s)
    return pl.pallas_call(
        _out_ln_kernel,
        grid=(s // tm,),
        in_specs=[pl.BlockSpec((tm, d), lambda i: (i, 0)),
                  pl.BlockSpec((d, d), lambda i: (0, 0)),
                  pl.BlockSpec((tm, d), lambda i: (i, 0)),
                  pl.BlockSpec((1, d), lambda i: (0, 0)),
                  pl.BlockSpec((1, d), lambda i: (0, 0))],
        out_specs=pl.BlockSpec((tm, d), lambda i: (i, 0)),
        out_shape=jax.ShapeDtypeStruct((s, d), F32),
        compiler_params=_cparams("parallel"),
        name="out_ln1",
    )(mixed, w_out, x, g, b)


SLAB = 16


def _store_rows(ref2, val, per_token, row0=0):
    n = val.shape[0]
    for s in range(val.shape[1] // LANES):
        ref2[pl.ds(row0 + s, n, stride=per_token), :] = val[:, s * LANES:(s + 1) * LANES]


def _load_rows(ref2, n, per_token, row0, k):
    return jnp.concatenate([ref2[pl.ds(row0 + s, n, stride=per_token), :] for s in range(k)], axis=1)


def _mem_kernel(x_ref, wq_ref, kv_ref, wo_ref, g_ref, b_ref, wr_ref, br_ref, x2_ref, x2r_ref, lg_ref):
    x1 = x_ref[...]
    hd = MEM_HEAD_DIM
    nh = MEM_HEADS
    q = (_dot(x1.astype(BF16), wq_ref[...]) * float(hd ** -0.5)).astype(BF16)
    outs = []
    for h in range(nh):
        kh = kv_ref[:, h * hd:(h + 1) * hd]
        vh = kv_ref[:, (nh + h) * hd:(nh + h + 1) * hd]
        s = _dot_nt(q[:, h * hd:(h + 1) * hd], kh)
        p = jnp.exp(s - jnp.max(s, axis=-1, keepdims=True))
        o = _dot(p.astype(BF16), vh) / jnp.sum(p, axis=-1, keepdims=True)
        outs.append(o.astype(BF16))
    o = jnp.concatenate(outs, axis=1)
    y = DN_ALPHA * x1 + _dot(o, wo_ref[...])
    x2 = _layernorm(y, g_ref[...], b_ref[...])
    x2_ref[...] = x2
    _store_rows(x2r_ref, x2, SLAB)
    lg_ref[...] = _dot(x2, wr_ref[...], precision=HIGHEST) + br_ref[...]


def _mem_attn_ln(x1, w_mq, kvm, w_mo, g, b, w_r, b_r):
    s, d = x1.shape
    tm = min(256, s)
    return pl.pallas_call(
        _mem_kernel,
        grid=(s // tm,),
        in_specs=[pl.BlockSpec((tm, d), lambda i: (i, 0)),
                  pl.BlockSpec(w_mq.shape, lambda i: (0, 0)),
                  pl.BlockSpec(kvm.shape, lambda i: (0, 0)),
                  pl.BlockSpec(w_mo.shape, lambda i: (0, 0)),
                  pl.BlockSpec((1, d), lambda i: (0, 0)),
                  pl.BlockSpec((1, d), lambda i: (0, 0)),
                  pl.BlockSpec(w_r.shape, lambda i: (0, 0)),
                  pl.BlockSpec((1, LANES), lambda i: (0, 0))],
        out_specs=[pl.BlockSpec((tm, d), lambda i: (i, 0)),
                   pl.BlockSpec((tm * SLAB, LANES), lambda i: (i, 0)),
                   pl.BlockSpec((tm, LANES), lambda i: (i, 0))],
        out_shape=[jax.ShapeDtypeStruct((s, d), F32),
                   jax.ShapeDtypeStruct((s * SLAB, LANES), F32),
                   jax.ShapeDtypeStruct((s, LANES), F32)],
        compiler_params=_cparams("parallel"),
        name="mem_attn_ln2",
    )(x1, w_mq, kvm, w_mo, g, b, w_r, b_r)


SEL_E1, SEL_E2, SEL_R1, SEL_R2, SEL_G1, SEL_G2 = range(6)
GRP_LANE0 = N_EXPERTS


def _route_kernel(lg_ref, sel_ref, cnt_ref, carry):
    i = pl.program_id(0)

    @pl.when(i == 0)
    def _():
        carry[...] = jnp.zeros(carry.shape, F32)

    lg = lg_ref[...]
    tm = lg.shape[0]
    lane = lax.broadcasted_iota(jnp.int32, lg.shape, 1)
    big = jnp.int32(4 * LANES)
    neg = jnp.float32(-jnp.inf)

    def first_max(vals):
        mx = jnp.max(vals, axis=-1, keepdims=True)
        idx = jnp.min(jnp.where(vals == mx, lane, big), axis=-1, keepdims=True)
        return mx, idx

    is_grp = (lane >= GRP_LANE0) & (lane < GRP_LANE0 + N_GROUPS)
    gl = jnp.where(is_grp, lg, neg)
    gmax, gidx = first_max(gl)
    p_top = 1.0 / jnp.sum(jnp.where(is_grp, jnp.exp(gl - gmax), 0.0), axis=-1, keepdims=True)
    lo = (gidx - GRP_LANE0) * EXPERTS_PER_GROUP
    in_grp = (lane >= lo) & (lane < lo + EXPERTS_PER_GROUP)
    el = jnp.where(in_grp, lg, neg)
    m1, i1 = first_max(el)
    m2, i2 = first_max(jnp.where(lane == i1, neg, el))
    r = jnp.exp(m2 - m1)
    g1 = p_top / (1.0 + r)
    g2 = p_top * r / (1.0 + r)

    hot1 = lane == i1
    hot2 = lane == i2
    onehot = jnp.where(hot1, 1.0, 0.0) + jnp.where(hot2, 1.0, 0.0)
    ri = lax.broadcasted_iota(jnp.int32, (tm, tm), 0)
    ci = lax.broadcasted_iota(jnp.int32, (tm, tm), 1)
    before = jnp.where(ci < ri, 1.0, 0.0).astype(BF16)
    rank = _dot(before, onehot.astype(BF16)) + carry[...]
    r1 = jnp.sum(jnp.where(hot1, rank, 0.0), axis=-1, keepdims=True)
    r2 = jnp.sum(jnp.where(hot2, rank, 0.0), axis=-1, keepdims=True)
    carry[...] = carry[...] + jnp.sum(onehot, axis=0, keepdims=True)
    cnt_ref[...] = carry[...]

    out = jnp.zeros(lg.shape, F32)
    for ln, val in ((SEL_E1, i1.astype(F32)), (SEL_E2, i2.astype(F32)), (SEL_R1, r1), (SEL_R2, r2),
                    (SEL_G1, g1), (SEL_G2, g2)):
        out = jnp.where(lane == ln, val, out)
    sel_ref[...] = out


def _route(logits):
    t = logits.shape[0]
    tm = min(512, t)
    return pl.pallas_call(
        _route_kernel,
        grid=(t // tm,),
        in_specs=[pl.BlockSpec((tm, LANES), lambda i: (i, 0))],
        out_specs=[pl.BlockSpec((tm, LANES), lambda i: (i, 0)),
                   pl.BlockSpec((1, LANES), lambda i: (0, 0))],
        out_shape=[jax.ShapeDtypeStruct((t, LANES), F32),
                   jax.ShapeDtypeStruct((1, LANES), F32)],
        scratch_shapes=[pltpu.VMEM((1, LANES), F32)],
        compiler_params=_cparams("arbitrary"),
        name="moe_route",
    )(logits)


def _dispatch_kernel(cnt_ref, e1_ref, e2_ref, r1_ref, r2_ref, rowa_ref, blk_ref, rows_ref, pstart, *, t, n_blocks):
    def seg(e, start):
        pstart[e] = start
        c = cnt_ref[e]
        nb = (c + MOE_ROWS - 1) // MOE_ROWS
        end = start + nb * MOE_ROWS

        def mark(p, carry):
            rowa_ref[p] = -1
            return carry
        lax.fori_loop(start + c, end, mark, 0)

        def blk(bi, carry):
            blk_ref[bi] = e
            rows_ref[bi] = jnp.minimum(start + c - bi * MOE_ROWS, MOE_ROWS)
            return carry
        lax.fori_loop(start // MOE_ROWS, end // MOE_ROWS, blk, 0)
        return end

    total = lax.fori_loop(0, N_EXPERTS, seg, 0)
    used = total // MOE_ROWS
    last = blk_ref[jnp.maximum(used - 1, 0)]

    def tail(bi, carry):
        blk_ref[bi] = last
        rows_ref[bi] = 0
        return carry
    lax.fori_loop(used, n_blocks, tail, 0)

    def unused(p, carry):
        rowa_ref[p] = -1
        return carry
    lax.fori_loop(total, n_blocks * MOE_ROWS, unused, 0)

    def place(tok, carry):
        rowa_ref[pstart[e1_ref[tok]] + r1_ref[tok]] = 2 * tok
        rowa_ref[pstart[e2_ref[tok]] + r2_ref[tok]] = 2 * tok + 1
        return carry
    lax.fori_loop(0, t, place, 0, unroll=8)


def _dispatch(cnt, e1, e2, r1, r2, n_blocks):
    t = e1.shape[0]
    smem = pl.BlockSpec(memory_space=pltpu.SMEM)
    return pl.pallas_call(
        functools.partial(_dispatch_kernel, t=t, n_blocks=n_blocks),
        in_specs=[smem] * 5,
        out_specs=[smem] * 3,
        out_shape=[jax.ShapeDtypeStruct((n_blocks * MOE_ROWS,), jnp.int32),
                   jax.ShapeDtypeStruct((n_blocks,), jnp.int32),
                   jax.ShapeDtypeStruct((n_blocks,), jnp.int32)],
        scratch_shapes=[pltpu.SMEM((N_EXPERTS,), jnp.int32)],
        name="moe_dispatch",
    )(cnt, e1, e2, r1, r2)


def _expert_kernel(blk_ref, rowa_ref, rows_ref, x_hbm, wg_ref, wu_ref, wd_ref, out_hbm, xbuf, ybuf, gsem, ssem):
    b = pl.program_id(0)
    n_blocks = pl.num_programs(0)
    slot = b % 2

    def slab(i):
        return pl.ds(pl.multiple_of(i * SLAB, SLAB), SLAB)

    def gather_copy(blk_slot, r, tok):
        return pltpu.make_async_copy(x_hbm.at[slab(tok)], xbuf.at[blk_slot, slab(r)], gsem.at[blk_slot])

    def scatter_copy(blk_slot, r, dst):
        return pltpu.make_async_copy(ybuf.at[blk_slot, slab(r)], out_hbm.at[slab(dst)], ssem.at[blk_slot])

    def issue_gathers(blk, blk_slot):
        def body(r, carry):
            a = rowa_ref[blk * MOE_ROWS + r]
            gather_copy(blk_slot, r, jnp.maximum(a, 0) // 2).start(priority=1)
            return carry
        lax.fori_loop(0, MOE_ROWS, body, 0, unroll=8)

    def wait_scatters(blk, blk_slot):
        n = rows_ref[blk]

        @pl.when(n > 0)
        def _():
            rows = pl.ds(0, pl.multiple_of(n * SLAB, SLAB))
            pltpu.make_async_copy(ybuf.at[blk_slot, rows], out_hbm.at[rows], ssem.at[blk_slot]).wait()

    @pl.when((b == 0) & (rows_ref[0] > 0))
    def _():
        issue_gathers(0, 0)

    nxt = jnp.minimum(b + 1, n_blocks - 1)

    @pl.when((b + 1 < n_blocks) & (rows_ref[nxt] > 0))
    def _():
        issue_gathers(nxt, 1 - slot)

    @pl.when(b >= 2)
    def _():
        wait_scatters(jnp.maximum(b - 2, 0), slot)

    n_rows = rows_ref[b]

    @pl.when(n_rows > 0)
    def _():
        pltpu.make_async_copy(x_hbm.at[pl.ds(0, MOE_ROWS * SLAB)], xbuf.at[slot], gsem.at[slot]).wait()
        xb = _load_rows(xbuf.at[slot], MOE_ROWS, SLAB, 0, SLAB).astype(BF16)
        hg = _dot(xb, wg_ref[...].astype(BF16))
        hu = _dot(xb, wu_ref[...].astype(BF16))
        hb = (hg * _sigmoid(hg) * hu).astype(BF16)
        _store_rows(ybuf.at[slot], _dot(hb, wd_ref[...].astype(BF16)), SLAB)

        def issue_scatter(r, carry):
            scatter_copy(slot, r, rowa_ref[b * MOE_ROWS + r]).start()
            return carry
        lax.fori_loop(0, n_rows, issue_scatter, 0)

    @pl.when(b == n_blocks - 1)
    def _():
        @pl.when(b >= 1)
        def _():
            wait_scatters(jnp.maximum(b - 1, 0), 1 - slot)
        wait_scatters(b, slot)


def _experts(x2r, row_a, blk_exp, blk_rows, w_gate_e, w_up_e, w_down_e):
    t = x2r.shape[0] // SLAB
    d = SLAB * LANES
    n_blocks = blk_exp.shape[0]
    de = w_gate_e.shape[2]
    grid_spec = pltpu.PrefetchScalarGridSpec(
        num_scalar_prefetch=3,
        grid=(n_blocks,),
        in_specs=[pl.BlockSpec(memory_space=pl.ANY),
                  pl.BlockSpec((None, d, de), lambda b, blk, rowa, nb: (blk[b], 0, 0)),
                  pl.BlockSpec((None, d, de), lambda b, blk, rowa, nb: (blk[b], 0, 0)),
                  pl.BlockSpec((None, de, d), lambda b, blk, rowa, nb: (blk[b], 0, 0))],
        out_specs=pl.BlockSpec(memory_space=pl.ANY),
        scratch_shapes=[pltpu.VMEM((2, MOE_ROWS * SLAB, LANES), F32), pltpu.VMEM((2, MOE_ROWS * SLAB, LANES), F32),
                        pltpu.SemaphoreType.DMA((2,)), pltpu.SemaphoreType.DMA((2,))],
    )
    return pl.pallas_call(
        _expert_kernel,
        grid_spec=grid_spec,
        out_shape=jax.ShapeDtypeStruct((2 * t * SLAB, LANES), F32),
        compiler_params=_cparams("arbitrary"),
        name="moe_experts",
    )(blk_exp, row_a, blk_rows, x2r, w_gate_e, w_up_e, w_down_e)


def _combine_kernel(x_ref, y_ref, sel_ref, g_ref, b_ref, o_ref):
    tm = x_ref.shape[0]
    sel = sel_ref[...]
    g1 = sel[:, SEL_G1:SEL_G1 + 1]
    g2 = sel[:, SEL_G2:SEL_G2 + 1]
    y1 = _load_rows(y_ref, tm, 2 * SLAB, 0, SLAB)
    y2 = _load_rows(y_ref, tm, 2 * SLAB, SLAB, SLAB)
    y = DN_ALPHA * x_ref[...] + (g1 * y1 + g2 * y2)
    o_ref[...] = _layernorm(y, g_ref[...], b_ref[...])


def _combine_ln(x2, ys, sel, g, b):
    t, d = x2.shape
    tm = min(256, t)
    return pl.pallas_call(
        _combine_kernel,
        grid=(t // tm,),
        in_specs=[pl.BlockSpec((tm, d), lambda i: (i, 0)),
                  pl.BlockSpec((tm * 2 * SLAB, LANES), lambda i: (i, 0)),
                  pl.BlockSpec((tm, LANES), lambda i: (i, 0)),
                  pl.BlockSpec((1, d), lambda i: (0, 0)),
                  pl.BlockSpec((1, d), lambda i: (0, 0))],
        out_specs=pl.BlockSpec((tm, d), lambda i: (i, 0)),
        out_shape=jax.ShapeDtypeStruct((t, d), F32),
        compiler_params=_cparams("parallel"),
        name="combine_ln3",
    )(x2, ys, sel, g, b)


def _mixer(xb, positions, w_in, b_gate, mla_q_norm, w_uq, mla_kv_norm, w_ukv, w_o_mla,
           gdn_conv, gdn_a_log, gdn_dt_bias, gdn_norm, w_o_gdn):
    s, d = xb.shape
    nqkv = 2 * GDN_QK_HEADS * GDN_DK + GDN_V_HEADS * GDN_DV
    nz = GDN_V_HEADS * GDN_DV
    o0 = 0
    o1 = o0 + MLA_Q_LORA
    o2 = o1 + MLA_KV_LORA + MLA_ROPE
    o3 = o2 + nqkv
    o4 = o3 + nz
    o5 = o4 + GDN_V_HEADS
    o6 = o5 + GDN_V_HEADS

    def rot_cols(w):
        half = MLA_ROPE // 2
        return jnp.concatenate([-w[..., half:], w[..., :half]], axis=-1)

    w_q = w_in[:, o0:o1].astype(BF16)
    w_kpe = w_in[:, o1 + MLA_KV_LORA:o2]
    w_kv = jnp.concatenate([w_in[:, o1:o1 + MLA_KV_LORA], w_kpe, rot_cols(w_kpe)], axis=1).astype(BF16)
    w_ba = jnp.concatenate([w_in[:, o4:o6], jnp.zeros((d, LANES - 2 * GDN_V_HEADS), F32)], axis=1).astype(BF16)
    qd = _matmul(xb, w_q, BF16, 512, MLA_Q_LORA)
    kvd = _matmul(xb, w_kv, BF16, 512, w_kv.shape[1])
    hqkv = _matmul(xb, w_in[:, o2:o3].astype(BF16), BF16, 512, 1024)
    hz = _matmul(xb, w_in[:, o3:o4].astype(BF16), BF16, 512, 1024)
    hba = _matmul(xb, w_ba, F32, 512, LANES)
    hgate = _matmul(xb, w_in[:, o6:].astype(BF16), BF16, 512, 1024)

    cs = _rope_table(positions)
    hq = MLA_NOPE + MLA_ROPE
    wq3 = w_uq.reshape(MLA_Q_LORA, MLA_HEADS, hq)
    pe = wq3[..., MLA_NOPE:]
    wq = jnp.concatenate([pe, rot_cols(pe), wq3[..., :MLA_NOPE]], axis=-1)
    wq = wq.reshape(MLA_Q_LORA, MLA_HEADS * MLA_QK_PAD).astype(BF16)
    wkv3 = w_ukv.reshape(MLA_KV_LORA, MLA_HEADS, MLA_NOPE + MLA_V)
    wk = wkv3[..., :MLA_NOPE].reshape(MLA_KV_LORA, MLA_HEADS * MLA_NOPE).astype(BF16)
    wv = wkv3[..., MLA_NOPE:].reshape(MLA_KV_LORA, MLA_HEADS * MLA_V).astype(BF16)
    q = _mla_q_proj(qd, mla_q_norm.reshape(1, -1), wq, cs)
    k, v = _mla_kv_proj(kvd, mla_kv_norm.reshape(1, -1), wk, wv, cs)
    o_mla = _mla_attention(q, k, v)

    qkv_c = _gdn_conv(hqkv, gdn_conv)
    o_gdn = _gdn_delta(qkv_c, hz, hba[:, GDN_V_HEADS:2 * GDN_V_HEADS], hba[:, :GDN_V_HEADS],
                       gdn_a_log, gdn_dt_bias, gdn_norm.reshape(1, -1))

    return _branch_merge(o_mla, o_gdn, w_o_mla.astype(BF16), w_o_gdn.astype(BF16), hgate, b_gate.reshape(1, -1))


def _moe(x2, x2r, logits, w_gate_e, w_up_e, w_down_e, ln_g, ln_b):
    t, d = x2.shape
    n_blocks = (2 * t) // MOE_ROWS + N_EXPERTS
    sel, cnt = _route(logits)
    as_i32 = lambda col: sel[:, col].astype(jnp.int32)
    row_a, blk_exp, blk_rows = _dispatch(cnt[0, :N_EXPERTS].astype(jnp.int32), as_i32(SEL_E1), as_i32(SEL_E2),
                                     as_i32(SEL_R1), as_i32(SEL_R2), n_blocks)
    ys = _experts(x2r, row_a, blk_exp, blk_rows, w_gate_e, w_up_e, w_down_e)
    return _combine_ln(x2, ys, sel, ln_g, ln_b)


def _layer(x, mem, positions, w_in, b_gate, mla_q_norm, w_uq, mla_kv_norm, w_ukv, w_o_mla,
           gdn_conv, gdn_a_log, gdn_dt_bias, gdn_norm, w_o_gdn, w_out, ln1_g, ln1_b,
           w_mq, w_mkv, w_mo, ln2_g, ln2_b, w_route_grp, b_route_grp, w_route_exp, b_route_exp,
           w_gate_e, w_up_e, w_down_e, ln3_g, ln3_b):
    d = x.shape[1]
    row = lambda p: p.reshape(1, -1)
    mixed = _mixer(x.astype(BF16), positions, w_in, b_gate, mla_q_norm, w_uq, mla_kv_norm, w_ukv, w_o_mla,
                   gdn_conv, gdn_a_log, gdn_dt_bias, gdn_norm, w_o_gdn)
    x1 = _out_ln(mixed, w_out.astype(BF16), x, row(ln1_g), row(ln1_b))

    kvm = _matmul(mem.astype(BF16), w_mkv.astype(BF16), BF16, 256, 512)
    pad = LANES - N_EXPERTS - N_GROUPS
    w_r = jnp.concatenate([w_route_exp, w_route_grp, jnp.zeros((d, pad), F32)], axis=1)
    b_r = jnp.concatenate([b_route_exp, b_route_grp, jnp.zeros((pad,), F32)]).reshape(1, LANES)
    x2, x2r, logits = _mem_attn_ln(x1, w_mq.astype(BF16), kvm, w_mo.astype(BF16), row(ln2_g), row(ln2_b), w_r, b_r)

    return _moe(x2, x2r, logits, w_gate_e, w_up_e, w_down_e, row(ln3_g), row(ln3_b))


def kernel(x, mem, positions, w_in, b_gate, mla_q_norm, w_uq, mla_kv_norm, w_ukv, w_o_mla, gdn_conv, gdn_a_log,
           gdn_dt_bias, gdn_norm, w_o_gdn, w_out, ln1_g, ln1_b, w_mq, w_mkv, w_mo, ln2_g, ln2_b, w_route_grp,
           b_route_grp, w_route_exp, b_route_exp, w_gate_e, w_up_e, w_down_e, ln3_g, ln3_b):
    outs = []
    for bi in range(x.shape[0]):
        h = x[bi]
        for l in range(w_in.shape[0]):
            h = _layer(h, mem[bi], positions[bi], w_in[l], b_gate[l], mla_q_norm[l], w_uq[l], mla_kv_norm[l],
                       w_ukv[l], w_o_mla[l], gdn_conv[l], gdn_a_log[l], gdn_dt_bias[l], gdn_norm[l], w_o_gdn[l],
                       w_out[l], ln1_g[l], ln1_b[l], w_mq[l], w_mkv[l], w_mo[l], ln2_g[l], ln2_b[l],
                       w_route_grp[l], b_route_grp[l], w_route_exp[l], b_route_exp[l], w_gate_e[l], w_up_e[l],
                       w_down_e[l], ln3_g[l], ln3_b[l])
        outs.append(h)
    return jnp.stack(outs, axis=0)
```

```python
import functools

import numpy as np
import jax
import jax.numpy as jnp
from jax import lax
from jax.experimental import pallas as pl
from jax.experimental.pallas import tpu as pltpu

F32 = jnp.float32
BF16 = jnp.bfloat16
HIGHEST = lax.Precision.HIGHEST

LANES = 128
VMEM_LIMIT = 56 * 1024 * 1024

CHUNK = 64
MLA_HEADS = 16
MLA_Q_LORA = 768
MLA_KV_LORA = 512
MLA_NOPE = 128
MLA_ROPE = 64
MLA_V = 128
MLA_QK_PAD = 256
ROPE_THETA = 10000.0
GDN_QK_HEADS = 16
GDN_V_HEADS = 32
GDN_DK = 128
GDN_DV = 128
GDN_CONV = 4
GDN_GROUP = 32
MEM_HEADS = 4
MEM_HEAD_DIM = 128
N_GROUPS = 8
EXPERTS_PER_GROUP = 8
N_EXPERTS = 64
D_EXPERT = 512
MOE_ROWS = 256
RMS_EPS = 1e-6
LN_EPS = 1e-5
DN_ALPHA = 2.0 ** 0.25


def _cparams(*sem):
    return pltpu.CompilerParams(dimension_semantics=sem, vmem_limit_bytes=VMEM_LIMIT)


def _dot(a, b, **kw):
    return jnp.dot(a, b, preferred_element_type=F32, **kw)


def _dot_nt(a, b):
    return lax.dot_general(a, b, (((1,), (1,)), ((), ())), preferred_element_type=F32)


def _dot_tn(a, b):
    return lax.dot_general(a, b, (((0,), (0,)), ((), ())), preferred_element_type=F32)


def _sigmoid(x):
    return 1.0 / (1.0 + jnp.exp(-x))


def _layernorm(y, g, b):
    mu = jnp.mean(y, axis=-1, keepdims=True)
    d = y - mu
    var = jnp.mean(d * d, axis=-1, keepdims=True)
    return d * lax.rsqrt(var + LN_EPS) * g + b


def _mm_kernel(x_ref, w_ref, o_ref):
    o_ref[...] = _dot(x_ref[...], w_ref[...]).astype(o_ref.dtype)


def _matmul(x, w, out_dtype, tm, tn):
    m, k = x.shape
    n = w.shape[1]
    tm, tn = min(tm, m), min(tn, n)
    return pl.pallas_call(
        _mm_kernel,
        grid=(n // tn, m // tm),
        in_specs=[pl.BlockSpec((tm, k), lambda j, i: (i, 0)),
                  pl.BlockSpec((k, tn), lambda j, i: (0, j))],
        out_specs=pl.BlockSpec((tm, tn), lambda j, i: (i, j)),
        out_shape=jax.ShapeDtypeStruct((m, n), out_dtype),
        compiler_params=_cparams("parallel", "parallel"),
        name="matmul",
    )(x, w)


def _rope_table_kernel(pos_ref, inv_ref, o_ref):
    ang = pos_ref[...].astype(F32) * inv_ref[...]
    lane = lax.broadcasted_iota(jnp.int32, ang.shape, 1)
    o_ref[...] = jnp.where(lane < MLA_ROPE, jnp.cos(ang), jnp.sin(ang))


def _rope_table(positions):
    s = positions.shape[0]
    inv = 1.0 / (ROPE_THETA ** (np.arange(0, MLA_ROPE, 2, dtype=np.float32) / MLA_ROPE))
    inv4 = jnp.asarray(np.tile(inv.astype(np.float32), 4)[None, :])
    tm = min(512, s)
    return pl.pallas_call(
        _rope_table_kernel,
        grid=(s // tm,),
        in_specs=[pl.BlockSpec((tm, 1), lambda i: (i, 0)),
                  pl.BlockSpec((1, LANES), lambda i: (0, 0))],
        out_specs=pl.BlockSpec((tm, LANES), lambda i: (i, 0)),
        out_shape=jax.ShapeDtypeStruct((s, LANES), F32),
        compiler_params=_cparams("parallel"),
        name="rope_table",
    )(positions.reshape(s, 1), inv4)


def _rope_pair(t, cs):
    a = t * cs
    return a + pltpu.roll(a, MLA_ROPE, axis=1)


def _mla_q_kernel(qd_ref, qn_ref, w_ref, cs_ref, o_ref, *, scale):
    x = qd_ref[...].astype(F32)
    cq = x * lax.rsqrt(jnp.mean(x * x, axis=-1, keepdims=True) + RMS_EPS) * qn_ref[...]
    cqb = cq.astype(BF16)
    cs = cs_ref[...]
    for h in range(MLA_HEADS):
        lo = h * MLA_QK_PAD
        p = _dot(cqb, w_ref[:, lo:lo + MLA_QK_PAD])
        o_ref[:, lo:lo + LANES] = (_rope_pair(p[:, :LANES], cs) * scale).astype(o_ref.dtype)
        o_ref[:, lo + LANES:lo + MLA_QK_PAD] = (p[:, LANES:] * scale).astype(o_ref.dtype)


def _mla_q_proj(qd, q_norm, wq, cs):
    s = qd.shape[0]
    tm = min(256, s)
    n = MLA_HEADS * MLA_QK_PAD
    scale = float((MLA_NOPE + MLA_ROPE) ** -0.5 * np.log2(np.e))
    return pl.pallas_call(
        functools.partial(_mla_q_kernel, scale=scale),
        grid=(s // tm,),
        in_specs=[pl.BlockSpec((tm, MLA_Q_LORA), lambda i: (i, 0)),
                  pl.BlockSpec((1, MLA_Q_LORA), lambda i: (0, 0)),
                  pl.BlockSpec((MLA_Q_LORA, n), lambda i: (0, 0)),
                  pl.BlockSpec((tm, LANES), lambda i: (i, 0))],
        out_specs=pl.BlockSpec((tm, n), lambda i: (i, 0)),
        out_shape=jax.ShapeDtypeStruct((s, n), BF16),
        compiler_params=_cparams("parallel"),
        name="mla_q_proj",
    )(qd, q_norm, wq, cs)


def _mla_kv_kernel(kv_ref, kn_ref, wk_ref, wv_ref, cs_ref, k_ref, v_ref):
    x = kv_ref[:, :MLA_KV_LORA].astype(F32)
    ckv = x * lax.rsqrt(jnp.mean(x * x, axis=-1, keepdims=True) + RMS_EPS) * kn_ref[...]
    cb = ckv.astype(BF16)
    pe = _rope_pair(kv_ref[:, MLA_KV_LORA:].astype(F32), cs_ref[...])
    lane = lax.broadcasted_iota(jnp.int32, pe.shape, 1)
    pe = jnp.where(lane < MLA_ROPE, pe, 0.0).astype(k_ref.dtype)
    kn = _dot(cb, wk_ref[...]).astype(k_ref.dtype)
    for h in range(MLA_HEADS):
        lo = h * MLA_QK_PAD
        k_ref[:, lo:lo + LANES] = pe
        k_ref[:, lo + LANES:lo + MLA_QK_PAD] = kn[:, h * MLA_NOPE:(h + 1) * MLA_NOPE]
    v_ref[...] = _dot(cb, wv_ref[...]).astype(v_ref.dtype)


def _mla_kv_proj(kvd, kv_norm, wk, wv, cs):
    s, w = kvd.shape
    tm = min(256, s)
    nk = MLA_HEADS * MLA_QK_PAD
    nv = MLA_HEADS * MLA_V
    return pl.pallas_call(
        _mla_kv_kernel,
        grid=(s // tm,),
        in_specs=[pl.BlockSpec((tm, w), lambda i: (i, 0)),
                  pl.BlockSpec((1, MLA_KV_LORA), lambda i: (0, 0)),
                  pl.BlockSpec(wk.shape, lambda i: (0, 0)),
                  pl.BlockSpec(wv.shape, lambda i: (0, 0)),
                  pl.BlockSpec((tm, LANES), lambda i: (i, 0))],
        out_specs=[pl.BlockSpec((tm, nk), lambda i: (i, 0)),
                   pl.BlockSpec((tm, nv), lambda i: (i, 0))],
        out_shape=[jax.ShapeDtypeStruct((s, nk), BF16),
                   jax.ShapeDtypeStruct((s, nv), BF16)],
        compiler_params=_cparams("parallel"),
        name="mla_kv_proj",
    )(kvd, kv_norm, wk, wv, cs)


FLASH_HEADS = 2
FLASH_TILE = 1024


def _lane_repeat(x, n):
    return jnp.concatenate([x] * n, axis=1)


def _flash_kernel(qi_ref, kj_ref, q_ref, k_ref, v_ref, o_ref, m_sc, acc_sc):
    t = pl.program_id(1)
    i = qi_ref[t]
    j = kj_ref[t]
    hps = m_sc.shape[0]
    tk = k_ref.shape[0]
    heads = range(hps)

    @pl.when(j == 0)
    def _():
        m_sc[...] = jnp.full(m_sc.shape, -1e30, F32)
        acc_sc[...] = jnp.zeros(acc_sc.shape, F32)

    def step(masked):
        ss = [_dot_nt(q_ref[:, h * MLA_QK_PAD:(h + 1) * MLA_QK_PAD], k_ref[:, h * MLA_QK_PAD:(h + 1) * MLA_QK_PAD])
              for h in heads]
        if masked:
            r = lax.broadcasted_iota(jnp.int32, ss[0].shape, 0) // CHUNK
            c = lax.broadcasted_iota(jnp.int32, ss[0].shape, 1) // CHUNK
            keep = c <= r
            ss = [jnp.where(keep, s, -1e30) for s in ss]
        m_prev = [m_sc[h] for h in heads]
        m_new = [jnp.maximum(mp, jnp.max(s, axis=-1, keepdims=True)) for mp, s in zip(m_prev, ss)]
        alpha = [jnp.exp2(mp - mn) for mp, mn in zip(m_prev, m_new)]
        ps = [jnp.exp2(s - _lane_repeat(mn, tk // LANES)).astype(BF16) for s, mn in zip(ss, m_new)]
        ones = jnp.ones((tk, MLA_V), BF16)
        pv = [_dot(p, jnp.concatenate([v_ref[:, h * MLA_V:(h + 1) * MLA_V], ones], axis=1))
              for h, p in zip(heads, ps)]
        for h in heads:
            acc_sc[h] = _lane_repeat(alpha[h], 2) * acc_sc[h] + pv[h]
            m_sc[h] = m_new[h]

    @pl.when(j < i)
    def _():
        step(False)

    @pl.when(j == i)
    def _():
        step(True)
        for h in heads:
            acc = acc_sc[h]
            o_ref[:, h * MLA_V:(h + 1) * MLA_V] = (acc[:, :MLA_V] / acc[:, MLA_V:]).astype(o_ref.dtype)


def _mla_attention(q, k, v):
    s = q.shape[0]
    t = min(FLASH_TILE, s)
    n = s // t
    hps = FLASH_HEADS
    qi = np.array([i for i in range(n) for _ in range(i + 1)], np.int32)
    kj = np.array([j for i in range(n) for j in range(i + 1)], np.int32)
    grid_spec = pltpu.PrefetchScalarGridSpec(
        num_scalar_prefetch=2,
        grid=(MLA_HEADS // hps, qi.shape[0]),
        in_specs=[pl.BlockSpec((t, hps * MLA_QK_PAD), lambda h, st, qi_r, kj_r: (qi_r[st], h)),
                  pl.BlockSpec((t, hps * MLA_QK_PAD), lambda h, st, qi_r, kj_r: (kj_r[st], h)),
                  pl.BlockSpec((t, hps * MLA_V), lambda h, st, qi_r, kj_r: (kj_r[st], h))],
        out_specs=pl.BlockSpec((t, hps * MLA_V), lambda h, st, qi_r, kj_r: (qi_r[st], h)),
        scratch_shapes=[pltpu.VMEM((hps, t, LANES), F32), pltpu.VMEM((hps, t, 2 * MLA_V), F32)],
    )
    return pl.pallas_call(
        _flash_kernel,
        grid_spec=grid_spec,
        out_shape=jax.ShapeDtypeStruct((s, MLA_HEADS * MLA_V), BF16),
        compiler_params=_cparams("parallel", "arbitrary"),
        name="mla_flash",
    )(jnp.asarray(qi), jnp.asarray(kj), q, k, v)


CONV_HALO = 8


def _qkv_conv_kernel(x_ref, w_ref, cw_ref, o_ref, buf, *, tm, tn, n_qk_blocks):
    c = pl.program_id(0)
    i = pl.program_id(1)
    halo = CONV_HALO

    @pl.when(i == 0)
    def _():
        buf[0:halo, :] = jnp.zeros((halo, tn), F32)

    h = _dot(x_ref[...], w_ref[...])
    buf[halo:halo + tm, :] = h
    ext = buf[...]
    y = cw_ref[GDN_CONV - 1:GDN_CONV, :] * h
    for k in range(1, GDN_CONV):
        y = y + cw_ref[GDN_CONV - 1 - k:GDN_CONV - k, :] * pltpu.roll(ext, k, axis=0)[halo:halo + tm, :]
    buf[0:halo, :] = h[tm - halo:tm, :]
    y = y * (0.5 * jnp.tanh(0.5 * y) + 0.5)
    is_qk = c < n_qk_blocks
    for g in range(tn // LANES):
        seg = y[:, g * LANES:(g + 1) * LANES]
        inv = lax.rsqrt(jnp.sum(seg * seg, axis=-1, keepdims=True) + 1e-6)
        o_ref[:, g * LANES:(g + 1) * LANES] = (seg * jnp.where(is_qk, inv, 1.0)).astype(o_ref.dtype)


def _qkv_conv_proj(xb, w_qkv, conv_w):
    s, d = xb.shape
    c = w_qkv.shape[1]
    tm = min(512, s)
    tn = 512
    n_qk_blocks = (2 * GDN_QK_HEADS * GDN_DK) // tn
    kern = functools.partial(_qkv_conv_kernel, tm=tm, tn=tn, n_qk_blocks=n_qk_blocks)
    return pl.pallas_call(
        kern,
        grid=(c // tn, s // tm),
        in_specs=[pl.BlockSpec((tm, d), lambda cc, i: (i, 0)),
                  pl.BlockSpec((d, tn), lambda cc, i: (0, cc)),
                  pl.BlockSpec((GDN_CONV, tn), lambda cc, i: (0, cc))],
        out_specs=pl.BlockSpec((tm, tn), lambda cc, i: (i, cc)),
        out_shape=jax.ShapeDtypeStruct((s, c), BF16),
        scratch_shapes=[pltpu.VMEM((tm + CONV_HALO, tn), F32)],
        compiler_params=_cparams("parallel", "arbitrary"),
        name="qkv_conv_proj",
    )(xb, w_qkv, conv_w)


def _softplus(x):
    return jnp.maximum(x, 0.0) + jnp.log1p(jnp.exp(-jnp.abs(x)))


def _gdn_chunk_kernel(q_ref, k_ref, v_ref, z_ref, ac_ref, bc_ref, ar_ref, alc_ref, dtc_ref, alr_ref, dtr_ref,
                      nw_ref, o_ref, state):
    n = pl.program_id(1)
    g_heads = GDN_GROUP
    c = CHUNK

    @pl.when(n == 0)
    def _():
        state[...] = jnp.zeros(state.shape, F32)

    ri = lax.broadcasted_iota(jnp.int32, (c, c), 0)
    ci = lax.broadcasted_iota(jnp.int32, (c, c), 1)
    tril = ci <= ri
    strict = ci < ri
    ltri = tril.astype(F32)
    utri = (ri <= ci).astype(F32)
    eye = (ri == ci).astype(F32)

    g_col = -jnp.exp(alc_ref[...]) * _softplus(ac_ref[...] + dtc_ref[...])
    gc_col = _dot(ltri, g_col, precision=HIGHEST)
    g_row = -jnp.exp(alr_ref[...]) * _softplus(ar_ref[...] + dtr_ref[...])
    gc_row = _dot(g_row, utri, precision=HIGHEST)
    beta_col = _sigmoid(bc_ref[...])
    nw = nw_ref[...]
    scale = float(GDN_DK ** -0.5)

    vheads = range(g_heads)
    qkheads = range(g_heads // 2)
    qs = [q_ref[:, h * GDN_DK:(h + 1) * GDN_DK] for h in qkheads]
    ks = [k_ref[:, h * GDN_DK:(h + 1) * GDN_DK] for h in qkheads]
    kqs = [_dot_nt(jnp.concatenate([ks[h], qs[h]], axis=0), ks[h]) for h in qkheads]
    bcol = [beta_col[:, h:h + 1] for h in vheads]
    gcol = [gc_col[:, h:h + 1] for h in vheads]
    glast = [gc_col[c - 1:c, h:h + 1] for h in vheads]
    decay = [jnp.where(tril, jnp.exp(jnp.where(tril, gcol[h] - gc_row[h:h + 1, :], 0.0)), 0.0) for h in vheads]
    m = [jnp.where(strict, kqs[h // 2][:c] * bcol[h] * decay[h], 0.0) for h in vheads]
    a_qk = [(jnp.where(tril, kqs[h // 2][c:] * decay[h], 0.0) * scale).astype(BF16) for h in vheads]
    t_inv = [eye - m[h] for h in vheads]
    xp = [m[h].astype(BF16) for h in vheads]
    for _ in range(5):
        xp = [_dot(x, x) for x in xp]
        xp = [x.astype(BF16) for x in xp]
        t_inv = [t + _dot(t.astype(BF16), x) for t, x in zip(t_inv, xp)]
    egc = [jnp.exp(gcol[h]) for h in vheads]
    kf = [ks[h].astype(F32) for h in qkheads]
    rhs = [jnp.concatenate([v_ref[:, h * GDN_DV:(h + 1) * GDN_DV].astype(F32) * bcol[h],
                            kf[h // 2] * (bcol[h] * egc[h])], axis=1).astype(BF16) for h in vheads]
    uw = [_dot(t_inv[h].astype(BF16), rhs[h]) for h in vheads]
    st = [state[h] for h in vheads]
    lhs = [jnp.concatenate([uw[h][:, GDN_DV:], qs[h // 2].astype(F32) * egc[h]], axis=0).astype(BF16)
           for h in vheads]
    ws = [_dot(lhs[h], st[h].astype(BF16)) for h in vheads]
    v_new = [(uw[h][:, :GDN_DV] - ws[h][:c]).astype(BF16) for h in vheads]
    kdec = [(kf[h // 2] * jnp.exp(glast[h] - gcol[h])).astype(BF16) for h in vheads]
    o = [ws[h][c:] * scale + _dot(a_qk[h], v_new[h]) for h in vheads]
    for h in vheads:
        state[h] = st[h] * jnp.exp(glast[h]) + _dot_tn(kdec[h], v_new[h])
    for h in vheads:
        on = o[h] * lax.rsqrt(jnp.mean(o[h] * o[h], axis=-1, keepdims=True) + RMS_EPS) * nw
        zz = z_ref[:, h * GDN_DV:(h + 1) * GDN_DV].astype(F32)
        o_ref[:, h * GDN_DV:(h + 1) * GDN_DV] = (on * (zz * _sigmoid(zz))).astype(o_ref.dtype)


def _gdn_delta(qkv, z, a, b, a_log, dt_bias, norm_w):
    s = qkv.shape[0]
    g = GDN_GROUP
    ng = GDN_V_HEADS // g
    nc = s // CHUNK
    gq = g // 2
    wq = gq * GDN_DK
    wv = g * GDN_DV
    k_off = (GDN_QK_HEADS * GDN_DK) // wq
    v_off = (2 * GDN_QK_HEADS * GDN_DK) // wv
    a_col = a.reshape(s, ng, g).transpose(1, 0, 2)
    b_col = b.reshape(s, ng, g).transpose(1, 0, 2)
    a_row = a.reshape(nc, CHUNK, ng, g).transpose(2, 0, 3, 1)
    al_c = a_log.reshape(ng, 1, g)
    dt_c = dt_bias.reshape(ng, 1, g)
    al_r = a_log.reshape(ng, g, 1)
    dt_r = dt_bias.reshape(ng, g, 1)
    return pl.pallas_call(
        _gdn_chunk_kernel,
        grid=(ng, nc),
        in_specs=[pl.BlockSpec((CHUNK, wq), lambda hg, n: (n, hg)),
                  pl.BlockSpec((CHUNK, wq), lambda hg, n: (n, k_off + hg)),
                  pl.BlockSpec((CHUNK, wv), lambda hg, n: (n, v_off + hg)),
                  pl.BlockSpec((CHUNK, wv), lambda hg, n: (n, hg)),
                  pl.BlockSpec((None, CHUNK, g), lambda hg, n: (hg, n, 0)),
                  pl.BlockSpec((None, CHUNK, g), lambda hg, n: (hg, n, 0)),
                  pl.BlockSpec((None, None, g, CHUNK), lambda hg, n: (hg, n, 0, 0)),
                  pl.BlockSpec((None, 1, g), lambda hg, n: (hg, 0, 0)),
                  pl.BlockSpec((None, 1, g), lambda hg, n: (hg, 0, 0)),
                  pl.BlockSpec((None, g, 1), lambda hg, n: (hg, 0, 0)),
                  pl.BlockSpec((None, g, 1), lambda hg, n: (hg, 0, 0)),
                  pl.BlockSpec((1, GDN_DV), lambda hg, n: (0, 0))],
        out_specs=pl.BlockSpec((CHUNK, wv), lambda hg, n: (n, hg)),
        out_shape=jax.ShapeDtypeStruct((s, GDN_V_HEADS * GDN_DV), BF16),
        scratch_shapes=[pltpu.VMEM((g, GDN_DK, GDN_DV), F32)],
        compiler_params=_cparams("parallel", "arbitrary"),
        name="gdn_delta",
    )(qkv, qkv, qkv, z, a_col, b_col, a_row, al_c, dt_c, al_r, dt_r, norm_w)


def _branch_kernel(om_ref, og_ref, wm_ref, wg_ref, gm_ref, gg_ref, bm_ref, bg_ref, o_ref):
    ym = _dot(om_ref[...], wm_ref[...])
    yg = _dot(og_ref[...], wg_ref[...])
    sm = _sigmoid(gm_ref[...].astype(F32) + bm_ref[...])
    sg = _sigmoid(gg_ref[...].astype(F32) + bg_ref[...])
    o_ref[...] = (sm * ym + sg * yg).astype(o_ref.dtype)


def _branch_merge(o_mla, o_gdn, w_o_mla, w_o_gdn, gates, b_gate):
    s = o_mla.shape[0]
    d = w_o_mla.shape[1]
    tm = min(512, s)
    tn = min(512, d)
    nb = d // tn
    return pl.pallas_call(
        _branch_kernel,
        grid=(nb, s // tm),
        in_specs=[pl.BlockSpec((tm, o_mla.shape[1]), lambda j, i: (i, 0)),
                  pl.BlockSpec((tm, o_gdn.shape[1]), lambda j, i: (i, 0)),
                  pl.BlockSpec((w_o_mla.shape[0], tn), lambda j, i: (0, j)),
                  pl.BlockSpec((w_o_gdn.shape[0], tn), lambda j, i: (0, j)),
                  pl.BlockSpec((tm, tn), lambda j, i: (i, j)),
                  pl.BlockSpec((tm, tn), lambda j, i: (i, nb + j)),
                  pl.BlockSpec((1, tn), lambda j, i: (0, j)),
                  pl.BlockSpec((1, tn), lambda j, i: (0, nb + j))],
        out_specs=pl.BlockSpec((tm, tn), lambda j, i: (i, j)),
        out_shape=jax.ShapeDtypeStruct((s, d), BF16),
        compiler_params=_cparams("parallel", "parallel"),
        name="branch_merge",
    )(o_mla, o_gdn, w_o_mla, w_o_gdn, gates, gates, b_gate, b_gate)


def _out_ln_kernel(m_ref, w_ref, x_ref, g_ref, b_ref, o_ref):
    y = DN_ALPHA * x_ref[...] + _dot(m_ref[...], w_ref[...])
    o_ref[...] = _layernorm(y, g_ref[...], b_ref[...])


def _out_ln(mixed, w_out, x, g, b):
    s, d = x.shape
    tm = min(256, s)
    return pl.pallas_call(
        _out_ln_kernel,
        grid=(s // tm,),
        in_specs=[pl.BlockSpec((tm, d), lambda i: (i, 0)),
                  pl.BlockSpec((d, d), lambda i: (0, 0)),
                  pl.BlockSpec((tm, d), lambda i: (i, 0)),
                  pl.BlockSpec((1, d), lambda i: (0, 0)),
                  pl.BlockSpec((1, d), lambda i: (0, 0))],
        out_specs=pl.BlockSpec((tm, d), lambda i: (i, 0)),
        out_shape=jax.ShapeDtypeStruct((s, d), F32),
        compiler_params=_cparams("parallel"),
        name="out_ln1",
    )(mixed, w_out, x, g, b)


SLAB = 16


def _store_rows(ref2, val, per_token, row0=0):
    n = val.shape[0]
    for s in range(val.shape[1] // LANES):
        ref2[pl.ds(row0 + s, n, stride=per_token), :] = val[:, s * LANES:(s + 1) * LANES]


def _load_rows(ref2, n, per_token, row0, k):
    return jnp.concatenate([ref2[pl.ds(row0 + s, n, stride=per_token), :] for s in range(k)], axis=1)


def _mem_kernel(x_ref, wq_ref, kv_ref, wo_ref, g_ref, b_ref, wr_ref, br_ref, x2_ref, x2r_ref, lg_ref):
    x1 = x_ref[...]
    hd = MEM_HEAD_DIM
    nh = MEM_HEADS
    q = (_dot(x1.astype(BF16), wq_ref[...]) * float(hd ** -0.5)).astype(BF16)
    outs = []
    for h in range(nh):
        kh = kv_ref[:, h * hd:(h + 1) * hd]
        vh = kv_ref[:, (nh + h) * hd:(nh + h + 1) * hd]
        s = _dot_nt(q[:, h * hd:(h + 1) * hd], kh)
        p = jnp.exp(s - jnp.max(s, axis=-1, keepdims=True))
        o = _dot(p.astype(BF16), vh) / jnp.sum(p, axis=-1, keepdims=True)
        outs.append(o.astype(BF16))
    o = jnp.concatenate(outs, axis=1)
    y = DN_ALPHA * x1 + _dot(o, wo_ref[...])
    x2 = _layernorm(y, g_ref[...], b_ref[...])
    x2_ref[...] = x2
    _store_rows(x2r_ref, x2, SLAB)
    x_hi = x2.astype(BF16)
    x_lo = (x2 - x_hi.astype(F32)).astype(BF16)
    w = wr_ref[...]
    w_hi = w.astype(BF16)
    w_lo = (w - w_hi.astype(F32)).astype(BF16)
    lg_ref[...] = (_dot(x_hi, w_hi) + _dot(x_lo, w_hi)) + _dot(x_hi, w_lo) + br_ref[...]


def _mem_attn_ln(x1, w_mq, kvm, w_mo, g, b, w_r, b_r):
    s, d = x1.shape
    tm = min(256, s)
    return pl.pallas_call(
        _mem_kernel,
        grid=(s // tm,),
        in_specs=[pl.BlockSpec((tm, d), lambda i: (i, 0)),
                  pl.BlockSpec(w_mq.shape, lambda i: (0, 0)),
                  pl.BlockSpec(kvm.shape, lambda i: (0, 0)),
                  pl.BlockSpec(w_mo.shape, lambda i: (0, 0)),
                  pl.BlockSpec((1, d), lambda i: (0, 0)),
                  pl.BlockSpec((1, d), lambda i: (0, 0)),
                  pl.BlockSpec(w_r.shape, lambda i: (0, 0)),
                  pl.BlockSpec((1, LANES), lambda i: (0, 0))],
        out_specs=[pl.BlockSpec((tm, d), lambda i: (i, 0)),
                   pl.BlockSpec((tm * SLAB, LANES), lambda i: (i, 0)),
                   pl.BlockSpec((tm, LANES), lambda i: (i, 0))],
        out_shape=[jax.ShapeDtypeStruct((s, d), F32),
                   jax.ShapeDtypeStruct((s * SLAB, LANES), F32),
                   jax.ShapeDtypeStruct((s, LANES), F32)],
        compiler_params=_cparams("parallel"),
        name="mem_attn_ln2",
    )(x1, w_mq, kvm, w_mo, g, b, w_r, b_r)


SEL_E1, SEL_E2, SEL_R1, SEL_R2, SEL_G1, SEL_G2 = range(6)
GRP_LANE0 = N_EXPERTS


def _route_kernel(lg_ref, sel_ref, cnt_ref, carry):
    i = pl.program_id(0)

    @pl.when(i == 0)
    def _():
        carry[...] = jnp.zeros(carry.shape, F32)

    lg = lg_ref[...]
    tm = lg.shape[0]
    lane = lax.broadcasted_iota(jnp.int32, lg.shape, 1)
    big = jnp.int32(4 * LANES)
    neg = jnp.float32(-jnp.inf)

    def first_max(vals):
        mx = jnp.max(vals, axis=-1, keepdims=True)
        idx = jnp.min(jnp.where(vals == mx, lane, big), axis=-1, keepdims=True)
        return mx, idx

    is_grp = (lane >= GRP_LANE0) & (lane < GRP_LANE0 + N_GROUPS)
    gl = jnp.where(is_grp, lg, neg)
    gmax, gidx = first_max(gl)
    p_top = 1.0 / jnp.sum(jnp.where(is_grp, jnp.exp(gl - gmax), 0.0), axis=-1, keepdims=True)
    lo = (gidx - GRP_LANE0) * EXPERTS_PER_GROUP
    in_grp = (lane >= lo) & (lane < lo + EXPERTS_PER_GROUP)
    el = jnp.where(in_grp, lg, neg)
    m1, i1 = first_max(el)
    m2, i2 = first_max(jnp.where(lane == i1, neg, el))
    r = jnp.exp(m2 - m1)
    g1 = p_top / (1.0 + r)
    g2 = p_top * r / (1.0 + r)

    hot1 = lane == i1
    hot2 = lane == i2
    onehot = jnp.where(hot1, 1.0, 0.0) + jnp.where(hot2, 1.0, 0.0)
    ri = lax.broadcasted_iota(jnp.int32, (tm, tm), 0)
    ci = lax.broadcasted_iota(jnp.int32, (tm, tm), 1)
    before = jnp.where(ci < ri, 1.0, 0.0).astype(BF16)
    rank = _dot(before, onehot.astype(BF16)) + carry[...]
    r1 = jnp.sum(jnp.where(hot1, rank, 0.0), axis=-1, keepdims=True)
    r2 = jnp.sum(jnp.where(hot2, rank, 0.0), axis=-1, keepdims=True)
    carry[...] = carry[...] + jnp.sum(onehot, axis=0, keepdims=True)
    cnt_ref[...] = carry[...]

    out = jnp.zeros(lg.shape, F32)
    for ln, val in ((SEL_E1, i1.astype(F32)), (SEL_E2, i2.astype(F32)), (SEL_R1, r1), (SEL_R2, r2),
                    (SEL_G1, g1), (SEL_G2, g2)):
        out = jnp.where(lane == ln, val, out)
    sel_ref[...] = out


def _route(logits):
    t = logits.shape[0]
    tm = min(512, t)
    return pl.pallas_call(
        _route_kernel,
        grid=(t // tm,),
        in_specs=[pl.BlockSpec((tm, LANES), lambda i: (i, 0))],
        out_specs=[pl.BlockSpec((tm, LANES), lambda i: (i, 0)),
                   pl.BlockSpec((1, LANES), lambda i: (0, 0))],
        out_shape=[jax.ShapeDtypeStruct((t, LANES), F32),
                   jax.ShapeDtypeStruct((1, LANES), F32)],
        scratch_shapes=[pltpu.VMEM((1, LANES), F32)],
        compiler_params=_cparams("arbitrary"),
        name="moe_route",
    )(logits)


def _dispatch_kernel(cnt_ref, e1_ref, e2_ref, r1_ref, r2_ref, rowa_ref, blk_ref, rows_ref, pstart, *, t, n_blocks):
    def seg(e, start):
        pstart[e] = start
        c = cnt_ref[e]
        nb = (c + MOE_ROWS - 1) // MOE_ROWS
        end = start + nb * MOE_ROWS

        def mark(p, carry):
            rowa_ref[p] = -1
            return carry
        lax.fori_loop(start + c, end, mark, 0)

        def blk(bi, carry):
            blk_ref[bi] = e
            rows_ref[bi] = jnp.minimum(start + c - bi * MOE_ROWS, MOE_ROWS)
            return carry
        lax.fori_loop(start // MOE_ROWS, end // MOE_ROWS, blk, 0)
        return end

    total = lax.fori_loop(0, N_EXPERTS, seg, 0)
    used = total // MOE_ROWS
    last = blk_ref[jnp.maximum(used - 1, 0)]

    def tail(bi, carry):
        blk_ref[bi] = last
        rows_ref[bi] = 0

        def unused(p, c2):
            rowa_ref[bi * MOE_ROWS + p] = -1
            return c2
        lax.fori_loop(0, MOE_ROWS, unused, 0, unroll=16)
        return carry
    lax.fori_loop(used, n_blocks, tail, 0)

    def place(tok, carry):
        rowa_ref[pstart[e1_ref[tok]] + r1_ref[tok]] = 2 * tok
        rowa_ref[pstart[e2_ref[tok]] + r2_ref[tok]] = 2 * tok + 1
        return carry
    lax.fori_loop(0, t, place, 0, unroll=8)


def _dispatch(cnt, e1, e2, r1, r2, n_blocks):
    t = e1.shape[0]
    smem = pl.BlockSpec(memory_space=pltpu.SMEM)
    return pl.pallas_call(
        functools.partial(_dispatch_kernel, t=t, n_blocks=n_blocks),
        in_specs=[smem] * 5,
        out_specs=[smem] * 3,
        out_shape=[jax.ShapeDtypeStruct((n_blocks * MOE_ROWS,), jnp.int32),
                   jax.ShapeDtypeStruct((n_blocks,), jnp.int32),
                   jax.ShapeDtypeStruct((n_blocks,), jnp.int32)],
        scratch_shapes=[pltpu.SMEM((N_EXPERTS,), jnp.int32)],
        name="moe_dispatch",
    )(cnt, e1, e2, r1, r2)


def _expert_kernel(blk_ref, rowa_ref, rows_ref, x_hbm, wg_ref, wu_ref, wd_ref, out_hbm, xbuf, ybuf, gsem, ssem):
    b = pl.program_id(0)
    n_blocks = pl.num_programs(0)
    slot = b % 2

    def slab(i):
        return pl.ds(pl.multiple_of(i * SLAB, SLAB), SLAB)

    def gather_copy(blk_slot, r, tok):
        return pltpu.make_async_copy(x_hbm.at[slab(tok)], xbuf.at[blk_slot, slab(r)], gsem.at[blk_slot])

    def scatter_copy(blk_slot, r, dst):
        return pltpu.make_async_copy(ybuf.at[blk_slot, slab(r)], out_hbm.at[slab(dst)], ssem.at[blk_slot])

    def issue_gathers(blk, blk_slot):
        def body(pair, carry):
            for prio in range(2):
                r = 2 * pair + prio
                a = rowa_ref[blk * MOE_ROWS + r]
                gather_copy(blk_slot, r, jnp.maximum(a, 0) // 2).start(priority=prio)
            return carry
        lax.fori_loop(0, MOE_ROWS // 2, body, 0, unroll=4)

    def wait_gathers(blk_slot):
        pltpu.make_async_copy(x_hbm.at[pl.ds(0, MOE_ROWS * SLAB)], xbuf.at[blk_slot], gsem.at[blk_slot]).wait()

    def wait_scatters(blk, blk_slot):
        n = rows_ref[blk]

        @pl.when(n > 0)
        def _():
            rows = pl.ds(0, pl.multiple_of(n * SLAB, SLAB))
            pltpu.make_async_copy(ybuf.at[blk_slot, rows], out_hbm.at[rows], ssem.at[blk_slot]).wait()

    n_rows = rows_ref[b]
    prev_rows = rows_ref[jnp.maximum(b - 1, 0)]
    nxt = jnp.minimum(b + 1, n_blocks - 1)

    @pl.when(b == 0)
    def _():
        issue_gathers(0, 0)

    @pl.when((b == 0) | (prev_rows > 0))
    def _():
        wait_gathers(slot)

    @pl.when(b >= 2)
    def _():
        wait_scatters(jnp.maximum(b - 2, 0), slot)

    @pl.when(n_rows > 0)
    def _():
        issue_gathers(nxt, 1 - slot)
        xb = _load_rows(xbuf.at[slot], MOE_ROWS, SLAB, 0, SLAB).astype(BF16)
        hg = _dot(xb, wg_ref[...].astype(BF16))
        hu = _dot(xb, wu_ref[...].astype(BF16))
        hb = (hg * _sigmoid(hg) * hu).astype(BF16)
        _store_rows(ybuf.at[slot], _dot(hb, wd_ref[...].astype(BF16)), SLAB)

        def issue_scatter(r, carry):
            scatter_copy(slot, r, rowa_ref[b * MOE_ROWS + r]).start()
            return carry
        lax.fori_loop(0, n_rows, issue_scatter, 0)

    @pl.when(b == n_blocks - 1)
    def _():
        @pl.when(n_rows > 0)
        def _():
            wait_gathers(1 - slot)

        @pl.when(b >= 1)
        def _():
            wait_scatters(jnp.maximum(b - 1, 0), 1 - slot)
        wait_scatters(b, slot)


def _experts(x2r, row_a, blk_exp, blk_rows, w_gate_e, w_up_e, w_down_e):
    t = x2r.shape[0] // SLAB
    d = SLAB * LANES
    n_blocks = blk_exp.shape[0]
    de = w_gate_e.shape[2]
    grid_spec = pltpu.PrefetchScalarGridSpec(
        num_scalar_prefetch=3,
        grid=(n_blocks,),
        in_specs=[pl.BlockSpec(memory_space=pl.ANY),
                  pl.BlockSpec((None, d, de), lambda b, blk, rowa, nb: (blk[b], 0, 0)),
                  pl.BlockSpec((None, d, de), lambda b, blk, rowa, nb: (blk[b], 0, 0)),
                  pl.BlockSpec((None, de, d), lambda b, blk, rowa, nb: (blk[b], 0, 0))],
        out_specs=pl.BlockSpec(memory_space=pl.ANY),
        scratch_shapes=[pltpu.VMEM((2, MOE_ROWS * SLAB, LANES), F32), pltpu.VMEM((2, MOE_ROWS * SLAB, LANES), F32),
                        pltpu.SemaphoreType.DMA((2,)), pltpu.SemaphoreType.DMA((2,))],
    )
    return pl.pallas_call(
        _expert_kernel,
        grid_spec=grid_spec,
        out_shape=jax.ShapeDtypeStruct((2 * t * SLAB, LANES), F32),
        compiler_params=_cparams("arbitrary"),
        name="moe_experts",
    )(blk_exp, row_a, blk_rows, x2r, w_gate_e, w_up_e, w_down_e)


def _combine_kernel(x_ref, y_ref, sel_ref, g_ref, b_ref, o_ref):
    tm = x_ref.shape[0]
    sel = sel_ref[...]
    g1 = sel[:, SEL_G1:SEL_G1 + 1]
    g2 = sel[:, SEL_G2:SEL_G2 + 1]
    y1 = _load_rows(y_ref, tm, 2 * SLAB, 0, SLAB)
    y2 = _load_rows(y_ref, tm, 2 * SLAB, SLAB, SLAB)
    y = DN_ALPHA * x_ref[...] + (g1 * y1 + g2 * y2)
    o_ref[...] = _layernorm(y, g_ref[...], b_ref[...])


def _combine_ln(x2, ys, sel, g, b):
    t, d = x2.shape
    tm = min(256, t)
    return pl.pallas_call(
        _combine_kernel,
        grid=(t // tm,),
        in_specs=[pl.BlockSpec((tm, d), lambda i: (i, 0)),
                  pl.BlockSpec((tm * 2 * SLAB, LANES), lambda i: (i, 0)),
                  pl.BlockSpec((tm, LANES), lambda i: (i, 0)),
                  pl.BlockSpec((1, d), lambda i: (0, 0)),
                  pl.BlockSpec((1, d), lambda i: (0, 0))],
        out_specs=pl.BlockSpec((tm, d), lambda i: (i, 0)),
        out_shape=jax.ShapeDtypeStruct((t, d), F32),
        compiler_params=_cparams("parallel"),
        name="combine_ln3",
    )(x2, ys, sel, g, b)


def _mixer(xb, positions, w_in, b_gate, mla_q_norm, w_uq, mla_kv_norm, w_ukv, w_o_mla,
           gdn_conv, gdn_a_log, gdn_dt_bias, gdn_norm, w_o_gdn):
    s, d = xb.shape
    nqkv = 2 * GDN_QK_HEADS * GDN_DK + GDN_V_HEADS * GDN_DV
    nz = GDN_V_HEADS * GDN_DV
    o0 = 0
    o1 = o0 + MLA_Q_LORA
    o2 = o1 + MLA_KV_LORA + MLA_ROPE
    o3 = o2 + nqkv
    o4 = o3 + nz
    o5 = o4 + GDN_V_HEADS
    o6 = o5 + GDN_V_HEADS

    def rot_cols(w):
        half = MLA_ROPE // 2
        return jnp.concatenate([-w[..., half:], w[..., :half]], axis=-1)

    w_q = w_in[:, o0:o1].astype(BF16)
    w_kpe = w_in[:, o1 + MLA_KV_LORA:o2]
    w_kv = jnp.concatenate([w_in[:, o1:o1 + MLA_KV_LORA], w_kpe, rot_cols(w_kpe)], axis=1).astype(BF16)
    w_ba = jnp.concatenate([w_in[:, o4:o6], jnp.zeros((d, LANES - 2 * GDN_V_HEADS), F32)], axis=1).astype(BF16)
    qd = _matmul(xb, w_q, BF16, 512, MLA_Q_LORA)
    kvd = _matmul(xb, w_kv, BF16, 512, w_kv.shape[1])
    qkv_c = _qkv_conv_proj(xb, w_in[:, o2:o3].astype(BF16), gdn_conv)
    hz = _matmul(xb, w_in[:, o3:o4].astype(BF16), BF16, 512, 1024)
    hba = _matmul(xb, w_ba, F32, 512, LANES)
    hgate = _matmul(xb, w_in[:, o6:].astype(BF16), BF16, 512, 1024)

    cs = _rope_table(positions)
    hq = MLA_NOPE + MLA_ROPE
    wq3 = w_uq.reshape(MLA_Q_LORA, MLA_HEADS, hq)
    pe = wq3[..., MLA_NOPE:]
    wq = jnp.concatenate([pe, rot_cols(pe), wq3[..., :MLA_NOPE]], axis=-1)
    wq = wq.reshape(MLA_Q_LORA, MLA_HEADS * MLA_QK_PAD).astype(BF16)
    wkv3 = w_ukv.reshape(MLA_KV_LORA, MLA_HEADS, MLA_NOPE + MLA_V)
    wk = wkv3[..., :MLA_NOPE].reshape(MLA_KV_LORA, MLA_HEADS * MLA_NOPE).astype(BF16)
    wv = wkv3[..., MLA_NOPE:].reshape(MLA_KV_LORA, MLA_HEADS * MLA_V).astype(BF16)
    q = _mla_q_proj(qd, mla_q_norm.reshape(1, -1), wq, cs)
    k, v = _mla_kv_proj(kvd, mla_kv_norm.reshape(1, -1), wk, wv, cs)
    o_mla = _mla_attention(q, k, v)

    o_gdn = _gdn_delta(qkv_c, hz, hba[:, GDN_V_HEADS:2 * GDN_V_HEADS], hba[:, :GDN_V_HEADS],
                       gdn_a_log, gdn_dt_bias, gdn_norm.reshape(1, -1))

    return _branch_merge(o_mla, o_gdn, w_o_mla.astype(BF16), w_o_gdn.astype(BF16), hgate, b_gate.reshape(1, -1))


def _moe(x2, x2r, logits, w_gate_e, w_up_e, w_down_e, ln_g, ln_b):
    t, d = x2.shape
    n_blocks = (2 * t) // MOE_ROWS + N_EXPERTS
    sel, cnt = _route(logits)
    as_i32 = lambda col: sel[:, col].astype(jnp.int32)
    row_a, blk_exp, blk_rows = _dispatch(cnt[0, :N_EXPERTS].astype(jnp.int32), as_i32(SEL_E1), as_i32(SEL_E2),
                                     as_i32(SEL_R1), as_i32(SEL_R2), n_blocks)
    ys = _experts(x2r, row_a, blk_exp, blk_rows, w_gate_e, w_up_e, w_down_e)
    return _combine_ln(x2, ys, sel, ln_g, ln_b)


def _layer(x, mem, positions, w_in, b_gate, mla_q_norm, w_uq, mla_kv_norm, w_ukv, w_o_mla,
           gdn_conv, gdn_a_log, gdn_dt_bias, gdn_norm, w_o_gdn, w_out, ln1_g, ln1_b,
           w_mq, w_mkv, w_mo, ln2_g, ln2_b, w_route_grp, b_route_grp, w_route_exp, b_route_exp,
           w_gate_e, w_up_e, w_down_e, ln3_g, ln3_b):
    d = x.shape[1]
    row = lambda p: p.reshape(1, -1)
    mixed = _mixer(x.astype(BF16), positions, w_in, b_gate, mla_q_norm, w_uq, mla_kv_norm, w_ukv, w_o_mla,
                   gdn_conv, gdn_a_log, gdn_dt_bias, gdn_norm, w_o_gdn)
    x1 = _out_ln(mixed, w_out.astype(BF16), x, row(ln1_g), row(ln1_b))

    kvm = _matmul(mem.astype(BF16), w_mkv.astype(BF16), BF16, 256, 512)
    pad = LANES - N_EXPERTS - N_GROUPS
    w_r = jnp.concatenate([w_route_exp, w_route_grp, jnp.zeros((d, pad), F32)], axis=1)
    b_r = jnp.concatenate([b_route_exp, b_route_grp, jnp.zeros((pad,), F32)]).reshape(1, LANES)
    x2, x2r, logits = _mem_attn_ln(x1, w_mq.astype(BF16), kvm, w_mo.astype(BF16), row(ln2_g), row(ln2_b), w_r, b_r)

    return _moe(x2, x2r, logits, w_gate_e, w_up_e, w_down_e, row(ln3_g), row(ln3_b))


def kernel(x, mem, positions, w_in, b_gate, mla_q_norm, w_uq, mla_kv_norm, w_ukv, w_o_mla, gdn_conv, gdn_a_log,
           gdn_dt_bias, gdn_norm, w_o_gdn, w_out, ln1_g, ln1_b, w_mq, w_mkv, w_mo, ln2_g, ln2_b, w_route_grp,
           b_route_grp, w_route_exp, b_route_exp, w_gate_e, w_up_e, w_down_e, ln3_g, ln3_b):
    outs = []
    for bi in range(x.shape[0]):
        h = x[bi]
        for l in range(w_in.shape[0]):
            h = _layer(h, mem[bi], positions[bi], w_in[l], b_gate[l], mla_q_norm[l], w_uq[l], mla_kv_norm[l],
                       w_ukv[l], w_o_mla[l], gdn_conv[l], gdn_a_log[l], gdn_dt_bias[l], gdn_norm[l], w_o_gdn[l],
                       w_out[l], ln1_g[l], ln1_b[l], w_mq[l], w_mkv[l], w_mo[l], ln2_g[l], ln2_b[l],
                       w_route_grp[l], b_route_grp[l], w_route_exp[l], b_route_exp[l], w_gate_e[l], w_up_e[l],
                       w_down_e[l], ln3_g[l], ln3_b[l])
        outs.append(h)
    return jnp.stack(outs, axis=0)
```

```python
import functools

import numpy as np
import jax
import jax.numpy as jnp
from jax import lax
from jax.experimental import pallas as pl
from jax.experimental.pallas import tpu as pltpu

F32 = jnp.float32
BF16 = jnp.bfloat16
HIGHEST = lax.Precision.HIGHEST

LANES = 128
VMEM_LIMIT = 56 * 1024 * 1024

CHUNK = 64
MLA_HEADS = 16
MLA_Q_LORA = 768
MLA_KV_LORA = 512
MLA_NOPE = 128
MLA_ROPE = 64
MLA_V = 128
MLA_QK_PAD = 256
ROPE_THETA = 10000.0
GDN_QK_HEADS = 16
GDN_V_HEADS = 32
GDN_DK = 128
GDN_DV = 128
GDN_CONV = 4
GDN_GROUP = 32
MEM_HEADS = 4
MEM_HEAD_DIM = 128
N_GROUPS = 8
EXPERTS_PER_GROUP = 8
N_EXPERTS = 64
D_EXPERT = 512
MOE_ROWS = 256
RMS_EPS = 1e-6
LN_EPS = 1e-5
DN_ALPHA = 2.0 ** 0.25


def _cparams(*sem):
    return pltpu.CompilerParams(dimension_semantics=sem, vmem_limit_bytes=VMEM_LIMIT)


def _dot(a, b, **kw):
    return jnp.dot(a, b, preferred_element_type=F32, **kw)


def _dot_nt(a, b):
    return lax.dot_general(a, b, (((1,), (1,)), ((), ())), preferred_element_type=F32)


def _dot_tn(a, b):
    return lax.dot_general(a, b, (((0,), (0,)), ((), ())), preferred_element_type=F32)


def _sigmoid(x):
    return 1.0 / (1.0 + jnp.exp(-x))


def _layernorm(y, g, b):
    mu = jnp.mean(y, axis=-1, keepdims=True)
    d = y - mu
    var = jnp.mean(d * d, axis=-1, keepdims=True)
    return d * lax.rsqrt(var + LN_EPS) * g + b


def _mm_kernel(x_ref, w_ref, o_ref):
    o_ref[...] = _dot(x_ref[...], w_ref[...]).astype(o_ref.dtype)


def _matmul(x, w, out_dtype, tm, tn):
    m, k = x.shape
    n = w.shape[1]
    tm, tn = min(tm, m), min(tn, n)
    return pl.pallas_call(
        _mm_kernel,
        grid=(n // tn, m // tm),
        in_specs=[pl.BlockSpec((tm, k), lambda j, i: (i, 0)),
                  pl.BlockSpec((k, tn), lambda j, i: (0, j))],
        out_specs=pl.BlockSpec((tm, tn), lambda j, i: (i, j)),
        out_shape=jax.ShapeDtypeStruct((m, n), out_dtype),
        compiler_params=_cparams("parallel", "parallel"),
        name="matmul",
    )(x, w)


def _rope_table_kernel(pos_ref, inv_ref, o_ref):
    ang = pos_ref[...].astype(F32) * inv_ref[...]
    lane = lax.broadcasted_iota(jnp.int32, ang.shape, 1)
    o_ref[...] = jnp.where(lane < MLA_ROPE, jnp.cos(ang), jnp.sin(ang))


def _rope_table(positions):
    s = positions.shape[0]
    inv = 1.0 / (ROPE_THETA ** (np.arange(0, MLA_ROPE, 2, dtype=np.float32) / MLA_ROPE))
    inv4 = jnp.asarray(np.tile(inv.astype(np.float32), 4)[None, :])
    tm = min(512, s)
    return pl.pallas_call(
        _rope_table_kernel,
        grid=(s // tm,),
        in_specs=[pl.BlockSpec((tm, 1), lambda i: (i, 0)),
                  pl.BlockSpec((1, LANES), lambda i: (0, 0))],
        out_specs=pl.BlockSpec((tm, LANES), lambda i: (i, 0)),
        out_shape=jax.ShapeDtypeStruct((s, LANES), F32),
        compiler_params=_cparams("parallel"),
        name="rope_table",
    )(positions.reshape(s, 1), inv4)


def _rope_pair(t, cs):
    a = t * cs
    return a + pltpu.roll(a, MLA_ROPE, axis=1)


def _mla_q_kernel(qd_ref, qn_ref, w_ref, cs_ref, o_ref, *, scale):
    x = qd_ref[...].astype(F32)
    cq = x * lax.rsqrt(jnp.mean(x * x, axis=-1, keepdims=True) + RMS_EPS) * qn_ref[...]
    cqb = cq.astype(BF16)
    cs = cs_ref[...]
    for h in range(MLA_HEADS):
        lo = h * MLA_QK_PAD
        p = _dot(cqb, w_ref[:, lo:lo + MLA_QK_PAD])
        o_ref[:, lo:lo + LANES] = (_rope_pair(p[:, :LANES], cs) * scale).astype(o_ref.dtype)
        o_ref[:, lo + LANES:lo + MLA_QK_PAD] = (p[:, LANES:] * scale).astype(o_ref.dtype)


def _mla_q_proj(qd, q_norm, wq, cs):
    s = qd.shape[0]
    tm = min(256, s)
    n = MLA_HEADS * MLA_QK_PAD
    scale = float((MLA_NOPE + MLA_ROPE) ** -0.5 * np.log2(np.e))
    return pl.pallas_call(
        functools.partial(_mla_q_kernel, scale=scale),
        grid=(s // tm,),
        in_specs=[pl.BlockSpec((tm, MLA_Q_LORA), lambda i: (i, 0)),
                  pl.BlockSpec((1, MLA_Q_LORA), lambda i: (0, 0)),
                  pl.BlockSpec((MLA_Q_LORA, n), lambda i: (0, 0)),
                  pl.BlockSpec((tm, LANES), lambda i: (i, 0))],
        out_specs=pl.BlockSpec((tm, n), lambda i: (i, 0)),
        out_shape=jax.ShapeDtypeStruct((s, n), BF16),
        compiler_params=_cparams("parallel"),
        name="mla_q_proj",
    )(qd, q_norm, wq, cs)


def _mla_kv_kernel(kv_ref, kn_ref, wk_ref, wv_ref, cs_ref, k_ref, v_ref):
    x = kv_ref[:, :MLA_KV_LORA].astype(F32)
    ckv = x * lax.rsqrt(jnp.mean(x * x, axis=-1, keepdims=True) + RMS_EPS) * kn_ref[...]
    cb = ckv.astype(BF16)
    pe = _rope_pair(kv_ref[:, MLA_KV_LORA:].astype(F32), cs_ref[...])
    lane = lax.broadcasted_iota(jnp.int32, pe.shape, 1)
    pe = jnp.where(lane < MLA_ROPE, pe, 0.0).astype(k_ref.dtype)
    kn = _dot(cb, wk_ref[...]).astype(k_ref.dtype)
    for h in range(MLA_HEADS):
        lo = h * MLA_QK_PAD
        k_ref[:, lo:lo + LANES] = pe
        k_ref[:, lo + LANES:lo + MLA_QK_PAD] = kn[:, h * MLA_NOPE:(h + 1) * MLA_NOPE]
    v_ref[...] = _dot(cb, wv_ref[...]).astype(v_ref.dtype)


def _mla_kv_proj(kvd, kv_norm, wk, wv, cs):
    s, w = kvd.shape
    tm = min(256, s)
    nk = MLA_HEADS * MLA_QK_PAD
    nv = MLA_HEADS * MLA_V
    return pl.pallas_call(
        _mla_kv_kernel,
        grid=(s // tm,),
        in_specs=[pl.BlockSpec((tm, w), lambda i: (i, 0)),
                  pl.BlockSpec((1, MLA_KV_LORA), lambda i: (0, 0)),
                  pl.BlockSpec(wk.shape, lambda i: (0, 0)),
                  pl.BlockSpec(wv.shape, lambda i: (0, 0)),
                  pl.BlockSpec((tm, LANES), lambda i: (i, 0))],
        out_specs=[pl.BlockSpec((tm, nk), lambda i: (i, 0)),
                   pl.BlockSpec((tm, nv), lambda i: (i, 0))],
        out_shape=[jax.ShapeDtypeStruct((s, nk), BF16),
                   jax.ShapeDtypeStruct((s, nv), BF16)],
        compiler_params=_cparams("parallel"),
        name="mla_kv_proj",
    )(kvd, kv_norm, wk, wv, cs)


FLASH_HEADS = 2
FLASH_TILE = 1024


def _lane_repeat(x, n):
    return jnp.concatenate([x] * n, axis=1)


def _flash_kernel(qi_ref, kj_ref, q_ref, k_ref, v_ref, o_ref, m_sc, acc_sc):
    t = pl.program_id(1)
    i = qi_ref[t]
    j = kj_ref[t]
    hps = m_sc.shape[0]
    tk = k_ref.shape[0]
    heads = range(hps)

    @pl.when(j == 0)
    def _():
        m_sc[...] = jnp.full(m_sc.shape, -1e30, F32)
        acc_sc[...] = jnp.zeros(acc_sc.shape, F32)

    def update(r0, nr, nk, masked):
        rows = slice(r0, r0 + nr)
        ss = [_dot_nt(q_ref[rows, h * MLA_QK_PAD:(h + 1) * MLA_QK_PAD], k_ref[0:nk, h * MLA_QK_PAD:(h + 1) * MLA_QK_PAD])
              for h in heads]
        if masked:
            r = (lax.broadcasted_iota(jnp.int32, ss[0].shape, 0) + r0) // CHUNK
            c = lax.broadcasted_iota(jnp.int32, ss[0].shape, 1) // CHUNK
            keep = c <= r
            ss = [jnp.where(keep, s, -1e30) for s in ss]
        m_prev = [m_sc[h, rows] for h in heads]
        m_new = [jnp.maximum(mp, jnp.max(s, axis=-1, keepdims=True)) for mp, s in zip(m_prev, ss)]
        alpha = [jnp.exp2(mp - mn) for mp, mn in zip(m_prev, m_new)]
        ps = [jnp.exp2(s - _lane_repeat(mn, nk // LANES)).astype(BF16) for s, mn in zip(ss, m_new)]
        ones = jnp.ones((nk, MLA_V), BF16)
        pv = [_dot(p, jnp.concatenate([v_ref[0:nk, h * MLA_V:(h + 1) * MLA_V], ones], axis=1))
              for h, p in zip(heads, ps)]
        for h in heads:
            acc_sc[h, rows] = _lane_repeat(alpha[h], 2) * acc_sc[h, rows] + pv[h]
            m_sc[h, rows] = m_new[h]

    @pl.when(j < i)
    def _():
        update(0, tk, tk, False)

    @pl.when(j == i)
    def _():
        half = tk // 2
        if half % CHUNK == 0 and half % LANES == 0:
            update(0, half, half, True)
            update(half, half, tk, True)
        else:
            update(0, tk, tk, True)
        for h in heads:
            acc = acc_sc[h]
            o_ref[:, h * MLA_V:(h + 1) * MLA_V] = (acc[:, :MLA_V] / acc[:, MLA_V:]).astype(o_ref.dtype)


def _mla_attention(q, k, v):
    s = q.shape[0]
    t = min(FLASH_TILE, s)
    n = s // t
    hps = FLASH_HEADS
    qi = np.array([i for i in range(n) for _ in range(i + 1)], np.int32)
    kj = np.array([j for i in range(n) for j in range(i + 1)], np.int32)
    grid_spec = pltpu.PrefetchScalarGridSpec(
        num_scalar_prefetch=2,
        grid=(MLA_HEADS // hps, qi.shape[0]),
        in_specs=[pl.BlockSpec((t, hps * MLA_QK_PAD), lambda h, st, qi_r, kj_r: (qi_r[st], h)),
                  pl.BlockSpec((t, hps * MLA_QK_PAD), lambda h, st, qi_r, kj_r: (kj_r[st], h)),
                  pl.BlockSpec((t, hps * MLA_V), lambda h, st, qi_r, kj_r: (kj_r[st], h))],
        out_specs=pl.BlockSpec((t, hps * MLA_V), lambda h, st, qi_r, kj_r: (qi_r[st], h)),
        scratch_shapes=[pltpu.VMEM((hps, t, LANES), F32), pltpu.VMEM((hps, t, 2 * MLA_V), F32)],
    )
    return pl.pallas_call(
        _flash_kernel,
        grid_spec=grid_spec,
        out_shape=jax.ShapeDtypeStruct((s, MLA_HEADS * MLA_V), BF16),
        compiler_params=_cparams("parallel", "arbitrary"),
        name="mla_flash",
    )(jnp.asarray(qi), jnp.asarray(kj), q, k, v)


CONV_HALO = 8


def _qkv_conv_kernel(x_ref, w_ref, cw_ref, o_ref, buf, *, tm, tn, n_qk_blocks):
    c = pl.program_id(0)
    i = pl.program_id(1)
    halo = CONV_HALO

    @pl.when(i == 0)
    def _():
        buf[0:halo, :] = jnp.zeros((halo, tn), F32)

    h = _dot(x_ref[...], w_ref[...])
    buf[halo:halo + tm, :] = h
    ext = buf[...]
    y = cw_ref[GDN_CONV - 1:GDN_CONV, :] * h
    for k in range(1, GDN_CONV):
        y = y + cw_ref[GDN_CONV - 1 - k:GDN_CONV - k, :] * pltpu.roll(ext, k, axis=0)[halo:halo + tm, :]
    buf[0:halo, :] = h[tm - halo:tm, :]
    y = y * (0.5 * jnp.tanh(0.5 * y) + 0.5)
    is_qk = c < n_qk_blocks
    for g in range(tn // LANES):
        seg = y[:, g * LANES:(g + 1) * LANES]
        inv = lax.rsqrt(jnp.sum(seg * seg, axis=-1, keepdims=True) + 1e-6)
        o_ref[:, g * LANES:(g + 1) * LANES] = (seg * jnp.where(is_qk, inv, 1.0)).astype(o_ref.dtype)


def _qkv_conv_proj(xb, w_qkv, conv_w):
    s, d = xb.shape
    c = w_qkv.shape[1]
    tm = min(512, s)
    tn = 512
    n_qk_blocks = (2 * GDN_QK_HEADS * GDN_DK) // tn
    kern = functools.partial(_qkv_conv_kernel, tm=tm, tn=tn, n_qk_blocks=n_qk_blocks)
    return pl.pallas_call(
        kern,
        grid=(c // tn, s // tm),
        in_specs=[pl.BlockSpec((tm, d), lambda cc, i: (i, 0)),
                  pl.BlockSpec((d, tn), lambda cc, i: (0, cc)),
                  pl.BlockSpec((GDN_CONV, tn), lambda cc, i: (0, cc))],
        out_specs=pl.BlockSpec((tm, tn), lambda cc, i: (i, cc)),
        out_shape=jax.ShapeDtypeStruct((s, c), BF16),
        scratch_shapes=[pltpu.VMEM((tm + CONV_HALO, tn), F32)],
        compiler_params=_cparams("parallel", "arbitrary"),
        name="qkv_conv_proj",
    )(xb, w_qkv, conv_w)


def _softplus(x):
    return jnp.maximum(x, 0.0) + jnp.log1p(jnp.exp(-jnp.abs(x)))


def _gdn_chunk_kernel(q_ref, k_ref, v_ref, z_ref, ac_ref, bc_ref, ar_ref, alc_ref, dtc_ref, alr_ref, dtr_ref,
                      nw_ref, o_ref, state):
    n = pl.program_id(1)
    g_heads = GDN_GROUP
    c = CHUNK

    @pl.when(n == 0)
    def _():
        state[...] = jnp.zeros(state.shape, F32)

    ri = lax.broadcasted_iota(jnp.int32, (c, c), 0)
    ci = lax.broadcasted_iota(jnp.int32, (c, c), 1)
    tril = ci <= ri
    strict = ci < ri
    ltri = tril.astype(F32)
    utri = (ri <= ci).astype(F32)
    eye = (ri == ci).astype(F32)

    g_col = -jnp.exp(alc_ref[...]) * _softplus(ac_ref[...] + dtc_ref[...])
    gc_col = _dot(ltri, g_col, precision=HIGHEST)
    g_row = -jnp.exp(alr_ref[...]) * _softplus(ar_ref[...] + dtr_ref[...])
    gc_row = _dot(g_row, utri, precision=HIGHEST)
    beta_col = _sigmoid(bc_ref[...])
    nw = nw_ref[...]
    scale = float(GDN_DK ** -0.5)

    vheads = range(g_heads)
    qkheads = range(g_heads // 2)
    qs = [q_ref[:, h * GDN_DK:(h + 1) * GDN_DK] for h in qkheads]
    ks = [k_ref[:, h * GDN_DK:(h + 1) * GDN_DK] for h in qkheads]
    kqs = [_dot_nt(jnp.concatenate([ks[h], qs[h]], axis=0), ks[h]) for h in qkheads]
    bcol = [beta_col[:, h:h + 1] for h in vheads]
    gcol = [gc_col[:, h:h + 1] for h in vheads]
    glast = [gc_col[c - 1:c, h:h + 1] for h in vheads]
    decay = [jnp.where(tril, jnp.exp(jnp.where(tril, gcol[h] - gc_row[h:h + 1, :], 0.0)), 0.0) for h in vheads]
    m = [jnp.where(strict, kqs[h // 2][:c] * bcol[h] * decay[h], 0.0) for h in vheads]
    a_qk = [(jnp.where(tril, kqs[h // 2][c:] * decay[h], 0.0) * scale).astype(BF16) for h in vheads]
    t_inv = [eye - m[h] for h in vheads]
    xp = [m[h].astype(BF16) for h in vheads]
    for _ in range(5):
        xp = [_dot(x, x) for x in xp]
        xp = [x.astype(BF16) for x in xp]
        t_inv = [t + _dot(t.astype(BF16), x) for t, x in zip(t_inv, xp)]
    egc = [jnp.exp(gcol[h]) for h in vheads]
    kf = [ks[h].astype(F32) for h in qkheads]
    rhs = [jnp.concatenate([v_ref[:, h * GDN_DV:(h + 1) * GDN_DV].astype(F32) * bcol[h],
                            kf[h // 2] * (bcol[h] * egc[h])], axis=1).astype(BF16) for h in vheads]
    uw = [_dot(t_inv[h].astype(BF16), rhs[h]) for h in vheads]
    st = [state[h] for h in vheads]
    lhs = [jnp.concatenate([uw[h][:, GDN_DV:], qs[h // 2].astype(F32) * egc[h]], axis=0).astype(BF16)
           for h in vheads]
    ws = [_dot(lhs[h], st[h].astype(BF16)) for h in vheads]
    v_new = [(uw[h][:, :GDN_DV] - ws[h][:c]).astype(BF16) for h in vheads]
    kdec = [(kf[h // 2] * jnp.exp(glast[h] - gcol[h])).astype(BF16) for h in vheads]
    o = [ws[h][c:] * scale + _dot(a_qk[h], v_new[h]) for h in vheads]
    for h in vheads:
        state[h] = st[h] * jnp.exp(glast[h]) + _dot_tn(kdec[h], v_new[h])
    for h in vheads:
        on = o[h] * lax.rsqrt(jnp.mean(o[h] * o[h], axis=-1, keepdims=True) + RMS_EPS) * nw
        zz = z_ref[:, h * GDN_DV:(h + 1) * GDN_DV].astype(F32)
        o_ref[:, h * GDN_DV:(h + 1) * GDN_DV] = (on * (zz * _sigmoid(zz))).astype(o_ref.dtype)


def _gdn_delta(qkv, z, a, b, a_log, dt_bias, norm_w):
    s = qkv.shape[0]
    g = GDN_GROUP
    ng = GDN_V_HEADS // g
    nc = s // CHUNK
    gq = g // 2
    wq = gq * GDN_DK
    wv = g * GDN_DV
    k_off = (GDN_QK_HEADS * GDN_DK) // wq
    v_off = (2 * GDN_QK_HEADS * GDN_DK) // wv
    a_col = a.reshape(s, ng, g).transpose(1, 0, 2)
    b_col = b.reshape(s, ng, g).transpose(1, 0, 2)
    a_row = a.reshape(nc, CHUNK, ng, g).transpose(2, 0, 3, 1)
    al_c = a_log.reshape(ng, 1, g)
    dt_c = dt_bias.reshape(ng, 1, g)
    al_r = a_log.reshape(ng, g, 1)
    dt_r = dt_bias.reshape(ng, g, 1)
    return pl.pallas_call(
        _gdn_chunk_kernel,
        grid=(ng, nc),
        in_specs=[pl.BlockSpec((CHUNK, wq), lambda hg, n: (n, hg)),
                  pl.BlockSpec((CHUNK, wq), lambda hg, n: (n, k_off + hg)),
                  pl.BlockSpec((CHUNK, wv), lambda hg, n: (n, v_off + hg)),
                  pl.BlockSpec((CHUNK, wv), lambda hg, n: (n, hg)),
                  pl.BlockSpec((None, CHUNK, g), lambda hg, n: (hg, n, 0)),
                  pl.BlockSpec((None, CHUNK, g), lambda hg, n: (hg, n, 0)),
                  pl.BlockSpec((None, None, g, CHUNK), lambda hg, n: (hg, n, 0, 0)),
                  pl.BlockSpec((None, 1, g), lambda hg, n: (hg, 0, 0)),
                  pl.BlockSpec((None, 1, g), lambda hg, n: (hg, 0, 0)),
                  pl.BlockSpec((None, g, 1), lambda hg, n: (hg, 0, 0)),
                  pl.BlockSpec((None, g, 1), lambda hg, n: (hg, 0, 0)),
                  pl.BlockSpec((1, GDN_DV), lambda hg, n: (0, 0))],
        out_specs=pl.BlockSpec((CHUNK, wv), lambda hg, n: (n, hg)),
        out_shape=jax.ShapeDtypeStruct((s, GDN_V_HEADS * GDN_DV), BF16),
        scratch_shapes=[pltpu.VMEM((g, GDN_DK, GDN_DV), F32)],
        compiler_params=_cparams("parallel", "arbitrary"),
        name="gdn_delta",
    )(qkv, qkv, qkv, z, a_col, b_col, a_row, al_c, dt_c, al_r, dt_r, norm_w)


def _branch_kernel(om_ref, og_ref, wm_ref, wg_ref, gm_ref, gg_ref, bm_ref, bg_ref, o_ref):
    ym = _dot(om_ref[...], wm_ref[...])
    yg = _dot(og_ref[...], wg_ref[...])
    sm = _sigmoid(gm_ref[...].astype(F32) + bm_ref[...])
    sg = _sigmoid(gg_ref[...].astype(F32) + bg_ref[...])
    o_ref[...] = (sm * ym + sg * yg).astype(o_ref.dtype)


def _branch_merge(o_mla, o_gdn, w_o_mla, w_o_gdn, gates, b_gate):
    s = o_mla.shape[0]
    d = w_o_mla.shape[1]
    tm = min(512, s)
    tn = min(512, d)
    nb = d // tn
    return pl.pallas_call(
        _branch_kernel,
        grid=(nb, s // tm),
        in_specs=[pl.BlockSpec((tm, o_mla.shape[1]), lambda j, i: (i, 0)),
                  pl.BlockSpec((tm, o_gdn.shape[1]), lambda j, i: (i, 0)),
                  pl.BlockSpec((w_o_mla.shape[0], tn), lambda j, i: (0, j)),
                  pl.BlockSpec((w_o_gdn.shape[0], tn), lambda j, i: (0, j)),
                  pl.BlockSpec((tm, tn), lambda j, i: (i, j)),
                  pl.BlockSpec((tm, tn), lambda j, i: (i, nb + j)),
                  pl.BlockSpec((1, tn), lambda j, i: (0, j)),
                  pl.BlockSpec((1, tn), lambda j, i: (0, nb + j))],
        out_specs=pl.BlockSpec((tm, tn), lambda j, i: (i, j)),
        out_shape=jax.ShapeDtypeStruct((s, d), BF16),
        compiler_params=_cparams("parallel", "parallel"),
        name="branch_merge",
    )(o_mla, o_gdn, w_o_mla, w_o_gdn, gates, gates, b_gate, b_gate)


def _out_ln_kernel(m_ref, w_ref, x_ref, g_ref, b_ref, o_ref):
    y = DN_ALPHA * x_ref[...] + _dot(m_ref[...], w_ref[...])
    o_ref[...] = _layernorm(y, g_ref[...], b_ref[...])


def _out_ln(mixed, w_out, x, g, b):
    s, d = x.shape
    tm = min(256, s)
    return pl.pallas_call(
        _out_ln_kernel,
        grid=(s // tm,),
        in_specs=[pl.BlockSpec((tm, d), lambda i: (i, 0)),
                  pl.BlockSpec((d, d), lambda i: (0, 0)),
                  pl.BlockSpec((tm, d), lambda i: (i, 0)),
                  pl.BlockSpec((1, d), lambda i: (0, 0)),
                  pl.BlockSpec((1, d), lambda i: (0, 0))],
        out_specs=pl.BlockSpec((tm, d), lambda i: (i, 0)),
        out_shape=jax.ShapeDtypeStruct((s, d), F32),
        compiler_params=_cparams("parallel"),
        name="out_ln1",
    )(mixed, w_out, x, g, b)


SLAB = 16


def _store_rows(ref2, val, per_token, row0=0):
    n = val.shape[0]
    for s in range(val.shape[1] // LANES):
        ref2[pl.ds(row0 + s, n, stride=per_token), :] = val[:, s * LANES:(s + 1) * LANES]


def _load_rows(ref2, n, per_token, row0, k):
    return jnp.concatenate([ref2[pl.ds(row0 + s, n, stride=per_token), :] for s in range(k)], axis=1)


def _mem_kernel(x_ref, wq_ref, kv_ref, wo_ref, g_ref, b_ref, wr_ref, br_ref, x2_ref, x2r_ref, lg_ref):
    x1 = x_ref[...]
    hd = MEM_HEAD_DIM
    nh = MEM_HEADS
    q = (_dot(x1.astype(BF16), wq_ref[...]) * float(hd ** -0.5)).astype(BF16)
    outs = []
    for h in range(nh):
        kh = kv_ref[:, h * hd:(h + 1) * hd]
        vh = kv_ref[:, (nh + h) * hd:(nh + h + 1) * hd]
        s = _dot_nt(q[:, h * hd:(h + 1) * hd], kh)
        p = jnp.exp(s - jnp.max(s, axis=-1, keepdims=True))
        o = _dot(p.astype(BF16), vh) / jnp.sum(p, axis=-1, keepdims=True)
        outs.append(o.astype(BF16))
    o = jnp.concatenate(outs, axis=1)
    y = DN_ALPHA * x1 + _dot(o, wo_ref[...])
    x2 = _layernorm(y, g_ref[...], b_ref[...])
    x2_ref[...] = x2
    _store_rows(x2r_ref, x2, SLAB)
    x_hi = x2.astype(BF16)
    x_lo = (x2 - x_hi.astype(F32)).astype(BF16)
    w = wr_ref[...]
    w_hi = w.astype(BF16)
    w_lo = (w - w_hi.astype(F32)).astype(BF16)
    lg_ref[...] = (_dot(x_hi, w_hi) + _dot(x_lo, w_hi)) + _dot(x_hi, w_lo) + br_ref[...]


def _mem_attn_ln(x1, w_mq, kvm, w_mo, g, b, w_r, b_r):
    s, d = x1.shape
    tm = min(256, s)
    return pl.pallas_call(
        _mem_kernel,
        grid=(s // tm,),
        in_specs=[pl.BlockSpec((tm, d), lambda i: (i, 0)),
                  pl.BlockSpec(w_mq.shape, lambda i: (0, 0)),
                  pl.BlockSpec(kvm.shape, lambda i: (0, 0)),
                  pl.BlockSpec(w_mo.shape, lambda i: (0, 0)),
                  pl.BlockSpec((1, d), lambda i: (0, 0)),
                  pl.BlockSpec((1, d), lambda i: (0, 0)),
                  pl.BlockSpec(w_r.shape, lambda i: (0, 0)),
                  pl.BlockSpec((1, LANES), lambda i: (0, 0))],
        out_specs=[pl.BlockSpec((tm, d), lambda i: (i, 0)),
                   pl.BlockSpec((tm * SLAB, LANES), lambda i: (i, 0)),
                   pl.BlockSpec((tm, LANES), lambda i: (i, 0))],
        out_shape=[jax.ShapeDtypeStruct((s, d), F32),
                   jax.ShapeDtypeStruct((s * SLAB, LANES), F32),
                   jax.ShapeDtypeStruct((s, LANES), F32)],
        compiler_params=_cparams("parallel"),
        name="mem_attn_ln2",
    )(x1, w_mq, kvm, w_mo, g, b, w_r, b_r)


SEL_E1, SEL_E2, SEL_R1, SEL_R2, SEL_G1, SEL_G2 = range(6)
GRP_LANE0 = N_EXPERTS


def _route_kernel(lg_ref, sel_ref, cnt_ref, carry):
    i = pl.program_id(0)

    @pl.when(i == 0)
    def _():
        carry[...] = jnp.zeros(carry.shape, F32)

    lg = lg_ref[...]
    tm = lg.shape[0]
    lane = lax.broadcasted_iota(jnp.int32, lg.shape, 1)
    big = jnp.int32(4 * LANES)
    neg = jnp.float32(-jnp.inf)

    def first_max(vals):
        mx = jnp.max(vals, axis=-1, keepdims=True)
        idx = jnp.min(jnp.where(vals == mx, lane, big), axis=-1, keepdims=True)
        return mx, idx

    is_grp = (lane >= GRP_LANE0) & (lane < GRP_LANE0 + N_GROUPS)
    gl = jnp.where(is_grp, lg, neg)
    gmax, gidx = first_max(gl)
    p_top = 1.0 / jnp.sum(jnp.where(is_grp, jnp.exp(gl - gmax), 0.0), axis=-1, keepdims=True)
    lo = (gidx - GRP_LANE0) * EXPERTS_PER_GROUP
    in_grp = (lane >= lo) & (lane < lo + EXPERTS_PER_GROUP)
    el = jnp.where(in_grp, lg, neg)
    m1, i1 = first_max(el)
    m2, i2 = first_max(jnp.where(lane == i1, neg, el))
    r = jnp.exp(m2 - m1)
    g1 = p_top / (1.0 + r)
    g2 = p_top * r / (1.0 + r)

    hot1 = lane == i1
    hot2 = lane == i2
    onehot = jnp.where(hot1, 1.0, 0.0) + jnp.where(hot2, 1.0, 0.0)
    ri = lax.broadcasted_iota(jnp.int32, (tm, tm), 0)
    ci = lax.broadcasted_iota(jnp.int32, (tm, tm), 1)
    before = jnp.where(ci < ri, 1.0, 0.0).astype(BF16)
    rank = _dot(before, onehot.astype(BF16)) + carry[...]
    r1 = jnp.sum(jnp.where(hot1, rank, 0.0), axis=-1, keepdims=True)
    r2 = jnp.sum(jnp.where(hot2, rank, 0.0), axis=-1, keepdims=True)
    carry[...] = carry[...] + jnp.sum(onehot, axis=0, keepdims=True)
    cnt_ref[...] = carry[...]

    out = jnp.zeros(lg.shape, F32)
    for ln, val in ((SEL_E1, i1.astype(F32)), (SEL_E2, i2.astype(F32)), (SEL_R1, r1), (SEL_R2, r2),
                    (SEL_G1, g1), (SEL_G2, g2)):
        out = jnp.where(lane == ln, val, out)
    sel_ref[...] = out


def _route(logits):
    t = logits.shape[0]
    tm = min(512, t)
    return pl.pallas_call(
        _route_kernel,
        grid=(t // tm,),
        in_specs=[pl.BlockSpec((tm, LANES), lambda i: (i, 0))],
        out_specs=[pl.BlockSpec((tm, LANES), lambda i: (i, 0)),
                   pl.BlockSpec((1, LANES), lambda i: (0, 0))],
        out_shape=[jax.ShapeDtypeStruct((t, LANES), F32),
                   jax.ShapeDtypeStruct((1, LANES), F32)],
        scratch_shapes=[pltpu.VMEM((1, LANES), F32)],
        compiler_params=_cparams("arbitrary"),
        name="moe_route",
    )(logits)


def _dispatch_kernel(cnt_ref, e1_ref, e2_ref, r1_ref, r2_ref, rowa_ref, blk_ref, rows_ref, pstart, *, t, n_blocks):
    def seg(e, start):
        pstart[e] = start
        c = cnt_ref[e]
        nb = (c + MOE_ROWS - 1) // MOE_ROWS
        end = start + nb * MOE_ROWS

        def mark(p, carry):
            rowa_ref[p] = -1
            return carry
        lax.fori_loop(start + c, end, mark, 0)

        def blk(bi, carry):
            blk_ref[bi] = e
            rows_ref[bi] = jnp.minimum(start + c - bi * MOE_ROWS, MOE_ROWS)
            return carry
        lax.fori_loop(start // MOE_ROWS, end // MOE_ROWS, blk, 0)
        return end

    total = lax.fori_loop(0, N_EXPERTS, seg, 0)
    used = total // MOE_ROWS
    last = blk_ref[jnp.maximum(used - 1, 0)]

    def tail(bi, carry):
        blk_ref[bi] = last
        rows_ref[bi] = 0

        def unused(p, c2):
            rowa_ref[bi * MOE_ROWS + p] = -1
            return c2
        lax.fori_loop(0, MOE_ROWS, unused, 0, unroll=16)
        return carry
    lax.fori_loop(used, n_blocks, tail, 0)

    def place(tok, carry):
        rowa_ref[pstart[e1_ref[tok]] + r1_ref[tok]] = 2 * tok
        rowa_ref[pstart[e2_ref[tok]] + r2_ref[tok]] = 2 * tok + 1
        return carry
    lax.fori_loop(0, t, place, 0, unroll=8)


def _dispatch(cnt, e1, e2, r1, r2, n_blocks):
    t = e1.shape[0]
    smem = pl.BlockSpec(memory_space=pltpu.SMEM)
    return pl.pallas_call(
        functools.partial(_dispatch_kernel, t=t, n_blocks=n_blocks),
        in_specs=[smem] * 5,
        out_specs=[smem] * 3,
        out_shape=[jax.ShapeDtypeStruct((n_blocks * MOE_ROWS,), jnp.int32),
                   jax.ShapeDtypeStruct((n_blocks,), jnp.int32),
                   jax.ShapeDtypeStruct((n_blocks,), jnp.int32)],
        scratch_shapes=[pltpu.SMEM((N_EXPERTS,), jnp.int32)],
        name="moe_dispatch",
    )(cnt, e1, e2, r1, r2)


def _expert_kernel(blk_ref, rowa_ref, rows_ref, x_hbm, wg_ref, wu_ref, wd_ref, out_hbm, xbuf, ybuf, gsem, ssem):
    b = pl.program_id(0)
    n_blocks = pl.num_programs(0)
    slot = b % 2

    def slab(i):
        return pl.ds(pl.multiple_of(i * SLAB, SLAB), SLAB)

    def gather_copy(blk_slot, r, tok):
        return pltpu.make_async_copy(x_hbm.at[slab(tok)], xbuf.at[blk_slot, slab(r)], gsem.at[blk_slot])

    def scatter_copy(blk_slot, r, dst):
        return pltpu.make_async_copy(ybuf.at[blk_slot, slab(r)], out_hbm.at[slab(dst)], ssem.at[blk_slot])

    def issue_gathers(blk, blk_slot):
        def body(r, carry):
            a = rowa_ref[blk * MOE_ROWS + r]
            gather_copy(blk_slot, r, jnp.maximum(a, 0) // 2).start(priority=1)
            return carry
        lax.fori_loop(0, MOE_ROWS, body, 0, unroll=8)

    def wait_scatters(blk, blk_slot):
        n = rows_ref[blk]

        @pl.when(n > 0)
        def _():
            rows = pl.ds(0, pl.multiple_of(n * SLAB, SLAB))
            pltpu.make_async_copy(ybuf.at[blk_slot, rows], out_hbm.at[rows], ssem.at[blk_slot]).wait()

    @pl.when((b == 0) & (rows_ref[0] > 0))
    def _():
        issue_gathers(0, 0)

    nxt = jnp.minimum(b + 1, n_blocks - 1)

    @pl.when((b + 1 < n_blocks) & (rows_ref[nxt] > 0))
    def _():
        issue_gathers(nxt, 1 - slot)

    @pl.when(b >= 2)
    def _():
        wait_scatters(jnp.maximum(b - 2, 0), slot)

    n_rows = rows_ref[b]

    @pl.when(n_rows > 0)
    def _():
        pltpu.make_async_copy(x_hbm.at[pl.ds(0, MOE_ROWS * SLAB)], xbuf.at[slot], gsem.at[slot]).wait()
        xb = _load_rows(xbuf.at[slot], MOE_ROWS, SLAB, 0, SLAB).astype(BF16)
        hg = _dot(xb, wg_ref[...].astype(BF16))
        hu = _dot(xb, wu_ref[...].astype(BF16))
        hb = (hg * _sigmoid(hg) * hu).astype(BF16)
        _store_rows(ybuf.at[slot], _dot(hb, wd_ref[...].astype(BF16)), SLAB)

        def issue_scatter(r, carry):
            scatter_copy(slot, r, rowa_ref[b * MOE_ROWS + r]).start()
            return carry
        lax.fori_loop(0, n_rows, issue_scatter, 0)

    @pl.when(b == n_blocks - 1)
    def _():
        @pl.when(b >= 1)
        def _():
            wait_scatters(jnp.maximum(b - 1, 0), 1 - slot)
        wait_scatters(b, slot)


def _experts(x2r, row_a, blk_exp, blk_rows, w_gate_e, w_up_e, w_down_e):
    t = x2r.shape[0] // SLAB
    d = SLAB * LANES
    n_blocks = blk_exp.shape[0]
    de = w_gate_e.shape[2]
    grid_spec = pltpu.PrefetchScalarGridSpec(
        num_scalar_prefetch=3,
        grid=(n_blocks,),
        in_specs=[pl.BlockSpec(memory_space=pl.ANY),
                  pl.BlockSpec((None, d, de), lambda b, blk, rowa, nb: (blk[b], 0, 0)),
                  pl.BlockSpec((None, d, de), lambda b, blk, rowa, nb: (blk[b], 0, 0)),
                  pl.BlockSpec((None, de, d), lambda b, blk, rowa, nb: (blk[b], 0, 0))],
        out_specs=pl.BlockSpec(memory_space=pl.ANY),
        scratch_shapes=[pltpu.VMEM((2, MOE_ROWS * SLAB, LANES), F32), pltpu.VMEM((2, MOE_ROWS * SLAB, LANES), F32),
                        pltpu.SemaphoreType.DMA((2,)), pltpu.SemaphoreType.DMA((2,))],
    )
    return pl.pallas_call(
        _expert_kernel,
        grid_spec=grid_spec,
        out_shape=jax.ShapeDtypeStruct((2 * t * SLAB, LANES), F32),
        compiler_params=_cparams("arbitrary"),
        name="moe_experts",
    )(blk_exp, row_a, blk_rows, x2r, w_gate_e, w_up_e, w_down_e)


def _combine_kernel(x_ref, y_ref, sel_ref, g_ref, b_ref, o_ref):
    tm = x_ref.shape[0]
    sel = sel_ref[...]
    g1 = sel[:, SEL_G1:SEL_G1 + 1]
    g2 = sel[:, SEL_G2:SEL_G2 + 1]
    y1 = _load_rows(y_ref, tm, 2 * SLAB, 0, SLAB)
    y2 = _load_rows(y_ref, tm, 2 * SLAB, SLAB, SLAB)
    y = DN_ALPHA * x_ref[...] + (g1 * y1 + g2 * y2)
    o_ref[...] = _layernorm(y, g_ref[...], b_ref[...])


def _combine_ln(x2, ys, sel, g, b):
    t, d = x2.shape
    tm = min(256, t)
    return pl.pallas_call(
        _combine_kernel,
        grid=(t // tm,),
        in_specs=[pl.BlockSpec((tm, d), lambda i: (i, 0)),
                  pl.BlockSpec((tm * 2 * SLAB, LANES), lambda i: (i, 0)),
                  pl.BlockSpec((tm, LANES), lambda i: (i, 0)),
                  pl.BlockSpec((1, d), lambda i: (0, 0)),
                  pl.BlockSpec((1, d), lambda i: (0, 0))],
        out_specs=pl.BlockSpec((tm, d), lambda i: (i, 0)),
        out_shape=jax.ShapeDtypeStruct((t, d), F32),
        compiler_params=_cparams("parallel"),
        name="combine_ln3",
    )(x2, ys, sel, g, b)


def _mixer(xb, positions, w_in, b_gate, mla_q_norm, w_uq, mla_kv_norm, w_ukv, w_o_mla,
           gdn_conv, gdn_a_log, gdn_dt_bias, gdn_norm, w_o_gdn):
    s, d = xb.shape
    nqkv = 2 * GDN_QK_HEADS * GDN_DK + GDN_V_HEADS * GDN_DV
    nz = GDN_V_HEADS * GDN_DV
    o0 = 0
    o1 = o0 + MLA_Q_LORA
    o2 = o1 + MLA_KV_LORA + MLA_ROPE
    o3 = o2 + nqkv
    o4 = o3 + nz
    o5 = o4 + GDN_V_HEADS
    o6 = o5 + GDN_V_HEADS

    def rot_cols(w):
        half = MLA_ROPE // 2
        return jnp.concatenate([-w[..., half:], w[..., :half]], axis=-1)

    w_q = w_in[:, o0:o1].astype(BF16)
    w_kpe = w_in[:, o1 + MLA_KV_LORA:o2]
    w_kv = jnp.concatenate([w_in[:, o1:o1 + MLA_KV_LORA], w_kpe, rot_cols(w_kpe)], axis=1).astype(BF16)
    w_ba = jnp.concatenate([w_in[:, o4:o6], jnp.zeros((d, LANES - 2 * GDN_V_HEADS), F32)], axis=1).astype(BF16)
    qd = _matmul(xb, w_q, BF16, 512, MLA_Q_LORA)
    kvd = _matmul(xb, w_kv, BF16, 512, w_kv.shape[1])
    qkv_c = _qkv_conv_proj(xb, w_in[:, o2:o3].astype(BF16), gdn_conv)
    hz = _matmul(xb, w_in[:, o3:o4].astype(BF16), BF16, 512, 1024)
    hba = _matmul(xb, w_ba, F32, 512, LANES)
    hgate = _matmul(xb, w_in[:, o6:].astype(BF16), BF16, 512, 1024)

    cs = _rope_table(positions)
    hq = MLA_NOPE + MLA_ROPE
    wq3 = w_uq.reshape(MLA_Q_LORA, MLA_HEADS, hq)
    pe = wq3[..., MLA_NOPE:]
    wq = jnp.concatenate([pe, rot_cols(pe), wq3[..., :MLA_NOPE]], axis=-1)
    wq = wq.reshape(MLA_Q_LORA, MLA_HEADS * MLA_QK_PAD).astype(BF16)
    wkv3 = w_ukv.reshape(MLA_KV_LORA, MLA_HEADS, MLA_NOPE + MLA_V)
    wk = wkv3[..., :MLA_NOPE].reshape(MLA_KV_LORA, MLA_HEADS * MLA_NOPE).astype(BF16)
    wv = wkv3[..., MLA_NOPE:].reshape(MLA_KV_LORA, MLA_HEADS * MLA_V).astype(BF16)
    q = _mla_q_proj(qd, mla_q_norm.reshape(1, -1), wq, cs)
    k, v = _mla_kv_proj(kvd, mla_kv_norm.reshape(1, -1), wk, wv, cs)
    o_mla = _mla_attention(q, k, v)

    o_gdn = _gdn_delta(qkv_c, hz, hba[:, GDN_V_HEADS:2 * GDN_V_HEADS], hba[:, :GDN_V_HEADS],
                       gdn_a_log, gdn_dt_bias, gdn_norm.reshape(1, -1))

    return _branch_merge(o_mla, o_gdn, w_o_mla.astype(BF16), w_o_gdn.astype(BF16), hgate, b_gate.reshape(1, -1))


def _moe(x2, x2r, logits, w_gate_e, w_up_e, w_down_e, ln_g, ln_b):
    t, d = x2.shape
    n_blocks = (2 * t) // MOE_ROWS + N_EXPERTS
    sel, cnt = _route(logits)
    as_i32 = lambda col: sel[:, col].astype(jnp.int32)
    row_a, blk_exp, blk_rows = _dispatch(cnt[0, :N_EXPERTS].astype(jnp.int32), as_i32(SEL_E1), as_i32(SEL_E2),
                                     as_i32(SEL_R1), as_i32(SEL_R2), n_blocks)
    ys = _experts(x2r, row_a, blk_exp, blk_rows, w_gate_e, w_up_e, w_down_e)
    return _combine_ln(x2, ys, sel, ln_g, ln_b)


def _layer(x, mem, positions, w_in, b_gate, mla_q_norm, w_uq, mla_kv_norm, w_ukv, w_o_mla,
           gdn_conv, gdn_a_log, gdn_dt_bias, gdn_norm, w_o_gdn, w_out, ln1_g, ln1_b,
           w_mq, w_mkv, w_mo, ln2_g, ln2_b, w_route_grp, b_route_grp, w_route_exp, b_route_exp,
           w_gate_e, w_up_e, w_down_e, ln3_g, ln3_b):
    d = x.shape[1]
    row = lambda p: p.reshape(1, -1)
    mixed = _mixer(x.astype(BF16), positions, w_in, b_gate, mla_q_norm, w_uq, mla_kv_norm, w_ukv, w_o_mla,
                   gdn_conv, gdn_a_log, gdn_dt_bias, gdn_norm, w_o_gdn)
    x1 = _out_ln(mixed, w_out.astype(BF16), x, row(ln1_g), row(ln1_b))

    kvm = _matmul(mem.astype(BF16), w_mkv.astype(BF16), BF16, 256, 512)
    pad = LANES - N_EXPERTS - N_GROUPS
    w_r = jnp.concatenate([w_route_exp, w_route_grp, jnp.zeros((d, pad), F32)], axis=1)
    b_r = jnp.concatenate([b_route_exp, b_route_grp, jnp.zeros((pad,), F32)]).reshape(1, LANES)
    x2, x2r, logits = _mem_attn_ln(x1, w_mq.astype(BF16), kvm, w_mo.astype(BF16), row(ln2_g), row(ln2_b), w_r, b_r)

    return _moe(x2, x2r, logits, w_gate_e, w_up_e, w_down_e, row(ln3_g), row(ln3_b))


def kernel(x, mem, positions, w_in, b_gate, mla_q_norm, w_uq, mla_kv_norm, w_ukv, w_o_mla, gdn_conv, gdn_a_log,
           gdn_dt_bias, gdn_norm, w_o_gdn, w_out, ln1_g, ln1_b, w_mq, w_mkv, w_mo, ln2_g, ln2_b, w_route_grp,
           b_route_grp, w_route_exp, b_route_exp, w_gate_e, w_up_e, w_down_e, ln3_g, ln3_b):
    outs = []
    for bi in range(x.shape[0]):
        h = x[bi]
        for l in range(w_in.shape[0]):
            h = _layer(h, mem[bi], positions[bi], w_in[l], b_gate[l], mla_q_norm[l], w_uq[l], mla_kv_norm[l],
                       w_ukv[l], w_o_mla[l], gdn_conv[l], gdn_a_log[l], gdn_dt_bias[l], gdn_norm[l], w_o_gdn[l],
                       w_out[l], ln1_g[l], ln1_b[l], w_mq[l], w_mkv[l], w_mo[l], ln2_g[l], ln2_b[l],
                       w_route_grp[l], b_route_grp[l], w_route_exp[l], b_route_exp[l], w_gate_e[l], w_up_e[l],
                       w_down_e[l], ln3_g[l], ln3_b[l])
        outs.append(h)
    return jnp.stack(outs, axis=0)
```

```python
import functools

import numpy as np
import jax
import jax.numpy as jnp
from jax import lax
from jax.experimental import pallas as pl
from jax.experimental.pallas import tpu as pltpu

F32 = jnp.float32
BF16 = jnp.bfloat16
HIGHEST = lax.Precision.HIGHEST

LANES = 128
VMEM_LIMIT = 56 * 1024 * 1024

CHUNK = 64
MLA_HEADS = 16
MLA_Q_LORA = 768
MLA_KV_LORA = 512
MLA_NOPE = 128
MLA_ROPE = 64
MLA_V = 128
MLA_QK_PAD = 256
ROPE_THETA = 10000.0
GDN_QK_HEADS = 16
GDN_V_HEADS = 32
GDN_DK = 128
GDN_DV = 128
GDN_CONV = 4
GDN_GROUP = 32
MEM_HEADS = 4
MEM_HEAD_DIM = 128
N_GROUPS = 8
EXPERTS_PER_GROUP = 8
N_EXPERTS = 64
D_EXPERT = 512
MOE_ROWS = 256
RMS_EPS = 1e-6
LN_EPS = 1e-5
DN_ALPHA = 2.0 ** 0.25


def _cparams(*sem):
    return pltpu.CompilerParams(dimension_semantics=sem, vmem_limit_bytes=VMEM_LIMIT)


def _dot(a, b, **kw):
    return jnp.dot(a, b, preferred_element_type=F32, **kw)


def _dot_nt(a, b):
    return lax.dot_general(a, b, (((1,), (1,)), ((), ())), preferred_element_type=F32)


def _dot_tn(a, b):
    return lax.dot_general(a, b, (((0,), (0,)), ((), ())), preferred_element_type=F32)


def _sigmoid(x):
    return 1.0 / (1.0 + jnp.exp(-x))


def _layernorm(y, g, b):
    mu = jnp.mean(y, axis=-1, keepdims=True)
    d = y - mu
    var = jnp.mean(d * d, axis=-1, keepdims=True)
    return d * lax.rsqrt(var + LN_EPS) * g + b


def _mm_kernel(x_ref, w_ref, o_ref):
    o_ref[...] = _dot(x_ref[...], w_ref[...]).astype(o_ref.dtype)


def _matmul(x, w, out_dtype, tm, tn):
    m, k = x.shape
    n = w.shape[1]
    tm, tn = min(tm, m), min(tn, n)
    return pl.pallas_call(
        _mm_kernel,
        grid=(n // tn, m // tm),
        in_specs=[pl.BlockSpec((tm, k), lambda j, i: (i, 0)),
                  pl.BlockSpec((k, tn), lambda j, i: (0, j))],
        out_specs=pl.BlockSpec((tm, tn), lambda j, i: (i, j)),
        out_shape=jax.ShapeDtypeStruct((m, n), out_dtype),
        compiler_params=_cparams("parallel", "parallel"),
        name="matmul",
    )(x, w)


def _rope_table_kernel(pos_ref, inv_ref, o_ref):
    ang = pos_ref[...].astype(F32) * inv_ref[...]
    lane = lax.broadcasted_iota(jnp.int32, ang.shape, 1)
    o_ref[...] = jnp.where(lane < MLA_ROPE, jnp.cos(ang), jnp.sin(ang))


def _rope_table(positions):
    s = positions.shape[0]
    inv = 1.0 / (ROPE_THETA ** (np.arange(0, MLA_ROPE, 2, dtype=np.float32) / MLA_ROPE))
    inv4 = jnp.asarray(np.tile(inv.astype(np.float32), 4)[None, :])
    tm = min(512, s)
    return pl.pallas_call(
        _rope_table_kernel,
        grid=(s // tm,),
        in_specs=[pl.BlockSpec((tm, 1), lambda i: (i, 0)),
                  pl.BlockSpec((1, LANES), lambda i: (0, 0))],
        out_specs=pl.BlockSpec((tm, LANES), lambda i: (i, 0)),
        out_shape=jax.ShapeDtypeStruct((s, LANES), F32),
        compiler_params=_cparams("parallel"),
        name="rope_table",
    )(positions.reshape(s, 1), inv4)


def _rope_pair(t, cs):
    a = t * cs
    return a + pltpu.roll(a, MLA_ROPE, axis=1)


def _mla_q_kernel(qd_ref, qn_ref, w_ref, cs_ref, o_ref, *, scale):
    x = qd_ref[...].astype(F32)
    cq = x * lax.rsqrt(jnp.mean(x * x, axis=-1, keepdims=True) + RMS_EPS) * qn_ref[...]
    cqb = cq.astype(BF16)
    cs = cs_ref[...]
    for h in range(MLA_HEADS):
        lo = h * MLA_QK_PAD
        p = _dot(cqb, w_ref[:, lo:lo + MLA_QK_PAD])
        o_ref[:, lo:lo + LANES] = (_rope_pair(p[:, :LANES], cs) * scale).astype(o_ref.dtype)
        o_ref[:, lo + LANES:lo + MLA_QK_PAD] = (p[:, LANES:] * scale).astype(o_ref.dtype)


def _mla_q_proj(qd, q_norm, wq, cs):
    s = qd.shape[0]
    tm = min(256, s)
    n = MLA_HEADS * MLA_QK_PAD
    scale = float((MLA_NOPE + MLA_ROPE) ** -0.5 * np.log2(np.e))
    return pl.pallas_call(
        functools.partial(_mla_q_kernel, scale=scale),
        grid=(s // tm,),
        in_specs=[pl.BlockSpec((tm, MLA_Q_LORA), lambda i: (i, 0)),
                  pl.BlockSpec((1, MLA_Q_LORA), lambda i: (0, 0)),
                  pl.BlockSpec((MLA_Q_LORA, n), lambda i: (0, 0)),
                  pl.BlockSpec((tm, LANES), lambda i: (i, 0))],
        out_specs=pl.BlockSpec((tm, n), lambda i: (i, 0)),
        out_shape=jax.ShapeDtypeStruct((s, n), BF16),
        compiler_params=_cparams("parallel"),
        name="mla_q_proj",
    )(qd, q_norm, wq, cs)


def _mla_kv_kernel(kv_ref, kn_ref, wk_ref, wv_ref, cs_ref, k_ref, v_ref):
    x = kv_ref[:, :MLA_KV_LORA].astype(F32)
    ckv = x * lax.rsqrt(jnp.mean(x * x, axis=-1, keepdims=True) + RMS_EPS) * kn_ref[...]
    cb = ckv.astype(BF16)
    pe = _rope_pair(kv_ref[:, MLA_KV_LORA:].astype(F32), cs_ref[...])
    lane = lax.broadcasted_iota(jnp.int32, pe.shape, 1)
    pe = jnp.where(lane < MLA_ROPE, pe, 0.0).astype(k_ref.dtype)
    kn = _dot(cb, wk_ref[...]).astype(k_ref.dtype)
    for h in range(MLA_HEADS):
        lo = h * MLA_QK_PAD
        k_ref[:, lo:lo + LANES] = pe
        k_ref[:, lo + LANES:lo + MLA_QK_PAD] = kn[:, h * MLA_NOPE:(h + 1) * MLA_NOPE]
    v_ref[...] = _dot(cb, wv_ref[...]).astype(v_ref.dtype)


def _mla_kv_proj(kvd, kv_norm, wk, wv, cs):
    s, w = kvd.shape
    tm = min(256, s)
    nk = MLA_HEADS * MLA_QK_PAD
    nv = MLA_HEADS * MLA_V
    return pl.pallas_call(
        _mla_kv_kernel,
        grid=(s // tm,),
        in_specs=[pl.BlockSpec((tm, w), lambda i: (i, 0)),
                  pl.BlockSpec((1, MLA_KV_LORA), lambda i: (0, 0)),
                  pl.BlockSpec(wk.shape, lambda i: (0, 0)),
                  pl.BlockSpec(wv.shape, lambda i: (0, 0)),
                  pl.BlockSpec((tm, LANES), lambda i: (i, 0))],
        out_specs=[pl.BlockSpec((tm, nk), lambda i: (i, 0)),
                   pl.BlockSpec((tm, nv), lambda i: (i, 0))],
        out_shape=[jax.ShapeDtypeStruct((s, nk), BF16),
                   jax.ShapeDtypeStruct((s, nv), BF16)],
        compiler_params=_cparams("parallel"),
        name="mla_kv_proj",
    )(kvd, kv_norm, wk, wv, cs)


FLASH_HEADS = 2
FLASH_TILE = 1024


def _lane_repeat(x, n):
    return jnp.concatenate([x] * n, axis=1)


def _flash_kernel(qi_ref, kj_ref, q_ref, k_ref, v_ref, o_ref, m_sc, acc_sc):
    t = pl.program_id(1)
    i = qi_ref[t]
    j = kj_ref[t]
    hps = m_sc.shape[0]
    tk = k_ref.shape[0]
    heads = range(hps)

    @pl.when(j == 0)
    def _():
        m_sc[...] = jnp.full(m_sc.shape, -1e30, F32)
        acc_sc[...] = jnp.zeros(acc_sc.shape, F32)

    def update(r0, nr, nk, masked):
        rows = slice(r0, r0 + nr)
        ss = [_dot_nt(q_ref[rows, h * MLA_QK_PAD:(h + 1) * MLA_QK_PAD], k_ref[0:nk, h * MLA_QK_PAD:(h + 1) * MLA_QK_PAD])
              for h in heads]
        if masked:
            r = (lax.broadcasted_iota(jnp.int32, ss[0].shape, 0) + r0) // CHUNK
            c = lax.broadcasted_iota(jnp.int32, ss[0].shape, 1) // CHUNK
            keep = c <= r
            ss = [jnp.where(keep, s, -1e30) for s in ss]
        m_prev = [m_sc[h, rows] for h in heads]
        m_new = [jnp.maximum(mp, jnp.max(s, axis=-1, keepdims=True)) for mp, s in zip(m_prev, ss)]
        alpha = [jnp.exp2(mp - mn) for mp, mn in zip(m_prev, m_new)]
        ps = [jnp.exp2(s - _lane_repeat(mn, nk // LANES)).astype(BF16) for s, mn in zip(ss, m_new)]
        ones = jnp.ones((nk, MLA_V), BF16)
        pv = [_dot(p, jnp.concatenate([v_ref[0:nk, h * MLA_V:(h + 1) * MLA_V], ones], axis=1))
              for h, p in zip(heads, ps)]
        for h in heads:
            acc_sc[h, rows] = _lane_repeat(alpha[h], 2) * acc_sc[h, rows] + pv[h]
            m_sc[h, rows] = m_new[h]

    @pl.when(j < i)
    def _():
        update(0, tk, tk, False)

    @pl.when(j == i)
    def _():
        half = tk // 2
        if half % CHUNK == 0 and half % LANES == 0:
            update(0, half, half, True)
            update(half, half, tk, True)
        else:
            update(0, tk, tk, True)
        for h in heads:
            acc = acc_sc[h]
            o_ref[:, h * MLA_V:(h + 1) * MLA_V] = (acc[:, :MLA_V] / acc[:, MLA_V:]).astype(o_ref.dtype)


def _mla_attention(q, k, v):
    s = q.shape[0]
    t = min(FLASH_TILE, s)
    n = s // t
    hps = FLASH_HEADS
    qi = np.array([i for i in range(n) for _ in range(i + 1)], np.int32)
    kj = np.array([j for i in range(n) for j in range(i + 1)], np.int32)
    grid_spec = pltpu.PrefetchScalarGridSpec(
        num_scalar_prefetch=2,
        grid=(MLA_HEADS // hps, qi.shape[0]),
        in_specs=[pl.BlockSpec((t, hps * MLA_QK_PAD), lambda h, st, qi_r, kj_r: (qi_r[st], h)),
                  pl.BlockSpec((t, hps * MLA_QK_PAD), lambda h, st, qi_r, kj_r: (kj_r[st], h)),
                  pl.BlockSpec((t, hps * MLA_V), lambda h, st, qi_r, kj_r: (kj_r[st], h))],
        out_specs=pl.BlockSpec((t, hps * MLA_V), lambda h, st, qi_r, kj_r: (qi_r[st], h)),
        scratch_shapes=[pltpu.VMEM((hps, t, LANES), F32), pltpu.VMEM((hps, t, 2 * MLA_V), F32)],
    )
    return pl.pallas_call(
        _flash_kernel,
        grid_spec=grid_spec,
        out_shape=jax.ShapeDtypeStruct((s, MLA_HEADS * MLA_V), BF16),
        compiler_params=_cparams("parallel", "arbitrary"),
        name="mla_flash",
    )(jnp.asarray(qi), jnp.asarray(kj), q, k, v)


CONV_HALO = 8


def _qkv_conv_kernel(x_ref, w_ref, cw_ref, o_ref, buf, *, tm, tn, n_qk_blocks):
    c = pl.program_id(0)
    i = pl.program_id(1)
    halo = CONV_HALO

    @pl.when(i == 0)
    def _():
        buf[0:halo, :] = jnp.zeros((halo, tn), F32)

    h = _dot(x_ref[...], w_ref[...])
    buf[halo:halo + tm, :] = h
    ext = buf[...]
    y = cw_ref[GDN_CONV - 1:GDN_CONV, :] * h
    for k in range(1, GDN_CONV):
        y = y + cw_ref[GDN_CONV - 1 - k:GDN_CONV - k, :] * pltpu.roll(ext, k, axis=0)[halo:halo + tm, :]
    buf[0:halo, :] = h[tm - halo:tm, :]
    y = y * (0.5 * jnp.tanh(0.5 * y) + 0.5)
    is_qk = c < n_qk_blocks
    for g in range(tn // LANES):
        seg = y[:, g * LANES:(g + 1) * LANES]
        inv = lax.rsqrt(jnp.sum(seg * seg, axis=-1, keepdims=True) + 1e-6)
        o_ref[:, g * LANES:(g + 1) * LANES] = (seg * jnp.where(is_qk, inv, 1.0)).astype(o_ref.dtype)


def _qkv_conv_proj(xb, w_qkv, conv_w):
    s, d = xb.shape
    c = w_qkv.shape[1]
    tm = min(512, s)
    tn = 512
    n_qk_blocks = (2 * GDN_QK_HEADS * GDN_DK) // tn
    kern = functools.partial(_qkv_conv_kernel, tm=tm, tn=tn, n_qk_blocks=n_qk_blocks)
    return pl.pallas_call(
        kern,
        grid=(c // tn, s // tm),
        in_specs=[pl.BlockSpec((tm, d), lambda cc, i: (i, 0)),
                  pl.BlockSpec((d, tn), lambda cc, i: (0, cc)),
                  pl.BlockSpec((GDN_CONV, tn), lambda cc, i: (0, cc))],
        out_specs=pl.BlockSpec((tm, tn), lambda cc, i: (i, cc)),
        out_shape=jax.ShapeDtypeStruct((s, c), BF16),
        scratch_shapes=[pltpu.VMEM((tm + CONV_HALO, tn), F32)],
        compiler_params=_cparams("parallel", "arbitrary"),
        name="qkv_conv_proj",
    )(xb, w_qkv, conv_w)


def _softplus(x):
    return jnp.maximum(x, 0.0) + jnp.log1p(jnp.exp(-jnp.abs(x)))


def _gdn_chunk_kernel(q_ref, k_ref, v_ref, z_ref, ac_ref, bc_ref, ar_ref, alc_ref, dtc_ref, alr_ref, dtr_ref,
                      nw_ref, o_ref, state):
    n = pl.program_id(1)
    g_heads = GDN_GROUP
    c = CHUNK

    @pl.when(n == 0)
    def _():
        state[...] = jnp.zeros(state.shape, F32)

    ri = lax.broadcasted_iota(jnp.int32, (c, c), 0)
    ci = lax.broadcasted_iota(jnp.int32, (c, c), 1)
    tril = ci <= ri
    strict = ci < ri
    ltri = tril.astype(F32)
    utri = (ri <= ci).astype(F32)
    eye = (ri == ci).astype(F32)

    g_col = -jnp.exp(alc_ref[...]) * _softplus(ac_ref[...] + dtc_ref[...])
    gc_col = _dot(ltri, g_col, precision=HIGHEST)
    g_row = -jnp.exp(alr_ref[...]) * _softplus(ar_ref[...] + dtr_ref[...])
    gc_row = _dot(g_row, utri, precision=HIGHEST)
    beta_col = _sigmoid(bc_ref[...])
    nw = nw_ref[...]
    scale = float(GDN_DK ** -0.5)

    vheads = range(g_heads)
    qkheads = range(g_heads // 2)
    qs = [q_ref[:, h * GDN_DK:(h + 1) * GDN_DK] for h in qkheads]
    ks = [k_ref[:, h * GDN_DK:(h + 1) * GDN_DK] for h in qkheads]
    kqs = [_dot_nt(jnp.concatenate([ks[h], qs[h]], axis=0), ks[h]) for h in qkheads]
    bcol = [beta_col[:, h:h + 1] for h in vheads]
    gcol = [gc_col[:, h:h + 1] for h in vheads]
    glast = [gc_col[c - 1:c, h:h + 1] for h in vheads]
    decay = [jnp.where(tril, jnp.exp(jnp.where(tril, gcol[h] - gc_row[h:h + 1, :], 0.0)), 0.0) for h in vheads]
    m = [jnp.where(strict, kqs[h // 2][:c] * bcol[h] * decay[h], 0.0) for h in vheads]
    a_qk = [(jnp.where(tril, kqs[h // 2][c:] * decay[h], 0.0) * scale).astype(BF16) for h in vheads]
    t_inv = [eye - m[h] for h in vheads]
    xp = [m[h].astype(BF16) for h in vheads]
    xp = [_dot(x, x).astype(BF16) for x in xp]
    for _ in range(4):
        prod = [_dot(jnp.concatenate([x, t.astype(BF16)], axis=0), x) for x, t in zip(xp, t_inv)]
        xp = [p[:c].astype(BF16) for p in prod]
        t_inv = [t + p[c:] for t, p in zip(t_inv, prod)]
    t_inv = [t + _dot(t.astype(BF16), x) for t, x in zip(t_inv, xp)]
    egc = [jnp.exp(gcol[h]) for h in vheads]
    kf = [ks[h].astype(F32) for h in qkheads]
    rhs = [jnp.concatenate([v_ref[:, h * GDN_DV:(h + 1) * GDN_DV].astype(F32) * bcol[h],
                            kf[h // 2] * (bcol[h] * egc[h])], axis=1).astype(BF16) for h in vheads]
    uw = [_dot(t_inv[h].astype(BF16), rhs[h]) for h in vheads]
    st = [state[h] for h in vheads]
    lhs = [jnp.concatenate([uw[h][:, GDN_DV:], qs[h // 2].astype(F32) * egc[h]], axis=0).astype(BF16)
           for h in vheads]
    ws = [_dot(lhs[h], st[h].astype(BF16)) for h in vheads]
    v_new = [(uw[h][:, :GDN_DV] - ws[h][:c]).astype(BF16) for h in vheads]
    kdec = [(kf[h // 2] * jnp.exp(glast[h] - gcol[h])).astype(BF16) for h in vheads]
    o = [ws[h][c:] * scale + _dot(a_qk[h], v_new[h]) for h in vheads]
    for h in vheads:
        state[h] = st[h] * jnp.exp(glast[h]) + _dot_tn(kdec[h], v_new[h])
    for h in vheads:
        on = o[h] * lax.rsqrt(jnp.mean(o[h] * o[h], axis=-1, keepdims=True) + RMS_EPS) * nw
        zz = z_ref[:, h * GDN_DV:(h + 1) * GDN_DV].astype(F32)
        o_ref[:, h * GDN_DV:(h + 1) * GDN_DV] = (on * (zz * _sigmoid(zz))).astype(o_ref.dtype)


def _gdn_delta(qkv, z, a, b, a_log, dt_bias, norm_w):
    s = qkv.shape[0]
    g = GDN_GROUP
    ng = GDN_V_HEADS // g
    nc = s // CHUNK
    gq = g // 2
    wq = gq * GDN_DK
    wv = g * GDN_DV
    k_off = (GDN_QK_HEADS * GDN_DK) // wq
    v_off = (2 * GDN_QK_HEADS * GDN_DK) // wv
    a_col = a.reshape(s, ng, g).transpose(1, 0, 2)
    b_col = b.reshape(s, ng, g).transpose(1, 0, 2)
    a_row = a.reshape(nc, CHUNK, ng, g).transpose(2, 0, 3, 1)
    al_c = a_log.reshape(ng, 1, g)
    dt_c = dt_bias.reshape(ng, 1, g)
    al_r = a_log.reshape(ng, g, 1)
    dt_r = dt_bias.reshape(ng, g, 1)
    return pl.pallas_call(
        _gdn_chunk_kernel,
        grid=(ng, nc),
        in_specs=[pl.BlockSpec((CHUNK, wq), lambda hg, n: (n, hg)),
                  pl.BlockSpec((CHUNK, wq), lambda hg, n: (n, k_off + hg)),
                  pl.BlockSpec((CHUNK, wv), lambda hg, n: (n, v_off + hg)),
                  pl.BlockSpec((CHUNK, wv), lambda hg, n: (n, hg)),
                  pl.BlockSpec((None, CHUNK, g), lambda hg, n: (hg, n, 0)),
                  pl.BlockSpec((None, CHUNK, g), lambda hg, n: (hg, n, 0)),
                  pl.BlockSpec((None, None, g, CHUNK), lambda hg, n: (hg, n, 0, 0)),
                  pl.BlockSpec((None, 1, g), lambda hg, n: (hg, 0, 0)),
                  pl.BlockSpec((None, 1, g), lambda hg, n: (hg, 0, 0)),
                  pl.BlockSpec((None, g, 1), lambda hg, n: (hg, 0, 0)),
                  pl.BlockSpec((None, g, 1), lambda hg, n: (hg, 0, 0)),
                  pl.BlockSpec((1, GDN_DV), lambda hg, n: (0, 0))],
        out_specs=pl.BlockSpec((CHUNK, wv), lambda hg, n: (n, hg)),
        out_shape=jax.ShapeDtypeStruct((s, GDN_V_HEADS * GDN_DV), BF16),
        scratch_shapes=[pltpu.VMEM((g, GDN_DK, GDN_DV), F32)],
        compiler_params=_cparams("parallel", "arbitrary"),
        name="gdn_delta",
    )(qkv, qkv, qkv, z, a_col, b_col, a_row, al_c, dt_c, al_r, dt_r, norm_w)


def _branch_kernel(om_ref, og_ref, wm_ref, wg_ref, gm_ref, gg_ref, bm_ref, bg_ref, o_ref):
    ym = _dot(om_ref[...], wm_ref[...])
    yg = _dot(og_ref[...], wg_ref[...])
    sm = _sigmoid(gm_ref[...].astype(F32) + bm_ref[...])
    sg = _sigmoid(gg_ref[...].astype(F32) + bg_ref[...])
    o_ref[...] = (sm * ym + sg * yg).astype(o_ref.dtype)


def _branch_merge(o_mla, o_gdn, w_o_mla, w_o_gdn, gates, b_gate):
    s = o_mla.shape[0]
    d = w_o_mla.shape[1]
    tm = min(512, s)
    tn = min(512, d)
    nb = d // tn
    return pl.pallas_call(
        _branch_kernel,
        grid=(nb, s // tm),
        in_specs=[pl.BlockSpec((tm, o_mla.shape[1]), lambda j, i: (i, 0)),
                  pl.BlockSpec((tm, o_gdn.shape[1]), lambda j, i: (i, 0)),
                  pl.BlockSpec((w_o_mla.shape[0], tn), lambda j, i: (0, j)),
                  pl.BlockSpec((w_o_gdn.shape[0], tn), lambda j, i: (0, j)),
                  pl.BlockSpec((tm, tn), lambda j, i: (i, j)),
                  pl.BlockSpec((tm, tn), lambda j, i: (i, nb + j)),
                  pl.BlockSpec((1, tn), lambda j, i: (0, j)),
                  pl.BlockSpec((1, tn), lambda j, i: (0, nb + j))],
        out_specs=pl.BlockSpec((tm, tn), lambda j, i: (i, j)),
        out_shape=jax.ShapeDtypeStruct((s, d), BF16),
        compiler_params=_cparams("parallel", "parallel"),
        name="branch_merge",
    )(o_mla, o_gdn, w_o_mla, w_o_gdn, gates, gates, b_gate, b_gate)


def _out_ln_kernel(m_ref, w_ref, x_ref, g_ref, b_ref, o_ref):
    y = DN_ALPHA * x_ref[...] + _dot(m_ref[...], w_ref[...])
    o_ref[...] = _layernorm(y, g_ref[...], b_ref[...])


def _out_ln(mixed, w_out, x, g, b):
    s, d = x.shape
    tm = min(256, s)
    return pl.pallas_call(
        _out_ln_kernel,
        grid=(s // tm,),
        in_specs=[pl.BlockSpec((tm, d), lambda i: (i, 0)),
                  pl.BlockSpec((d, d), lambda i: (0, 0)),
                  pl.BlockSpec((tm, d), lambda i: (i, 0)),
                  pl.BlockSpec((1, d), lambda i: (0, 0)),
                  pl.BlockSpec((1, d), lambda i: (0, 0))],
        out_specs=pl.BlockSpec((tm, d), lambda i: (i, 0)),
        out_shape=jax.ShapeDtypeStruct((s, d), F32),
        compiler_params=_cparams("parallel"),
        name="out_ln1",
    )(mixed, w_out, x, g, b)


SLAB = 16


def _store_rows(ref2, val, per_token, row0=0):
    n = val.shape[0]
    for s in range(val.shape[1] // LANES):
        ref2[pl.ds(row0 + s, n, stride=per_token), :] = val[:, s * LANES:(s + 1) * LANES]


def _load_rows(ref2, n, per_token, row0, k):
    return jnp.concatenate([ref2[pl.ds(row0 + s, n, stride=per_token), :] for s in range(k)], axis=1)


def _mem_kernel(x_ref, wq_ref, kv_ref, wo_ref, g_ref, b_ref, wr_ref, br_ref, x2_ref, x2r_ref, lg_ref):
    x1 = x_ref[...]
    hd = MEM_HEAD_DIM
    nh = MEM_HEADS
    q = (_dot(x1.astype(BF16), wq_ref[...]) * float(hd ** -0.5)).astype(BF16)
    outs = []
    for h in range(nh):
        kh = kv_ref[:, h * hd:(h + 1) * hd]
        vh = kv_ref[:, (nh + h) * hd:(nh + h + 1) * hd]
        s = _dot_nt(q[:, h * hd:(h + 1) * hd], kh)
        p = jnp.exp(s - jnp.max(s, axis=-1, keepdims=True))
        o = _dot(p.astype(BF16), vh) / jnp.sum(p, axis=-1, keepdims=True)
        outs.append(o.astype(BF16))
    o = jnp.concatenate(outs, axis=1)
    y = DN_ALPHA * x1 + _dot(o, wo_ref[...])
    x2 = _layernorm(y, g_ref[...], b_ref[...])
    x2_ref[...] = x2
    _store_rows(x2r_ref, x2, SLAB)
    x_hi = x2.astype(BF16)
    x_lo = (x2 - x_hi.astype(F32)).astype(BF16)
    w = wr_ref[...]
    w_hi = w.astype(BF16)
    w_lo = (w - w_hi.astype(F32)).astype(BF16)
    lg_ref[...] = (_dot(x_hi, w_hi) + _dot(x_lo, w_hi)) + _dot(x_hi, w_lo) + br_ref[...]


def _mem_attn_ln(x1, w_mq, kvm, w_mo, g, b, w_r, b_r):
    s, d = x1.shape
    tm = min(256, s)
    return pl.pallas_call(
        _mem_kernel,
        grid=(s // tm,),
        in_specs=[pl.BlockSpec((tm, d), lambda i: (i, 0)),
                  pl.BlockSpec(w_mq.shape, lambda i: (0, 0)),
                  pl.BlockSpec(kvm.shape, lambda i: (0, 0)),
                  pl.BlockSpec(w_mo.shape, lambda i: (0, 0)),
                  pl.BlockSpec((1, d), lambda i: (0, 0)),
                  pl.BlockSpec((1, d), lambda i: (0, 0)),
                  pl.BlockSpec(w_r.shape, lambda i: (0, 0)),
                  pl.BlockSpec((1, LANES), lambda i: (0, 0))],
        out_specs=[pl.BlockSpec((tm, d), lambda i: (i, 0)),
                   pl.BlockSpec((tm * SLAB, LANES), lambda i: (i, 0)),
                   pl.BlockSpec((tm, LANES), lambda i: (i, 0))],
        out_shape=[jax.ShapeDtypeStruct((s, d), F32),
                   jax.ShapeDtypeStruct((s * SLAB, LANES), F32),
                   jax.ShapeDtypeStruct((s, LANES), F32)],
        compiler_params=_cparams("parallel"),
        name="mem_attn_ln2",
    )(x1, w_mq, kvm, w_mo, g, b, w_r, b_r)


SEL_E1, SEL_E2, SEL_R1, SEL_R2, SEL_G1, SEL_G2 = range(6)
GRP_LANE0 = N_EXPERTS


def _route_kernel(lg_ref, sel_ref, cnt_ref, carry):
    i = pl.program_id(0)

    @pl.when(i == 0)
    def _():
        carry[...] = jnp.zeros(carry.shape, F32)

    lg = lg_ref[...]
    tm = lg.shape[0]
    lane = lax.broadcasted_iota(jnp.int32, lg.shape, 1)
    big = jnp.int32(4 * LANES)
    neg = jnp.float32(-jnp.inf)

    def first_max(vals):
        mx = jnp.max(vals, axis=-1, keepdims=True)
        idx = jnp.min(jnp.where(vals == mx, lane, big), axis=-1, keepdims=True)
        return mx, idx

    is_grp = (lane >= GRP_LANE0) & (lane < GRP_LANE0 + N_GROUPS)
    gl = jnp.where(is_grp, lg, neg)
    gmax, gidx = first_max(gl)
    p_top = 1.0 / jnp.sum(jnp.where(is_grp, jnp.exp(gl - gmax), 0.0), axis=-1, keepdims=True)
    lo = (gidx - GRP_LANE0) * EXPERTS_PER_GROUP
    in_grp = (lane >= lo) & (lane < lo + EXPERTS_PER_GROUP)
    el = jnp.where(in_grp, lg, neg)
    m1, i1 = first_max(el)
    m2, i2 = first_max(jnp.where(lane == i1, neg, el))
    r = jnp.exp(m2 - m1)
    g1 = p_top / (1.0 + r)
    g2 = p_top * r / (1.0 + r)

    hot1 = lane == i1
    hot2 = lane == i2
    onehot = jnp.where(hot1, 1.0, 0.0) + jnp.where(hot2, 1.0, 0.0)
    ri = lax.broadcasted_iota(jnp.int32, (tm, tm), 0)
    ci = lax.broadcasted_iota(jnp.int32, (tm, tm), 1)
    before = jnp.where(ci < ri, 1.0, 0.0).astype(BF16)
    rank = _dot(before, onehot.astype(BF16)) + carry[...]
    r1 = jnp.sum(jnp.where(hot1, rank, 0.0), axis=-1, keepdims=True)
    r2 = jnp.sum(jnp.where(hot2, rank, 0.0), axis=-1, keepdims=True)
    carry[...] = carry[...] + jnp.sum(onehot, axis=0, keepdims=True)
    cnt_ref[...] = carry[...]

    out = jnp.zeros(lg.shape, F32)
    for ln, val in ((SEL_E1, i1.astype(F32)), (SEL_E2, i2.astype(F32)), (SEL_R1, r1), (SEL_R2, r2),
                    (SEL_G1, g1), (SEL_G2, g2)):
        out = jnp.where(lane == ln, val, out)
    sel_ref[...] = out


def _route(logits):
    t = logits.shape[0]
    tm = min(512, t)
    return pl.pallas_call(
        _route_kernel,
        grid=(t // tm,),
        in_specs=[pl.BlockSpec((tm, LANES), lambda i: (i, 0))],
        out_specs=[pl.BlockSpec((tm, LANES), lambda i: (i, 0)),
                   pl.BlockSpec((1, LANES), lambda i: (0, 0))],
        out_shape=[jax.ShapeDtypeStruct((t, LANES), F32),
                   jax.ShapeDtypeStruct((1, LANES), F32)],
        scratch_shapes=[pltpu.VMEM((1, LANES), F32)],
        compiler_params=_cparams("arbitrary"),
        name="moe_route",
    )(logits)


def _dispatch_kernel(cnt_ref, e1_ref, e2_ref, r1_ref, r2_ref, rowa_ref, blk_ref, rows_ref, d1_ref, d2_ref, pstart,
                     used_ref, *, t, n_blocks):
    def seg(e, start):
        pstart[e] = start
        c = cnt_ref[e]
        nb = (c + MOE_ROWS - 1) // MOE_ROWS
        end = start + nb * MOE_ROWS

        def mark(p, carry):
            rowa_ref[p] = -1
            return carry
        lax.fori_loop(start + c, end, mark, 0)

        def blk(bi, carry):
            blk_ref[bi] = e
            rows_ref[bi] = jnp.minimum(start + c - bi * MOE_ROWS, MOE_ROWS)
            return carry
        lax.fori_loop(start // MOE_ROWS, end // MOE_ROWS, blk, 0)
        return end

    total = lax.fori_loop(0, N_EXPERTS, seg, 0)
    used = total // MOE_ROWS
    used_ref[0] = used
    last = blk_ref[jnp.maximum(used - 1, 0)]

    def tail(bi, carry):
        blk_ref[bi] = last
        rows_ref[bi] = 0

        def unused(p, c2):
            rowa_ref[bi * MOE_ROWS + p] = -1
            return c2
        lax.fori_loop(0, MOE_ROWS, unused, 0, unroll=16)
        return carry
    lax.fori_loop(used, n_blocks, tail, 0)

    def place(tok, carry):
        p1 = pstart[e1_ref[tok]] + r1_ref[tok]
        p2 = pstart[e2_ref[tok]] + r2_ref[tok]
        rowa_ref[p1] = 2 * tok
        rowa_ref[p2] = 2 * tok + 1
        d1_ref[tok] = p1
        d2_ref[tok] = p2
        return carry
    lax.fori_loop(0, t, place, 0, unroll=8)


def _dispatch(cnt, e1, e2, r1, r2, n_blocks):
    t = e1.shape[0]
    smem = pl.BlockSpec(memory_space=pltpu.SMEM)
    i32 = lambda n: jax.ShapeDtypeStruct((n,), jnp.int32)
    return pl.pallas_call(
        functools.partial(_dispatch_kernel, t=t, n_blocks=n_blocks),
        in_specs=[smem] * 5,
        out_specs=[smem] * 7,
        out_shape=[i32(n_blocks * MOE_ROWS), i32(n_blocks), i32(n_blocks), i32(t), i32(t), i32(N_EXPERTS), i32(1)],
        name="moe_dispatch",
    )(cnt, e1, e2, r1, r2)


def _scatter_in_kernel(d1_ref, d2_ref, cnt_ref, pstart_ref, used_ref, x_ref, xs_hbm, zbuf, sem, zsem, *, tm, n_blocks):
    i = pl.program_id(0)
    blk_rows = MOE_ROWS * SLAB

    def slab(j):
        return pl.ds(pl.multiple_of(j * SLAB, SLAB), SLAB)

    def block(bi):
        return pl.ds(pl.multiple_of(bi * blk_rows, blk_rows), blk_rows)

    def pad_range(e):
        c = cnt_ref[e]
        lo = pstart_ref[e] + c
        return lo, pstart_ref[e] + (c + MOE_ROWS - 1) // MOE_ROWS * MOE_ROWS

    @pl.when(i == 0)
    def _():
        zbuf[...] = jnp.zeros(zbuf.shape, F32)

        def zero_pads(e, carry):
            lo, hi = pad_range(e)

            def z(p, c2):
                pltpu.make_async_copy(zbuf.at[slab(0)], xs_hbm.at[slab(p)], zsem).start()
                return c2
            lax.fori_loop(lo, hi, z, 0)
            return carry
        lax.fori_loop(0, N_EXPERTS, zero_pads, 0)

        def zero_block(bi, carry):
            pltpu.make_async_copy(zbuf, xs_hbm.at[block(bi)], zsem).start()
            return carry
        lax.fori_loop(used_ref[0], n_blocks, zero_block, 0)

    def scatter(r, carry):
        tok = i * tm + r
        pltpu.make_async_copy(x_ref.at[slab(r)], xs_hbm.at[slab(d1_ref[tok])], sem).start()
        pltpu.make_async_copy(x_ref.at[slab(r)], xs_hbm.at[slab(d2_ref[tok])], sem).start()
        return carry
    lax.fori_loop(0, tm, scatter, 0, unroll=8)

    for _ in range(2):
        pltpu.make_async_copy(x_ref, xs_hbm.at[pl.ds(0, tm * SLAB)], sem).wait()

    @pl.when(i == pl.num_programs(0) - 1)
    def _():
        def wait_pads(e, carry):
            lo, hi = pad_range(e)
            n = hi - lo

            @pl.when(n > 0)
            def _():
                rows = pl.ds(0, pl.multiple_of(n * SLAB, SLAB))
                pltpu.make_async_copy(zbuf.at[rows], xs_hbm.at[rows], zsem).wait()
            return carry
        lax.fori_loop(0, N_EXPERTS, wait_pads, 0)

        def wait_block(bi, carry):
            pltpu.make_async_copy(zbuf, xs_hbm.at[block(0)], zsem).wait()
            return carry
        lax.fori_loop(used_ref[0], n_blocks, wait_block, 0)


def _scatter_in(x2r, d1, d2, cnt, pstart, used, n_blocks):
    t = x2r.shape[0] // SLAB
    tm = min(MOE_ROWS, t)
    grid_spec = pltpu.PrefetchScalarGridSpec(
        num_scalar_prefetch=5,
        grid=(t // tm,),
        in_specs=[pl.BlockSpec((tm * SLAB, LANES), lambda i, *_: (i, 0))],
        out_specs=pl.BlockSpec(memory_space=pl.ANY),
        scratch_shapes=[pltpu.VMEM((MOE_ROWS * SLAB, LANES), F32),
                        pltpu.SemaphoreType.DMA(()), pltpu.SemaphoreType.DMA(())],
    )
    return pl.pallas_call(
        functools.partial(_scatter_in_kernel, tm=tm, n_blocks=n_blocks),
        grid_spec=grid_spec,
        out_shape=jax.ShapeDtypeStruct((n_blocks * MOE_ROWS * SLAB, LANES), F32),
        compiler_params=_cparams("arbitrary"),
        name="moe_scatter_in",
    )(d1, d2, cnt, pstart, used, x2r)


def _expert_kernel(blk_ref, rowa_ref, rows_ref, used_ref, x_ref, wg_ref, wu_ref, wd_ref, out_hbm, ybuf, ssem):
    b = pl.program_id(0)
    n_blocks = pl.num_programs(0)
    slot = b % 2

    def slab(i):
        return pl.ds(pl.multiple_of(i * SLAB, SLAB), SLAB)

    def scatter_copy(blk_slot, r, dst):
        return pltpu.make_async_copy(ybuf.at[blk_slot, slab(r)], out_hbm.at[slab(dst)], ssem.at[blk_slot])

    def wait_scatters(blk, blk_slot):
        n = rows_ref[blk]

        @pl.when(n > 0)
        def _():
            rows = pl.ds(0, pl.multiple_of(n * SLAB, SLAB))
            pltpu.make_async_copy(ybuf.at[blk_slot, rows], out_hbm.at[rows], ssem.at[blk_slot]).wait()

    @pl.when(b >= 2)
    def _():
        wait_scatters(jnp.maximum(b - 2, 0), slot)

    n_rows = rows_ref[b]

    @pl.when(n_rows > 0)
    def _():
        xb = _load_rows(x_ref, MOE_ROWS, SLAB, 0, SLAB).astype(BF16)
        hg = _dot(xb, wg_ref[...].astype(BF16))
        hu = _dot(xb, wu_ref[...].astype(BF16))
        hb = (hg * _sigmoid(hg) * hu).astype(BF16)
        _store_rows(ybuf.at[slot], _dot(hb, wd_ref[...].astype(BF16)), SLAB)

        def issue_scatter(r, carry):
            scatter_copy(slot, r, rowa_ref[b * MOE_ROWS + r]).start()
            return carry
        lax.fori_loop(0, n_rows, issue_scatter, 0)

    @pl.when(b == n_blocks - 1)
    def _():
        @pl.when(b >= 1)
        def _():
            wait_scatters(jnp.maximum(b - 1, 0), 1 - slot)
        wait_scatters(b, slot)


def _experts(xs, t, row_a, blk_exp, blk_rows, used, w_gate_e, w_up_e, w_down_e):
    d = SLAB * LANES
    n_blocks = blk_exp.shape[0]
    de = w_gate_e.shape[2]
    weights = lambda b, blk, rowa, rows, nu: (blk[b], 0, 0)
    grid_spec = pltpu.PrefetchScalarGridSpec(
        num_scalar_prefetch=4,
        grid=(n_blocks,),
        in_specs=[pl.BlockSpec((MOE_ROWS * SLAB, LANES), lambda b, blk, rowa, rows, nu: (jnp.minimum(b, nu[0] - 1), 0)),
                  pl.BlockSpec((None, d, de), weights),
                  pl.BlockSpec((None, d, de), weights),
                  pl.BlockSpec((None, de, d), weights)],
        out_specs=pl.BlockSpec(memory_space=pl.ANY),
        scratch_shapes=[pltpu.VMEM((2, MOE_ROWS * SLAB, LANES), F32), pltpu.SemaphoreType.DMA((2,))],
    )
    return pl.pallas_call(
        _expert_kernel,
        grid_spec=grid_spec,
        out_shape=jax.ShapeDtypeStruct((2 * t * SLAB, LANES), F32),
        compiler_params=_cparams("arbitrary"),
        name="moe_experts",
    )(blk_exp, row_a, blk_rows, used, xs, w_gate_e, w_up_e, w_down_e)


def _combine_kernel(x_ref, y_ref, sel_ref, g_ref, b_ref, o_ref):
    tm = x_ref.shape[0]
    sel = sel_ref[...]
    g1 = sel[:, SEL_G1:SEL_G1 + 1]
    g2 = sel[:, SEL_G2:SEL_G2 + 1]
    y1 = _load_rows(y_ref, tm, 2 * SLAB, 0, SLAB)
    y2 = _load_rows(y_ref, tm, 2 * SLAB, SLAB, SLAB)
    y = DN_ALPHA * x_ref[...] + (g1 * y1 + g2 * y2)
    o_ref[...] = _layernorm(y, g_ref[...], b_ref[...])


def _combine_ln(x2, ys, sel, g, b):
    t, d = x2.shape
    tm = min(256, t)
    return pl.pallas_call(
        _combine_kernel,
        grid=(t // tm,),
        in_specs=[pl.BlockSpec((tm, d), lambda i: (i, 0)),
                  pl.BlockSpec((tm * 2 * SLAB, LANES), lambda i: (i, 0)),
                  pl.BlockSpec((tm, LANES), lambda i: (i, 0)),
                  pl.BlockSpec((1, d), lambda i: (0, 0)),
                  pl.BlockSpec((1, d), lambda i: (0, 0))],
        out_specs=pl.BlockSpec((tm, d), lambda i: (i, 0)),
        out_shape=jax.ShapeDtypeStruct((t, d), F32),
        compiler_params=_cparams("parallel"),
        name="combine_ln3",
    )(x2, ys, sel, g, b)


def _mixer(xb, positions, w_in, b_gate, mla_q_norm, w_uq, mla_kv_norm, w_ukv, w_o_mla,
           gdn_conv, gdn_a_log, gdn_dt_bias, gdn_norm, w_o_gdn):
    s, d = xb.shape
    nqkv = 2 * GDN_QK_HEADS * GDN_DK + GDN_V_HEADS * GDN_DV
    nz = GDN_V_HEADS * GDN_DV
    o0 = 0
    o1 = o0 + MLA_Q_LORA
    o2 = o1 + MLA_KV_LORA + MLA_ROPE
    o3 = o2 + nqkv
    o4 = o3 + nz
    o5 = o4 + GDN_V_HEADS
    o6 = o5 + GDN_V_HEADS

    def rot_cols(w):
        half = MLA_ROPE // 2
        return jnp.concatenate([-w[..., half:], w[..., :half]], axis=-1)

    w_q = w_in[:, o0:o1].astype(BF16)
    w_kpe = w_in[:, o1 + MLA_KV_LORA:o2]
    w_kv = jnp.concatenate([w_in[:, o1:o1 + MLA_KV_LORA], w_kpe, rot_cols(w_kpe)], axis=1).astype(BF16)
    w_ba = jnp.concatenate([w_in[:, o4:o6], jnp.zeros((d, LANES - 2 * GDN_V_HEADS), F32)], axis=1).astype(BF16)
    qd = _matmul(xb, w_q, BF16, 512, MLA_Q_LORA)
    kvd = _matmul(xb, w_kv, BF16, 512, w_kv.shape[1])
    qkv_c = _qkv_conv_proj(xb, w_in[:, o2:o3].astype(BF16), gdn_conv)
    hz = _matmul(xb, w_in[:, o3:o4].astype(BF16), BF16, 512, 1024)
    hba = _matmul(xb, w_ba, F32, 512, LANES)
    hgate = _matmul(xb, w_in[:, o6:].astype(BF16), BF16, 512, 1024)

    cs = _rope_table(positions)
    hq = MLA_NOPE + MLA_ROPE
    wq3 = w_uq.reshape(MLA_Q_LORA, MLA_HEADS, hq)
    pe = wq3[..., MLA_NOPE:]
    wq = jnp.concatenate([pe, rot_cols(pe), wq3[..., :MLA_NOPE]], axis=-1)
    wq = wq.reshape(MLA_Q_LORA, MLA_HEADS * MLA_QK_PAD).astype(BF16)
    wkv3 = w_ukv.reshape(MLA_KV_LORA, MLA_HEADS, MLA_NOPE + MLA_V)
    wk = wkv3[..., :MLA_NOPE].reshape(MLA_KV_LORA, MLA_HEADS * MLA_NOPE).astype(BF16)
    wv = wkv3[..., MLA_NOPE:].reshape(MLA_KV_LORA, MLA_HEADS * MLA_V).astype(BF16)
    q = _mla_q_proj(qd, mla_q_norm.reshape(1, -1), wq, cs)
    k, v = _mla_kv_proj(kvd, mla_kv_norm.reshape(1, -1), wk, wv, cs)
    o_mla = _mla_attention(q, k, v)

    o_gdn = _gdn_delta(qkv_c, hz, hba[:, GDN_V_HEADS:2 * GDN_V_HEADS], hba[:, :GDN_V_HEADS],
                       gdn_a_log, gdn_dt_bias, gdn_norm.reshape(1, -1))

    return _branch_merge(o_mla, o_gdn, w_o_mla.astype(BF16), w_o_gdn.astype(BF16), hgate, b_gate.reshape(1, -1))


def _moe(x2, x2r, logits, w_gate_e, w_up_e, w_down_e, ln_g, ln_b):
    t, d = x2.shape
    n_blocks = (2 * t) // MOE_ROWS + N_EXPERTS
    sel, cnt = _route(logits)
    as_i32 = lambda col: sel[:, col].astype(jnp.int32)
    cnt_i = cnt[0, :N_EXPERTS].astype(jnp.int32)
    row_a, blk_exp, blk_rows, d1, d2, pstart, used = _dispatch(cnt_i, as_i32(SEL_E1), as_i32(SEL_E2),
                                                               as_i32(SEL_R1), as_i32(SEL_R2), n_blocks)
    xs = _scatter_in(x2r, d1, d2, cnt_i, pstart, used, n_blocks)
    ys = _experts(xs, t, row_a, blk_exp, blk_rows, used, w_gate_e, w_up_e, w_down_e)
    return _combine_ln(x2, ys, sel, ln_g, ln_b)


def _layer(x, mem, positions, w_in, b_gate, mla_q_norm, w_uq, mla_kv_norm, w_ukv, w_o_mla,
           gdn_conv, gdn_a_log, gdn_dt_bias, gdn_norm, w_o_gdn, w_out, ln1_g, ln1_b,
           w_mq, w_mkv, w_mo, ln2_g, ln2_b, w_route_grp, b_route_grp, w_route_exp, b_route_exp,
           w_gate_e, w_up_e, w_down_e, ln3_g, ln3_b):
    d = x.shape[1]
    row = lambda p: p.reshape(1, -1)
    mixed = _mixer(x.astype(BF16), positions, w_in, b_gate, mla_q_norm, w_uq, mla_kv_norm, w_ukv, w_o_mla,
                   gdn_conv, gdn_a_log, gdn_dt_bias, gdn_norm, w_o_gdn)
    x1 = _out_ln(mixed, w_out.astype(BF16), x, row(ln1_g), row(ln1_b))

    kvm = _matmul(mem.astype(BF16), w_mkv.astype(BF16), BF16, 256, 512)
    pad = LANES - N_EXPERTS - N_GROUPS
    w_r = jnp.concatenate([w_route_exp, w_route_grp, jnp.zeros((d, pad), F32)], axis=1)
    b_r = jnp.concatenate([b_route_exp, b_route_grp, jnp.zeros((pad,), F32)]).reshape(1, LANES)
    x2, x2r, logits = _mem_attn_ln(x1, w_mq.astype(BF16), kvm, w_mo.astype(BF16), row(ln2_g), row(ln2_b), w_r, b_r)

    return _moe(x2, x2r, logits, w_gate_e, w_up_e, w_down_e, row(ln3_g), row(ln3_b))


def kernel(x, mem, positions, w_in, b_gate, mla_q_norm, w_uq, mla_kv_norm, w_ukv, w_o_mla, gdn_conv, gdn_a_log,
           gdn_dt_bias, gdn_norm, w_o_gdn, w_out, ln1_g, ln1_b, w_mq, w_mkv, w_mo, ln2_g, ln2_b, w_route_grp,
           b_route_grp, w_route_exp, b_route_exp, w_gate_e, w_up_e, w_down_e, ln3_g, ln3_b):
    outs = []
    for bi in range(x.shape[0]):
        h = x[bi]
        for l in range(w_in.shape[0]):
            h = _layer(h, mem[bi], positions[bi], w_in[l], b_gate[l], mla_q_norm[l], w_uq[l], mla_kv_norm[l],
                       w_ukv[l], w_o_mla[l], gdn_conv[l], gdn_a_log[l], gdn_dt_bias[l], gdn_norm[l], w_o_gdn[l],
                       w_out[l], ln1_g[l], ln1_b[l], w_mq[l], w_mkv[l], w_mo[l], ln2_g[l], ln2_b[l],
                       w_route_grp[l], b_route_grp[l], w_route_exp[l], b_route_exp[l], w_gate_e[l], w_up_e[l],
                       w_down_e[l], ln3_g[l], ln3_b[l])
        outs.append(h)
    return jnp.stack(outs, axis=0)
```

```python
import functools

import numpy as np
import jax
import jax.numpy as jnp
from jax import lax
from jax.experimental import pallas as pl
from jax.experimental.pallas import tpu as pltpu

F32 = jnp.float32
BF16 = jnp.bfloat16
HIGHEST = lax.Precision.HIGHEST

LANES = 128
VMEM_LIMIT = 56 * 1024 * 1024

CHUNK = 64
MLA_HEADS = 16
MLA_Q_LORA = 768
MLA_KV_LORA = 512
MLA_NOPE = 128
MLA_ROPE = 64
MLA_V = 128
MLA_QK_PAD = 256
ROPE_THETA = 10000.0
GDN_QK_HEADS = 16
GDN_V_HEADS = 32
GDN_DK = 128
GDN_DV = 128
GDN_CONV = 4
GDN_GROUP = 32
MEM_HEADS = 4
MEM_HEAD_DIM = 128
N_GROUPS = 8
EXPERTS_PER_GROUP = 8
N_EXPERTS = 64
D_EXPERT = 512
MOE_ROWS = 256
RMS_EPS = 1e-6
LN_EPS = 1e-5
DN_ALPHA = 2.0 ** 0.25


def _cparams(*sem):
    return pltpu.CompilerParams(dimension_semantics=sem, vmem_limit_bytes=VMEM_LIMIT)


def _dot(a, b, **kw):
    return jnp.dot(a, b, preferred_element_type=F32, **kw)


def _dot_nt(a, b):
    return lax.dot_general(a, b, (((1,), (1,)), ((), ())), preferred_element_type=F32)


def _dot_tn(a, b):
    return lax.dot_general(a, b, (((0,), (0,)), ((), ())), preferred_element_type=F32)


def _sigmoid(x):
    return 1.0 / (1.0 + jnp.exp(-x))


def _layernorm(y, g, b):
    mu = jnp.mean(y, axis=-1, keepdims=True)
    d = y - mu
    var = jnp.mean(d * d, axis=-1, keepdims=True)
    return d * lax.rsqrt(var + LN_EPS) * g + b


def _mm_kernel(x_ref, w_ref, o_ref):
    o_ref[...] = _dot(x_ref[...], w_ref[...]).astype(o_ref.dtype)


def _matmul(x, w, out_dtype, tm, tn):
    m, k = x.shape
    n = w.shape[1]
    tm, tn = min(tm, m), min(tn, n)
    return pl.pallas_call(
        _mm_kernel,
        grid=(n // tn, m // tm),
        in_specs=[pl.BlockSpec((tm, k), lambda j, i: (i, 0)),
                  pl.BlockSpec((k, tn), lambda j, i: (0, j))],
        out_specs=pl.BlockSpec((tm, tn), lambda j, i: (i, j)),
        out_shape=jax.ShapeDtypeStruct((m, n), out_dtype),
        compiler_params=_cparams("parallel", "parallel"),
        name="matmul",
    )(x, w)


def _rope_table_kernel(pos_ref, inv_ref, o_ref):
    ang = pos_ref[...].astype(F32) * inv_ref[...]
    lane = lax.broadcasted_iota(jnp.int32, ang.shape, 1)
    o_ref[...] = jnp.where(lane < MLA_ROPE, jnp.cos(ang), jnp.sin(ang))


def _rope_table(positions):
    s = positions.shape[0]
    inv = 1.0 / (ROPE_THETA ** (np.arange(0, MLA_ROPE, 2, dtype=np.float32) / MLA_ROPE))
    inv4 = jnp.asarray(np.tile(inv.astype(np.float32), 4)[None, :])
    tm = min(512, s)
    return pl.pallas_call(
        _rope_table_kernel,
        grid=(s // tm,),
        in_specs=[pl.BlockSpec((tm, 1), lambda i: (i, 0)),
                  pl.BlockSpec((1, LANES), lambda i: (0, 0))],
        out_specs=pl.BlockSpec((tm, LANES), lambda i: (i, 0)),
        out_shape=jax.ShapeDtypeStruct((s, LANES), F32),
        compiler_params=_cparams("parallel"),
        name="rope_table",
    )(positions.reshape(s, 1), inv4)


def _rope_pair(t, cs):
    a = t * cs
    return a + pltpu.roll(a, MLA_ROPE, axis=1)


def _mla_q_kernel(qd_ref, qn_ref, w_ref, cs_ref, o_ref, *, scale):
    x = qd_ref[...].astype(F32)
    cq = x * lax.rsqrt(jnp.mean(x * x, axis=-1, keepdims=True) + RMS_EPS) * qn_ref[...]
    cqb = cq.astype(BF16)
    cs = cs_ref[...]
    for h in range(MLA_HEADS):
        lo = h * MLA_QK_PAD
        p = _dot(cqb, w_ref[:, lo:lo + MLA_QK_PAD])
        o_ref[:, lo:lo + LANES] = (_rope_pair(p[:, :LANES], cs) * scale).astype(o_ref.dtype)
        o_ref[:, lo + LANES:lo + MLA_QK_PAD] = (p[:, LANES:] * scale).astype(o_ref.dtype)


def _mla_q_proj(qd, q_norm, wq, cs):
    s = qd.shape[0]
    tm = min(512, s)
    n = MLA_HEADS * MLA_QK_PAD
    scale = float((MLA_NOPE + MLA_ROPE) ** -0.5 * np.log2(np.e))
    return pl.pallas_call(
        functools.partial(_mla_q_kernel, scale=scale),
        grid=(s // tm,),
        in_specs=[pl.BlockSpec((tm, MLA_Q_LORA), lambda i: (i, 0)),
                  pl.BlockSpec((1, MLA_Q_LORA), lambda i: (0, 0)),
                  pl.BlockSpec((MLA_Q_LORA, n), lambda i: (0, 0)),
                  pl.BlockSpec((tm, LANES), lambda i: (i, 0))],
        out_specs=pl.BlockSpec((tm, n), lambda i: (i, 0)),
        out_shape=jax.ShapeDtypeStruct((s, n), BF16),
        compiler_params=_cparams("parallel"),
        name="mla_q_proj",
    )(qd, q_norm, wq, cs)


def _mla_kv_kernel(kv_ref, kn_ref, wk_ref, wv_ref, cs_ref, k_ref, v_ref):
    x = kv_ref[:, :MLA_KV_LORA].astype(F32)
    ckv = x * lax.rsqrt(jnp.mean(x * x, axis=-1, keepdims=True) + RMS_EPS) * kn_ref[...]
    cb = ckv.astype(BF16)
    pe = _rope_pair(kv_ref[:, MLA_KV_LORA:].astype(F32), cs_ref[...])
    lane = lax.broadcasted_iota(jnp.int32, pe.shape, 1)
    pe = jnp.where(lane < MLA_ROPE, pe, 0.0).astype(k_ref.dtype)
    kn = _dot(cb, wk_ref[...]).astype(k_ref.dtype)
    for h in range(MLA_HEADS):
        lo = h * MLA_QK_PAD
        k_ref[:, lo:lo + LANES] = pe
        k_ref[:, lo + LANES:lo + MLA_QK_PAD] = kn[:, h * MLA_NOPE:(h + 1) * MLA_NOPE]
    v_ref[...] = _dot(cb, wv_ref[...]).astype(v_ref.dtype)


def _mla_kv_proj(kvd, kv_norm, wk, wv, cs):
    s, w = kvd.shape
    tm = min(512, s)
    nk = MLA_HEADS * MLA_QK_PAD
    nv = MLA_HEADS * MLA_V
    return pl.pallas_call(
        _mla_kv_kernel,
        grid=(s // tm,),
        in_specs=[pl.BlockSpec((tm, w), lambda i: (i, 0)),
                  pl.BlockSpec((1, MLA_KV_LORA), lambda i: (0, 0)),
                  pl.BlockSpec(wk.shape, lambda i: (0, 0)),
                  pl.BlockSpec(wv.shape, lambda i: (0, 0)),
                  pl.BlockSpec((tm, LANES), lambda i: (i, 0))],
        out_specs=[pl.BlockSpec((tm, nk), lambda i: (i, 0)),
                   pl.BlockSpec((tm, nv), lambda i: (i, 0))],
        out_shape=[jax.ShapeDtypeStruct((s, nk), BF16),
                   jax.ShapeDtypeStruct((s, nv), BF16)],
        compiler_params=_cparams("parallel"),
        name="mla_kv_proj",
    )(kvd, kv_norm, wk, wv, cs)


FLASH_HEADS = 2
FLASH_TILE = 1024


def _lane_repeat(x, n):
    return jnp.concatenate([x] * n, axis=1)


def _flash_kernel(qi_ref, kj_ref, q_ref, k_ref, v_ref, o_ref, m_sc, acc_sc):
    t = pl.program_id(1)
    i = qi_ref[t]
    j = kj_ref[t]
    hps = m_sc.shape[0]
    tk = k_ref.shape[0]
    heads = range(hps)

    @pl.when(j == 0)
    def _():
        m_sc[...] = jnp.full(m_sc.shape, -1e30, F32)
        acc_sc[...] = jnp.zeros(acc_sc.shape, F32)

    def update(r0, nr, nk, masked):
        rows = slice(r0, r0 + nr)
        ss = [_dot_nt(q_ref[rows, h * MLA_QK_PAD:(h + 1) * MLA_QK_PAD], k_ref[0:nk, h * MLA_QK_PAD:(h + 1) * MLA_QK_PAD])
              for h in heads]
        if masked:
            r = (lax.broadcasted_iota(jnp.int32, ss[0].shape, 0) + r0) // CHUNK
            c = lax.broadcasted_iota(jnp.int32, ss[0].shape, 1) // CHUNK
            keep = c <= r
            ss = [jnp.where(keep, s, -1e30) for s in ss]
        m_prev = [m_sc[h, rows] for h in heads]
        m_new = [jnp.maximum(mp, jnp.max(s, axis=-1, keepdims=True)) for mp, s in zip(m_prev, ss)]
        alpha = [jnp.exp2(mp - mn) for mp, mn in zip(m_prev, m_new)]
        ps = [jnp.exp2(s - _lane_repeat(mn, nk // LANES)).astype(BF16) for s, mn in zip(ss, m_new)]
        ones = jnp.ones((nk, MLA_V), BF16)
        pv = [_dot(p, jnp.concatenate([v_ref[0:nk, h * MLA_V:(h + 1) * MLA_V], ones], axis=1))
              for h, p in zip(heads, ps)]
        for h in heads:
            acc_sc[h, rows] = _lane_repeat(alpha[h], 2) * acc_sc[h, rows] + pv[h]
            m_sc[h, rows] = m_new[h]

    @pl.when(j < i)
    def _():
        update(0, tk, tk, False)

    @pl.when(j == i)
    def _():
        half = tk // 2
        if half % CHUNK == 0 and half % LANES == 0:
            update(0, half, half, True)
            update(half, half, tk, True)
        else:
            update(0, tk, tk, True)
        for h in heads:
            acc = acc_sc[h]
            o_ref[:, h * MLA_V:(h + 1) * MLA_V] = (acc[:, :MLA_V] / acc[:, MLA_V:]).astype(o_ref.dtype)


def _mla_attention(q, k, v):
    s = q.shape[0]
    t = min(FLASH_TILE, s)
    n = s // t
    hps = FLASH_HEADS
    qi = np.array([i for i in range(n) for _ in range(i + 1)], np.int32)
    kj = np.array([j for i in range(n) for j in range(i + 1)], np.int32)
    grid_spec = pltpu.PrefetchScalarGridSpec(
        num_scalar_prefetch=2,
        grid=(MLA_HEADS // hps, qi.shape[0]),
        in_specs=[pl.BlockSpec((t, hps * MLA_QK_PAD), lambda h, st, qi_r, kj_r: (qi_r[st], h)),
                  pl.BlockSpec((t, hps * MLA_QK_PAD), lambda h, st, qi_r, kj_r: (kj_r[st], h)),
                  pl.BlockSpec((t, hps * MLA_V), lambda h, st, qi_r, kj_r: (kj_r[st], h))],
        out_specs=pl.BlockSpec((t, hps * MLA_V), lambda h, st, qi_r, kj_r: (qi_r[st], h)),
        scratch_shapes=[pltpu.VMEM((hps, t, LANES), F32), pltpu.VMEM((hps, t, 2 * MLA_V), F32)],
    )
    return pl.pallas_call(
        _flash_kernel,
        grid_spec=grid_spec,
        out_shape=jax.ShapeDtypeStruct((s, MLA_HEADS * MLA_V), BF16),
        compiler_params=_cparams("parallel", "arbitrary"),
        name="mla_flash",
    )(jnp.asarray(qi), jnp.asarray(kj), q, k, v)


CONV_HALO = 8


def _qkv_conv_kernel(x_ref, w_ref, cw_ref, o_ref, buf, *, tm, tn, n_qk_blocks):
    c = pl.program_id(0)
    i = pl.program_id(1)
    halo = CONV_HALO

    @pl.when(i == 0)
    def _():
        buf[0:halo, :] = jnp.zeros((halo, tn), F32)

    h = _dot(x_ref[...], w_ref[...])
    buf[halo:halo + tm, :] = h
    ext = buf[...]
    y = cw_ref[GDN_CONV - 1:GDN_CONV, :] * h
    for k in range(1, GDN_CONV):
        y = y + cw_ref[GDN_CONV - 1 - k:GDN_CONV - k, :] * pltpu.roll(ext, k, axis=0)[halo:halo + tm, :]
    buf[0:halo, :] = h[tm - halo:tm, :]
    y = y * (0.5 * jnp.tanh(0.5 * y) + 0.5)
    is_qk = c < n_qk_blocks
    for g in range(tn // LANES):
        seg = y[:, g * LANES:(g + 1) * LANES]
        inv = lax.rsqrt(jnp.sum(seg * seg, axis=-1, keepdims=True) + 1e-6)
        o_ref[:, g * LANES:(g + 1) * LANES] = (seg * jnp.where(is_qk, inv, 1.0)).astype(o_ref.dtype)


def _qkv_conv_proj(xb, w_qkv, conv_w):
    s, d = xb.shape
    c = w_qkv.shape[1]
    tm = min(1024, s)
    tn = 1024
    n_qk_blocks = (2 * GDN_QK_HEADS * GDN_DK) // tn
    kern = functools.partial(_qkv_conv_kernel, tm=tm, tn=tn, n_qk_blocks=n_qk_blocks)
    return pl.pallas_call(
        kern,
        grid=(c // tn, s // tm),
        in_specs=[pl.BlockSpec((tm, d), lambda cc, i: (i, 0)),
                  pl.BlockSpec((d, tn), lambda cc, i: (0, cc)),
                  pl.BlockSpec((GDN_CONV, tn), lambda cc, i: (0, cc))],
        out_specs=pl.BlockSpec((tm, tn), lambda cc, i: (i, cc)),
        out_shape=jax.ShapeDtypeStruct((s, c), BF16),
        scratch_shapes=[pltpu.VMEM((tm + CONV_HALO, tn), F32)],
        compiler_params=_cparams("parallel", "arbitrary"),
        name="qkv_conv_proj",
    )(xb, w_qkv, conv_w)


def _softplus(x):
    return jnp.maximum(x, 0.0) + jnp.log1p(jnp.exp(-jnp.abs(x)))


def _gdn_chunk_kernel(q_ref, k_ref, v_ref, z_ref, ac_ref, bc_ref, ar_ref, alc_ref, dtc_ref, alr_ref, dtr_ref,
                      nw_ref, o_ref, state):
    n = pl.program_id(1)
    g_heads = GDN_GROUP
    c = CHUNK

    @pl.when(n == 0)
    def _():
        state[...] = jnp.zeros(state.shape, F32)

    ri = lax.broadcasted_iota(jnp.int32, (c, c), 0)
    ci = lax.broadcasted_iota(jnp.int32, (c, c), 1)
    tril = ci <= ri
    strict = ci < ri
    ltri = tril.astype(F32)
    utri = (ri <= ci).astype(F32)
    eye = (ri == ci).astype(F32)

    g_col = -jnp.exp(alc_ref[...]) * _softplus(ac_ref[...] + dtc_ref[...])
    gc_col = _dot(ltri, g_col, precision=HIGHEST)
    g_row = -jnp.exp(alr_ref[...]) * _softplus(ar_ref[...] + dtr_ref[...])
    gc_row = _dot(g_row, utri, precision=HIGHEST)
    beta_col = _sigmoid(bc_ref[...])
    nw = nw_ref[...]
    scale = float(GDN_DK ** -0.5)

    vheads = range(g_heads)
    qkheads = range(g_heads // 2)
    qs = [q_ref[:, h * GDN_DK:(h + 1) * GDN_DK] for h in qkheads]
    ks = [k_ref[:, h * GDN_DK:(h + 1) * GDN_DK] for h in qkheads]
    kqs = [_dot_nt(jnp.concatenate([ks[h], qs[h]], axis=0), ks[h]) for h in qkheads]
    bcol = [beta_col[:, h:h + 1] for h in vheads]
    gcol = [gc_col[:, h:h + 1] for h in vheads]
    glast = [gc_col[c - 1:c, h:h + 1] for h in vheads]
    decay = [jnp.where(tril, jnp.exp(jnp.where(tril, gcol[h] - gc_row[h:h + 1, :], 0.0)), 0.0) for h in vheads]
    m = [jnp.where(strict, kqs[h // 2][:c] * bcol[h] * decay[h], 0.0) for h in vheads]
    a_qk = [(jnp.where(tril, kqs[h // 2][c:] * decay[h], 0.0) * scale).astype(BF16) for h in vheads]
    t_inv = [eye - m[h] for h in vheads]
    xp = [m[h].astype(BF16) for h in vheads]
    xp = [_dot(x, x).astype(BF16) for x in xp]
    for _ in range(4):
        prod = [_dot(jnp.concatenate([x, t.astype(BF16)], axis=0), x) for x, t in zip(xp, t_inv)]
        xp = [p[:c].astype(BF16) for p in prod]
        t_inv = [t + p[c:] for t, p in zip(t_inv, prod)]
    t_inv = [t + _dot(t.astype(BF16), x) for t, x in zip(t_inv, xp)]
    egc = [jnp.exp(gcol[h]) for h in vheads]
    kf = [ks[h].astype(F32) for h in qkheads]
    rhs = [jnp.concatenate([v_ref[:, h * GDN_DV:(h + 1) * GDN_DV].astype(F32) * bcol[h],
                            kf[h // 2] * (bcol[h] * egc[h])], axis=1).astype(BF16) for h in vheads]
    uw = [_dot(t_inv[h].astype(BF16), rhs[h]) for h in vheads]
    st = [state[h] for h in vheads]
    lhs = [jnp.concatenate([uw[h][:, GDN_DV:], qs[h // 2].astype(F32) * egc[h]], axis=0).astype(BF16)
           for h in vheads]
    ws = [_dot(lhs[h], st[h].astype(BF16)) for h in vheads]
    v_new = [(uw[h][:, :GDN_DV] - ws[h][:c]).astype(BF16) for h in vheads]
    kdec = [(kf[h // 2] * jnp.exp(glast[h] - gcol[h])).astype(BF16) for h in vheads]
    o = [ws[h][c:] * scale + _dot(a_qk[h], v_new[h]) for h in vheads]
    for h in vheads:
        state[h] = st[h] * jnp.exp(glast[h]) + _dot_tn(kdec[h], v_new[h])
    for h in vheads:
        on = o[h] * lax.rsqrt(jnp.mean(o[h] * o[h], axis=-1, keepdims=True) + RMS_EPS) * nw
        zz = z_ref[:, h * GDN_DV:(h + 1) * GDN_DV].astype(F32)
        o_ref[:, h * GDN_DV:(h + 1) * GDN_DV] = (on * (zz * _sigmoid(zz))).astype(o_ref.dtype)


def _gdn_delta(qkv, z, a, b, a_log, dt_bias, norm_w):
    s = qkv.shape[0]
    g = GDN_GROUP
    ng = GDN_V_HEADS // g
    nc = s // CHUNK
    gq = g // 2
    wq = gq * GDN_DK
    wv = g * GDN_DV
    k_off = (GDN_QK_HEADS * GDN_DK) // wq
    v_off = (2 * GDN_QK_HEADS * GDN_DK) // wv
    a_col = a.reshape(s, ng, g).transpose(1, 0, 2)
    b_col = b.reshape(s, ng, g).transpose(1, 0, 2)
    a_row = a.reshape(nc, CHUNK, ng, g).transpose(2, 0, 3, 1)
    al_c = a_log.reshape(ng, 1, g)
    dt_c = dt_bias.reshape(ng, 1, g)
    al_r = a_log.reshape(ng, g, 1)
    dt_r = dt_bias.reshape(ng, g, 1)
    return pl.pallas_call(
        _gdn_chunk_kernel,
        grid=(ng, nc),
        in_specs=[pl.BlockSpec((CHUNK, wq), lambda hg, n: (n, hg)),
                  pl.BlockSpec((CHUNK, wq), lambda hg, n: (n, k_off + hg)),
                  pl.BlockSpec((CHUNK, wv), lambda hg, n: (n, v_off + hg)),
                  pl.BlockSpec((CHUNK, wv), lambda hg, n: (n, hg)),
                  pl.BlockSpec((None, CHUNK, g), lambda hg, n: (hg, n, 0)),
                  pl.BlockSpec((None, CHUNK, g), lambda hg, n: (hg, n, 0)),
                  pl.BlockSpec((None, None, g, CHUNK), lambda hg, n: (hg, n, 0, 0)),
                  pl.BlockSpec((None, 1, g), lambda hg, n: (hg, 0, 0)),
                  pl.BlockSpec((None, 1, g), lambda hg, n: (hg, 0, 0)),
                  pl.BlockSpec((None, g, 1), lambda hg, n: (hg, 0, 0)),
                  pl.BlockSpec((None, g, 1), lambda hg, n: (hg, 0, 0)),
                  pl.BlockSpec((1, GDN_DV), lambda hg, n: (0, 0))],
        out_specs=pl.BlockSpec((CHUNK, wv), lambda hg, n: (n, hg)),
        out_shape=jax.ShapeDtypeStruct((s, GDN_V_HEADS * GDN_DV), BF16),
        scratch_shapes=[pltpu.VMEM((g, GDN_DK, GDN_DV), F32)],
        compiler_params=_cparams("parallel", "arbitrary"),
        name="gdn_delta",
    )(qkv, qkv, qkv, z, a_col, b_col, a_row, al_c, dt_c, al_r, dt_r, norm_w)


def _branch_kernel(om_ref, og_ref, wm_ref, wg_ref, gm_ref, gg_ref, bm_ref, bg_ref, o_ref):
    ym = _dot(om_ref[...], wm_ref[...])
    yg = _dot(og_ref[...], wg_ref[...])
    sm = _sigmoid(gm_ref[...].astype(F32) + bm_ref[...])
    sg = _sigmoid(gg_ref[...].astype(F32) + bg_ref[...])
    o_ref[...] = (sm * ym + sg * yg).astype(o_ref.dtype)


def _branch_merge(o_mla, o_gdn, w_o_mla, w_o_gdn, gates, b_gate):
    s = o_mla.shape[0]
    d = w_o_mla.shape[1]
    tm = min(1024, s)
    tn = min(512, d)
    nb = d // tn
    return pl.pallas_call(
        _branch_kernel,
        grid=(nb, s // tm),
        in_specs=[pl.BlockSpec((tm, o_mla.shape[1]), lambda j, i: (i, 0)),
                  pl.BlockSpec((tm, o_gdn.shape[1]), lambda j, i: (i, 0)),
                  pl.BlockSpec((w_o_mla.shape[0], tn), lambda j, i: (0, j)),
                  pl.BlockSpec((w_o_gdn.shape[0], tn), lambda j, i: (0, j)),
                  pl.BlockSpec((tm, tn), lambda j, i: (i, j)),
                  pl.BlockSpec((tm, tn), lambda j, i: (i, nb + j)),
                  pl.BlockSpec((1, tn), lambda j, i: (0, j)),
                  pl.BlockSpec((1, tn), lambda j, i: (0, nb + j))],
        out_specs=pl.BlockSpec((tm, tn), lambda j, i: (i, j)),
        out_shape=jax.ShapeDtypeStruct((s, d), BF16),
        compiler_params=_cparams("parallel", "parallel"),
        name="branch_merge",
    )(o_mla, o_gdn, w_o_mla, w_o_gdn, gates, gates, b_gate, b_gate)


def _out_ln_kernel(m_ref, w_ref, x_ref, g_ref, b_ref, o_ref):
    y = DN_ALPHA * x_ref[...] + _dot(m_ref[...], w_ref[...])
    o_ref[...] = _layernorm(y, g_ref[...], b_ref[...])


def _out_ln(mixed, w_out, x, g, b):
    s, d = x.shape
    tm = min(512, s)
    return pl.pallas_call(
        _out_ln_kernel,
        grid=(s // tm,),
        in_specs=[pl.BlockSpec((tm, d), lambda i: (i, 0)),
                  pl.BlockSpec((d, d), lambda i: (0, 0)),
                  pl.BlockSpec((tm, d), lambda i: (i, 0)),
                  pl.BlockSpec((1, d), lambda i: (0, 0)),
                  pl.BlockSpec((1, d), lambda i: (0, 0))],
        out_specs=pl.BlockSpec((tm, d), lambda i: (i, 0)),
        out_shape=jax.ShapeDtypeStruct((s, d), F32),
        compiler_params=_cparams("parallel"),
        name="out_ln1",
    )(mixed, w_out, x, g, b)


SLAB = 16


def _store_rows(ref2, val, per_token, row0=0):
    n = val.shape[0]
    for s in range(val.shape[1] // LANES):
        ref2[pl.ds(row0 + s, n, stride=per_token), :] = val[:, s * LANES:(s + 1) * LANES]


def _load_rows(ref2, n, per_token, row0, k):
    return jnp.concatenate([ref2[pl.ds(row0 + s, n, stride=per_token), :] for s in range(k)], axis=1)


def _mem_kernel(x_ref, wq_ref, kv_ref, wo_ref, g_ref, b_ref, wr_ref, br_ref, x2_ref, x2r_ref, lg_ref):
    x1 = x_ref[...]
    hd = MEM_HEAD_DIM
    nh = MEM_HEADS
    q = (_dot(x1.astype(BF16), wq_ref[...]) * float(hd ** -0.5)).astype(BF16)
    outs = []
    for h in range(nh):
        kh = kv_ref[:, h * hd:(h + 1) * hd]
        vh = kv_ref[:, (nh + h) * hd:(nh + h + 1) * hd]
        s = _dot_nt(q[:, h * hd:(h + 1) * hd], kh)
        p = jnp.exp(s - jnp.max(s, axis=-1, keepdims=True))
        o = _dot(p.astype(BF16), vh) / jnp.sum(p, axis=-1, keepdims=True)
        outs.append(o.astype(BF16))
    o = jnp.concatenate(outs, axis=1)
    y = DN_ALPHA * x1 + _dot(o, wo_ref[...])
    x2 = _layernorm(y, g_ref[...], b_ref[...])
    x2_ref[...] = x2
    _store_rows(x2r_ref, x2, SLAB)
    x_hi = x2.astype(BF16)
    x_lo = (x2 - x_hi.astype(F32)).astype(BF16)
    w = wr_ref[...]
    w_hi = w.astype(BF16)
    w_lo = (w - w_hi.astype(F32)).astype(BF16)
    lg_ref[...] = (_dot(x_hi, w_hi) + _dot(x_lo, w_hi)) + _dot(x_hi, w_lo) + br_ref[...]


def _mem_attn_ln(x1, w_mq, kvm, w_mo, g, b, w_r, b_r):
    s, d = x1.shape
    tm = min(512, s)
    return pl.pallas_call(
        _mem_kernel,
        grid=(s // tm,),
        in_specs=[pl.BlockSpec((tm, d), lambda i: (i, 0)),
                  pl.BlockSpec(w_mq.shape, lambda i: (0, 0)),
                  pl.BlockSpec(kvm.shape, lambda i: (0, 0)),
                  pl.BlockSpec(w_mo.shape, lambda i: (0, 0)),
                  pl.BlockSpec((1, d), lambda i: (0, 0)),
                  pl.BlockSpec((1, d), lambda i: (0, 0)),
                  pl.BlockSpec(w_r.shape, lambda i: (0, 0)),
                  pl.BlockSpec((1, LANES), lambda i: (0, 0))],
        out_specs=[pl.BlockSpec((tm, d), lambda i: (i, 0)),
                   pl.BlockSpec((tm * SLAB, LANES), lambda i: (i, 0)),
                   pl.BlockSpec((tm, LANES), lambda i: (i, 0))],
        out_shape=[jax.ShapeDtypeStruct((s, d), F32),
                   jax.ShapeDtypeStruct((s * SLAB, LANES), F32),
                   jax.ShapeDtypeStruct((s, LANES), F32)],
        compiler_params=_cparams("parallel"),
        name="mem_attn_ln2",
    )(x1, w_mq, kvm, w_mo, g, b, w_r, b_r)


SEL_E1, SEL_E2, SEL_R1, SEL_R2, SEL_G1, SEL_G2 = range(6)
GRP_LANE0 = N_EXPERTS


def _route_kernel(lg_ref, sel_ref, cnt_ref, carry):
    i = pl.program_id(0)

    @pl.when(i == 0)
    def _():
        carry[...] = jnp.zeros(carry.shape, F32)

    lg = lg_ref[...]
    tm = lg.shape[0]
    lane = lax.broadcasted_iota(jnp.int32, lg.shape, 1)
    big = jnp.int32(4 * LANES)
    neg = jnp.float32(-jnp.inf)

    def first_max(vals):
        mx = jnp.max(vals, axis=-1, keepdims=True)
        idx = jnp.min(jnp.where(vals == mx, lane, big), axis=-1, keepdims=True)
        return mx, idx

    is_grp = (lane >= GRP_LANE0) & (lane < GRP_LANE0 + N_GROUPS)
    gl = jnp.where(is_grp, lg, neg)
    gmax, gidx = first_max(gl)
    p_top = 1.0 / jnp.sum(jnp.where(is_grp, jnp.exp(gl - gmax), 0.0), axis=-1, keepdims=True)
    lo = (gidx - GRP_LANE0) * EXPERTS_PER_GROUP
    in_grp = (lane >= lo) & (lane < lo + EXPERTS_PER_GROUP)
    el = jnp.where(in_grp, lg, neg)
    m1, i1 = first_max(el)
    m2, i2 = first_max(jnp.where(lane == i1, neg, el))
    r = jnp.exp(m2 - m1)
    g1 = p_top / (1.0 + r)
    g2 = p_top * r / (1.0 + r)

    hot1 = lane == i1
    hot2 = lane == i2
    onehot = jnp.where(hot1, 1.0, 0.0) + jnp.where(hot2, 1.0, 0.0)
    ri = lax.broadcasted_iota(jnp.int32, (tm, tm), 0)
    ci = lax.broadcasted_iota(jnp.int32, (tm, tm), 1)
    before = jnp.where(ci < ri, 1.0, 0.0).astype(BF16)
    rank = _dot(before, onehot.astype(BF16)) + carry[...]
    r1 = jnp.sum(jnp.where(hot1, rank, 0.0), axis=-1, keepdims=True)
    r2 = jnp.sum(jnp.where(hot2, rank, 0.0), axis=-1, keepdims=True)
    carry[...] = carry[...] + jnp.sum(onehot, axis=0, keepdims=True)
    cnt_ref[...] = carry[...]

    out = jnp.zeros(lg.shape, F32)
    for ln, val in ((SEL_E1, i1.astype(F32)), (SEL_E2, i2.astype(F32)), (SEL_R1, r1), (SEL_R2, r2),
                    (SEL_G1, g1), (SEL_G2, g2)):
        out = jnp.where(lane == ln, val, out)
    sel_ref[...] = out


def _route(logits):
    t = logits.shape[0]
    tm = min(512, t)
    return pl.pallas_call(
        _route_kernel,
        grid=(t // tm,),
        in_specs=[pl.BlockSpec((tm, LANES), lambda i: (i, 0))],
        out_specs=[pl.BlockSpec((tm, LANES), lambda i: (i, 0)),
                   pl.BlockSpec((1, LANES), lambda i: (0, 0))],
        out_shape=[jax.ShapeDtypeStruct((t, LANES), F32),
                   jax.ShapeDtypeStruct((1, LANES), F32)],
        scratch_shapes=[pltpu.VMEM((1, LANES), F32)],
        compiler_params=_cparams("arbitrary"),
        name="moe_route",
    )(logits)


def _dispatch_kernel(cnt_ref, e1_ref, e2_ref, r1_ref, r2_ref, rowa_ref, blk_ref, rows_ref, d1_ref, d2_ref, pstart,
                     used_ref, *, t, n_blocks):
    def seg(e, start):
        pstart[e] = start
        c = cnt_ref[e]
        nb = (c + MOE_ROWS - 1) // MOE_ROWS
        end = start + nb * MOE_ROWS

        def mark(p, carry):
            rowa_ref[p] = -1
            return carry
        lax.fori_loop(start + c, end, mark, 0)

        def blk(bi, carry):
            blk_ref[bi] = e
            rows_ref[bi] = jnp.minimum(start + c - bi * MOE_ROWS, MOE_ROWS)
            return carry
        lax.fori_loop(start // MOE_ROWS, end // MOE_ROWS, blk, 0)
        return end

    total = lax.fori_loop(0, N_EXPERTS, seg, 0)
    used = total // MOE_ROWS
    used_ref[0] = used
    last = blk_ref[jnp.maximum(used - 1, 0)]

    def tail(bi, carry):
        blk_ref[bi] = last
        rows_ref[bi] = 0

        def unused(p, c2):
            rowa_ref[bi * MOE_ROWS + p] = -1
            return c2
        lax.fori_loop(0, MOE_ROWS, unused, 0, unroll=16)
        return carry
    lax.fori_loop(used, n_blocks, tail, 0)

    def place(tok, carry):
        p1 = pstart[e1_ref[tok]] + r1_ref[tok]
        p2 = pstart[e2_ref[tok]] + r2_ref[tok]
        rowa_ref[p1] = 2 * tok
        rowa_ref[p2] = 2 * tok + 1
        d1_ref[tok] = p1
        d2_ref[tok] = p2
        return carry
    lax.fori_loop(0, t, place, 0, unroll=8)


def _dispatch(cnt, e1, e2, r1, r2, n_blocks):
    t = e1.shape[0]
    smem = pl.BlockSpec(memory_space=pltpu.SMEM)
    i32 = lambda n: jax.ShapeDtypeStruct((n,), jnp.int32)
    return pl.pallas_call(
        functools.partial(_dispatch_kernel, t=t, n_blocks=n_blocks),
        in_specs=[smem] * 5,
        out_specs=[smem] * 7,
        out_shape=[i32(n_blocks * MOE_ROWS), i32(n_blocks), i32(n_blocks), i32(t), i32(t), i32(N_EXPERTS), i32(1)],
        name="moe_dispatch",
    )(cnt, e1, e2, r1, r2)


def _scatter_in_kernel(d1_ref, d2_ref, cnt_ref, pstart_ref, used_ref, x_ref, xs_hbm, zbuf, sem, zsem, *, tm, n_blocks):
    i = pl.program_id(0)
    blk_rows = MOE_ROWS * SLAB

    def slab(j):
        return pl.ds(pl.multiple_of(j * SLAB, SLAB), SLAB)

    def block(bi):
        return pl.ds(pl.multiple_of(bi * blk_rows, blk_rows), blk_rows)

    def pad_range(e):
        c = cnt_ref[e]
        lo = pstart_ref[e] + c
        return lo, pstart_ref[e] + (c + MOE_ROWS - 1) // MOE_ROWS * MOE_ROWS

    @pl.when(i == 0)
    def _():
        zbuf[...] = jnp.zeros(zbuf.shape, F32)

        def zero_pads(e, carry):
            lo, hi = pad_range(e)

            def z(p, c2):
                pltpu.make_async_copy(zbuf.at[slab(0)], xs_hbm.at[slab(p)], zsem).start()
                return c2
            lax.fori_loop(lo, hi, z, 0)
            return carry
        lax.fori_loop(0, N_EXPERTS, zero_pads, 0)

        def zero_block(bi, carry):
            pltpu.make_async_copy(zbuf, xs_hbm.at[block(bi)], zsem).start()
            return carry
        lax.fori_loop(used_ref[0], n_blocks, zero_block, 0)

    def scatter(r, carry):
        tok = i * tm + r
        pltpu.make_async_copy(x_ref.at[slab(r)], xs_hbm.at[slab(d1_ref[tok])], sem).start()
        pltpu.make_async_copy(x_ref.at[slab(r)], xs_hbm.at[slab(d2_ref[tok])], sem).start()
        return carry
    lax.fori_loop(0, tm, scatter, 0, unroll=8)

    for _ in range(2):
        pltpu.make_async_copy(x_ref, xs_hbm.at[pl.ds(0, tm * SLAB)], sem).wait()

    @pl.when(i == pl.num_programs(0) - 1)
    def _():
        def wait_pads(e, carry):
            lo, hi = pad_range(e)
            n = hi - lo

            @pl.when(n > 0)
            def _():
                rows = pl.ds(0, pl.multiple_of(n * SLAB, SLAB))
                pltpu.make_async_copy(zbuf.at[rows], xs_hbm.at[rows], zsem).wait()
            return carry
        lax.fori_loop(0, N_EXPERTS, wait_pads, 0)

        def wait_block(bi, carry):
            pltpu.make_async_copy(zbuf, xs_hbm.at[block(0)], zsem).wait()
            return carry
        lax.fori_loop(used_ref[0], n_blocks, wait_block, 0)


def _scatter_in(x2r, d1, d2, cnt, pstart, used, n_blocks):
    t = x2r.shape[0] // SLAB
    tm = min(MOE_ROWS, t)
    grid_spec = pltpu.PrefetchScalarGridSpec(
        num_scalar_prefetch=5,
        grid=(t // tm,),
        in_specs=[pl.BlockSpec((tm * SLAB, LANES), lambda i, *_: (i, 0))],
        out_specs=pl.BlockSpec(memory_space=pl.ANY),
        scratch_shapes=[pltpu.VMEM((MOE_ROWS * SLAB, LANES), F32),
                        pltpu.SemaphoreType.DMA(()), pltpu.SemaphoreType.DMA(())],
    )
    return pl.pallas_call(
        functools.partial(_scatter_in_kernel, tm=tm, n_blocks=n_blocks),
        grid_spec=grid_spec,
        out_shape=jax.ShapeDtypeStruct((n_blocks * MOE_ROWS * SLAB, LANES), F32),
        compiler_params=_cparams("arbitrary"),
        name="moe_scatter_in",
    )(d1, d2, cnt, pstart, used, x2r)


def _expert_kernel(blk_ref, rowa_ref, rows_ref, used_ref, x_ref, wg_ref, wu_ref, wd_ref, out_hbm, ybuf, ssem):
    b = pl.program_id(0)
    n_blocks = pl.num_programs(0)
    slot = b % 2

    def slab(i):
        return pl.ds(pl.multiple_of(i * SLAB, SLAB), SLAB)

    def scatter_copy(blk_slot, r, dst):
        return pltpu.make_async_copy(ybuf.at[blk_slot, slab(r)], out_hbm.at[slab(dst)], ssem.at[blk_slot])

    def wait_scatters(blk, blk_slot):
        n = rows_ref[blk]

        @pl.when(n > 0)
        def _():
            rows = pl.ds(0, pl.multiple_of(n * SLAB, SLAB))
            pltpu.make_async_copy(ybuf.at[blk_slot, rows], out_hbm.at[rows], ssem.at[blk_slot]).wait()

    @pl.when(b >= 2)
    def _():
        wait_scatters(jnp.maximum(b - 2, 0), slot)

    n_rows = rows_ref[b]

    @pl.when(n_rows > 0)
    def _():
        xb = _load_rows(x_ref, MOE_ROWS, SLAB, 0, SLAB).astype(BF16)
        hg = _dot(xb, wg_ref[...].astype(BF16))
        hu = _dot(xb, wu_ref[...].astype(BF16))
        hb = (hg * _sigmoid(hg) * hu).astype(BF16)
        _store_rows(ybuf.at[slot], _dot(hb, wd_ref[...].astype(BF16)), SLAB)

        def issue_scatter(r, carry):
            scatter_copy(slot, r, rowa_ref[b * MOE_ROWS + r]).start()
            return carry
        lax.fori_loop(0, n_rows, issue_scatter, 0)

    @pl.when(b == n_blocks - 1)
    def _():
        @pl.when(b >= 1)
        def _():
            wait_scatters(jnp.maximum(b - 1, 0), 1 - slot)
        wait_scatters(b, slot)


def _experts(xs, t, row_a, blk_exp, blk_rows, used, w_gate_e, w_up_e, w_down_e):
    d = SLAB * LANES
    n_blocks = blk_exp.shape[0]
    de = w_gate_e.shape[2]
    weights = lambda b, blk, rowa, rows, nu: (blk[b], 0, 0)
    grid_spec = pltpu.PrefetchScalarGridSpec(
        num_scalar_prefetch=4,
        grid=(n_blocks,),
        in_specs=[pl.BlockSpec((MOE_ROWS * SLAB, LANES), lambda b, blk, rowa, rows, nu: (jnp.minimum(b, nu[0] - 1), 0)),
                  pl.BlockSpec((None, d, de), weights),
                  pl.BlockSpec((None, d, de), weights),
                  pl.BlockSpec((None, de, d), weights)],
        out_specs=pl.BlockSpec(memory_space=pl.ANY),
        scratch_shapes=[pltpu.VMEM((2, MOE_ROWS * SLAB, LANES), F32), pltpu.SemaphoreType.DMA((2,))],
    )
    return pl.pallas_call(
        _expert_kernel,
        grid_spec=grid_spec,
        out_shape=jax.ShapeDtypeStruct((2 * t * SLAB, LANES), F32),
        compiler_params=_cparams("arbitrary"),
        name="moe_experts",
    )(blk_exp, row_a, blk_rows, used, xs, w_gate_e, w_up_e, w_down_e)


def _combine_kernel(x_ref, y_ref, sel_ref, g_ref, b_ref, o_ref):
    tm = x_ref.shape[0]
    sel = sel_ref[...]
    g1 = sel[:, SEL_G1:SEL_G1 + 1]
    g2 = sel[:, SEL_G2:SEL_G2 + 1]
    y1 = _load_rows(y_ref, tm, 2 * SLAB, 0, SLAB)
    y2 = _load_rows(y_ref, tm, 2 * SLAB, SLAB, SLAB)
    y = DN_ALPHA * x_ref[...] + (g1 * y1 + g2 * y2)
    o_ref[...] = _layernorm(y, g_ref[...], b_ref[...])


def _combine_ln(x2, ys, sel, g, b):
    t, d = x2.shape
    tm = min(512, t)
    return pl.pallas_call(
        _combine_kernel,
        grid=(t // tm,),
        in_specs=[pl.BlockSpec((tm, d), lambda i: (i, 0)),
                  pl.BlockSpec((tm * 2 * SLAB, LANES), lambda i: (i, 0)),
                  pl.BlockSpec((tm, LANES), lambda i: (i, 0)),
                  pl.BlockSpec((1, d), lambda i: (0, 0)),
                  pl.BlockSpec((1, d), lambda i: (0, 0))],
        out_specs=pl.BlockSpec((tm, d), lambda i: (i, 0)),
        out_shape=jax.ShapeDtypeStruct((t, d), F32),
        compiler_params=_cparams("parallel"),
        name="combine_ln3",
    )(x2, ys, sel, g, b)


def _mixer(xb, positions, w_in, b_gate, mla_q_norm, w_uq, mla_kv_norm, w_ukv, w_o_mla,
           gdn_conv, gdn_a_log, gdn_dt_bias, gdn_norm, w_o_gdn):
    s, d = xb.shape
    nqkv = 2 * GDN_QK_HEADS * GDN_DK + GDN_V_HEADS * GDN_DV
    nz = GDN_V_HEADS * GDN_DV
    o0 = 0
    o1 = o0 + MLA_Q_LORA
    o2 = o1 + MLA_KV_LORA + MLA_ROPE
    o3 = o2 + nqkv
    o4 = o3 + nz
    o5 = o4 + GDN_V_HEADS
    o6 = o5 + GDN_V_HEADS

    def rot_cols(w):
        half = MLA_ROPE // 2
        return jnp.concatenate([-w[..., half:], w[..., :half]], axis=-1)

    w_q = w_in[:, o0:o1].astype(BF16)
    w_kpe = w_in[:, o1 + MLA_KV_LORA:o2]
    w_kv = jnp.concatenate([w_in[:, o1:o1 + MLA_KV_LORA], w_kpe, rot_cols(w_kpe)], axis=1).astype(BF16)
    w_ba = jnp.concatenate([w_in[:, o4:o6], jnp.zeros((d, LANES - 2 * GDN_V_HEADS), F32)], axis=1).astype(BF16)
    qd = _matmul(xb, w_q, BF16, 1024, MLA_Q_LORA)
    kvd = _matmul(xb, w_kv, BF16, 1024, w_kv.shape[1])
    qkv_c = _qkv_conv_proj(xb, w_in[:, o2:o3].astype(BF16), gdn_conv)
    hz = _matmul(xb, w_in[:, o3:o4].astype(BF16), BF16, 1024, 2048)
    hba = _matmul(xb, w_ba, F32, 512, LANES)
    hgate = _matmul(xb, w_in[:, o6:].astype(BF16), BF16, 1024, 2048)

    cs = _rope_table(positions)
    hq = MLA_NOPE + MLA_ROPE
    wq3 = w_uq.reshape(MLA_Q_LORA, MLA_HEADS, hq)
    pe = wq3[..., MLA_NOPE:]
    wq = jnp.concatenate([pe, rot_cols(pe), wq3[..., :MLA_NOPE]], axis=-1)
    wq = wq.reshape(MLA_Q_LORA, MLA_HEADS * MLA_QK_PAD).astype(BF16)
    wkv3 = w_ukv.reshape(MLA_KV_LORA, MLA_HEADS, MLA_NOPE + MLA_V)
    wk = wkv3[..., :MLA_NOPE].reshape(MLA_KV_LORA, MLA_HEADS * MLA_NOPE).astype(BF16)
    wv = wkv3[..., MLA_NOPE:].reshape(MLA_KV_LORA, MLA_HEADS * MLA_V).astype(BF16)
    q = _mla_q_proj(qd, mla_q_norm.reshape(1, -1), wq, cs)
    k, v = _mla_kv_proj(kvd, mla_kv_norm.reshape(1, -1), wk, wv, cs)
    o_mla = _mla_attention(q, k, v)

    o_gdn = _gdn_delta(qkv_c, hz, hba[:, GDN_V_HEADS:2 * GDN_V_HEADS], hba[:, :GDN_V_HEADS],
                       gdn_a_log, gdn_dt_bias, gdn_norm.reshape(1, -1))

    return _branch_merge(o_mla, o_gdn, w_o_mla.astype(BF16), w_o_gdn.astype(BF16), hgate, b_gate.reshape(1, -1))


def _moe(x2, x2r, logits, w_gate_e, w_up_e, w_down_e, ln_g, ln_b):
    t, d = x2.shape
    n_blocks = (2 * t) // MOE_ROWS + N_EXPERTS
    sel, cnt = _route(logits)
    as_i32 = lambda col: sel[:, col].astype(jnp.int32)
    cnt_i = cnt[0, :N_EXPERTS].astype(jnp.int32)
    row_a, blk_exp, blk_rows, d1, d2, pstart, used = _dispatch(cnt_i, as_i32(SEL_E1), as_i32(SEL_E2),
                                                               as_i32(SEL_R1), as_i32(SEL_R2), n_blocks)
    xs = _scatter_in(x2r, d1, d2, cnt_i, pstart, used, n_blocks)
    ys = _experts(xs, t, row_a, blk_exp, blk_rows, used, w_gate_e, w_up_e, w_down_e)
    return _combine_ln(x2, ys, sel, ln_g, ln_b)


def _layer(x, mem, positions, w_in, b_gate, mla_q_norm, w_uq, mla_kv_norm, w_ukv, w_o_mla,
           gdn_conv, gdn_a_log, gdn_dt_bias, gdn_norm, w_o_gdn, w_out, ln1_g, ln1_b,
           w_mq, w_mkv, w_mo, ln2_g, ln2_b, w_route_grp, b_route_grp, w_route_exp, b_route_exp,
           w_gate_e, w_up_e, w_down_e, ln3_g, ln3_b):
    d = x.shape[1]
    row = lambda p: p.reshape(1, -1)
    mixed = _mixer(x.astype(BF16), positions, w_in, b_gate, mla_q_norm, w_uq, mla_kv_norm, w_ukv, w_o_mla,
                   gdn_conv, gdn_a_log, gdn_dt_bias, gdn_norm, w_o_gdn)
    x1 = _out_ln(mixed, w_out.astype(BF16), x, row(ln1_g), row(ln1_b))

    kvm = _matmul(mem.astype(BF16), w_mkv.astype(BF16), BF16, 256, 512)
    pad = LANES - N_EXPERTS - N_GROUPS
    w_r = jnp.concatenate([w_route_exp, w_route_grp, jnp.zeros((d, pad), F32)], axis=1)
    b_r = jnp.concatenate([b_route_exp, b_route_grp, jnp.zeros((pad,), F32)]).reshape(1, LANES)
    x2, x2r, logits = _mem_attn_ln(x1, w_mq.astype(BF16), kvm, w_mo.astype(BF16), row(ln2_g), row(ln2_b), w_r, b_r)

    return _moe(x2, x2r, logits, w_gate_e, w_up_e, w_down_e, row(ln3_g), row(ln3_b))


def kernel(x, mem, positions, w_in, b_gate, mla_q_norm, w_uq, mla_kv_norm, w_ukv, w_o_mla, gdn_conv, gdn_a_log,
           gdn_dt_bias, gdn_norm, w_o_gdn, w_out, ln1_g, ln1_b, w_mq, w_mkv, w_mo, ln2_g, ln2_b, w_route_grp,
           b_route_grp, w_route_exp, b_route_exp, w_gate_e, w_up_e, w_down_e, ln3_g, ln3_b):
    outs = []
    for bi in range(x.shape[0]):
        h = x[bi]
        for l in range(w_in.shape[0]):
            h = _layer(h, mem[bi], positions[bi], w_in[l], b_gate[l], mla_q_norm[l], w_uq[l], mla_kv_norm[l],
                       w_ukv[l], w_o_mla[l], gdn_conv[l], gdn_a_log[l], gdn_dt_bias[l], gdn_norm[l], w_o_gdn[l],
                       w_out[l], ln1_g[l], ln1_b[l], w_mq[l], w_mkv[l], w_mo[l], ln2_g[l], ln2_b[l],
                       w_route_grp[l], b_route_grp[l], w_route_exp[l], b_route_exp[l], w_gate_e[l], w_up_e[l],
                       w_down_e[l], ln3_g[l], ln3_b[l])
        outs.append(h)
    return jnp.stack(outs, axis=0)
```

```python
import functools

import numpy as np
import jax
import jax.numpy as jnp
from jax import lax
from jax.experimental import pallas as pl
from jax.experimental.pallas import tpu as pltpu

F32 = jnp.float32
BF16 = jnp.bfloat16
HIGHEST = lax.Precision.HIGHEST

LANES = 128
VMEM_LIMIT = 56 * 1024 * 1024

CHUNK = 64
MLA_HEADS = 16
MLA_Q_LORA = 768
MLA_KV_LORA = 512
MLA_NOPE = 128
MLA_ROPE = 64
MLA_V = 128
MLA_QK_PAD = 256
ROPE_THETA = 10000.0
GDN_QK_HEADS = 16
GDN_V_HEADS = 32
GDN_DK = 128
GDN_DV = 128
GDN_CONV = 4
GDN_GROUP = 32
MEM_HEADS = 4
MEM_HEAD_DIM = 128
N_GROUPS = 8
EXPERTS_PER_GROUP = 8
N_EXPERTS = 64
D_EXPERT = 512
MOE_ROWS = 256
RMS_EPS = 1e-6
LN_EPS = 1e-5
DN_ALPHA = 2.0 ** 0.25


def _cparams(*sem):
    return pltpu.CompilerParams(dimension_semantics=sem, vmem_limit_bytes=VMEM_LIMIT)


def _dot(a, b, **kw):
    return jnp.dot(a, b, preferred_element_type=F32, **kw)


def _dot_nt(a, b):
    return lax.dot_general(a, b, (((1,), (1,)), ((), ())), preferred_element_type=F32)


def _dot_tn(a, b):
    return lax.dot_general(a, b, (((0,), (0,)), ((), ())), preferred_element_type=F32)


def _sigmoid(x):
    return 1.0 / (1.0 + jnp.exp(-x))


def _layernorm(y, g, b):
    mu = jnp.mean(y, axis=-1, keepdims=True)
    d = y - mu
    var = jnp.mean(d * d, axis=-1, keepdims=True)
    return d * lax.rsqrt(var + LN_EPS) * g + b


def _mm_kernel(x_ref, w_ref, o_ref):
    o_ref[...] = _dot(x_ref[...], w_ref[...]).astype(o_ref.dtype)


def _matmul(x, w, out_dtype, tm, tn):
    m, k = x.shape
    n = w.shape[1]
    tm, tn = min(tm, m), min(tn, n)
    return pl.pallas_call(
        _mm_kernel,
        grid=(n // tn, m // tm),
        in_specs=[pl.BlockSpec((tm, k), lambda j, i: (i, 0)),
                  pl.BlockSpec((k, tn), lambda j, i: (0, j))],
        out_specs=pl.BlockSpec((tm, tn), lambda j, i: (i, j)),
        out_shape=jax.ShapeDtypeStruct((m, n), out_dtype),
        compiler_params=_cparams("parallel", "parallel"),
        name="matmul",
    )(x, w)


def _rope_table_kernel(pos_ref, inv_ref, o_ref):
    ang = pos_ref[...].astype(F32) * inv_ref[...]
    lane = lax.broadcasted_iota(jnp.int32, ang.shape, 1)
    o_ref[...] = jnp.where(lane < MLA_ROPE, jnp.cos(ang), jnp.sin(ang))


def _rope_table(positions):
    s = positions.shape[0]
    inv = 1.0 / (ROPE_THETA ** (np.arange(0, MLA_ROPE, 2, dtype=np.float32) / MLA_ROPE))
    inv4 = jnp.asarray(np.tile(inv.astype(np.float32), 4)[None, :])
    tm = min(512, s)
    return pl.pallas_call(
        _rope_table_kernel,
        grid=(s // tm,),
        in_specs=[pl.BlockSpec((tm, 1), lambda i: (i, 0)),
                  pl.BlockSpec((1, LANES), lambda i: (0, 0))],
        out_specs=pl.BlockSpec((tm, LANES), lambda i: (i, 0)),
        out_shape=jax.ShapeDtypeStruct((s, LANES), F32),
        compiler_params=_cparams("parallel"),
        name="rope_table",
    )(positions.reshape(s, 1), inv4)


def _rope_pair(t, cs):
    a = t * cs
    return a + pltpu.roll(a, MLA_ROPE, axis=1)


def _mla_q_kernel(qd_ref, qn_ref, w_ref, cs_ref, o_ref, *, scale):
    x = qd_ref[...].astype(F32)
    cq = x * lax.rsqrt(jnp.mean(x * x, axis=-1, keepdims=True) + RMS_EPS) * qn_ref[...]
    cqb = cq.astype(BF16)
    cs = cs_ref[...]
    for h in range(MLA_HEADS):
        lo = h * MLA_QK_PAD
        p = _dot(cqb, w_ref[:, lo:lo + MLA_QK_PAD])
        o_ref[:, lo:lo + LANES] = (_rope_pair(p[:, :LANES], cs) * scale).astype(o_ref.dtype)
        o_ref[:, lo + LANES:lo + MLA_QK_PAD] = (p[:, LANES:] * scale).astype(o_ref.dtype)


def _mla_q_proj(qd, q_norm, wq, cs):
    s = qd.shape[0]
    tm = min(512, s)
    n = MLA_HEADS * MLA_QK_PAD
    scale = float((MLA_NOPE + MLA_ROPE) ** -0.5 * np.log2(np.e))
    return pl.pallas_call(
        functools.partial(_mla_q_kernel, scale=scale),
        grid=(s // tm,),
        in_specs=[pl.BlockSpec((tm, MLA_Q_LORA), lambda i: (i, 0)),
                  pl.BlockSpec((1, MLA_Q_LORA), lambda i: (0, 0)),
                  pl.BlockSpec((MLA_Q_LORA, n), lambda i: (0, 0)),
                  pl.BlockSpec((tm, LANES), lambda i: (i, 0))],
        out_specs=pl.BlockSpec((tm, n), lambda i: (i, 0)),
        out_shape=jax.ShapeDtypeStruct((s, n), BF16),
        compiler_params=_cparams("parallel"),
        name="mla_q_proj",
    )(qd, q_norm, wq, cs)


def _mla_kv_kernel(kv_ref, kn_ref, wk_ref, wv_ref, cs_ref, k_ref, v_ref):
    x = kv_ref[:, :MLA_KV_LORA].astype(F32)
    ckv = x * lax.rsqrt(jnp.mean(x * x, axis=-1, keepdims=True) + RMS_EPS) * kn_ref[...]
    cb = ckv.astype(BF16)
    pe = _rope_pair(kv_ref[:, MLA_KV_LORA:].astype(F32), cs_ref[...])
    lane = lax.broadcasted_iota(jnp.int32, pe.shape, 1)
    pe = jnp.where(lane < MLA_ROPE, pe, 0.0).astype(k_ref.dtype)
    kn = _dot(cb, wk_ref[...]).astype(k_ref.dtype)
    for h in range(MLA_HEADS):
        lo = h * MLA_QK_PAD
        k_ref[:, lo:lo + LANES] = pe
        k_ref[:, lo + LANES:lo + MLA_QK_PAD] = kn[:, h * MLA_NOPE:(h + 1) * MLA_NOPE]
    v_ref[...] = _dot(cb, wv_ref[...]).astype(v_ref.dtype)


def _mla_kv_proj(kvd, kv_norm, wk, wv, cs):
    s, w = kvd.shape
    tm = min(512, s)
    nk = MLA_HEADS * MLA_QK_PAD
    nv = MLA_HEADS * MLA_V
    return pl.pallas_call(
        _mla_kv_kernel,
        grid=(s // tm,),
        in_specs=[pl.BlockSpec((tm, w), lambda i: (i, 0)),
                  pl.BlockSpec((1, MLA_KV_LORA), lambda i: (0, 0)),
                  pl.BlockSpec(wk.shape, lambda i: (0, 0)),
                  pl.BlockSpec(wv.shape, lambda i: (0, 0)),
                  pl.BlockSpec((tm, LANES), lambda i: (i, 0))],
        out_specs=[pl.BlockSpec((tm, nk), lambda i: (i, 0)),
                   pl.BlockSpec((tm, nv), lambda i: (i, 0))],
        out_shape=[jax.ShapeDtypeStruct((s, nk), BF16),
                   jax.ShapeDtypeStruct((s, nv), BF16)],
        compiler_params=_cparams("parallel"),
        name="mla_kv_proj",
    )(kvd, kv_norm, wk, wv, cs)


FLASH_HEADS = 4
FLASH_TILE = 1024


def _lane_repeat(x, n):
    return jnp.concatenate([x] * n, axis=1)


def _flash_kernel(qi_ref, kj_ref, q_ref, k_ref, v_ref, o_ref, m_sc, acc_sc):
    t = pl.program_id(1)
    i = qi_ref[t]
    j = kj_ref[t]
    hps = m_sc.shape[0]
    tk = k_ref.shape[0]
    heads = range(hps)

    @pl.when(j == 0)
    def _():
        m_sc[...] = jnp.full(m_sc.shape, -1e30, F32)
        acc_sc[...] = jnp.zeros(acc_sc.shape, F32)

    def update(r0, nr, nk, masked):
        rows = slice(r0, r0 + nr)
        ss = [_dot_nt(q_ref[rows, h * MLA_QK_PAD:(h + 1) * MLA_QK_PAD], k_ref[0:nk, h * MLA_QK_PAD:(h + 1) * MLA_QK_PAD])
              for h in heads]
        if masked:
            r = (lax.broadcasted_iota(jnp.int32, ss[0].shape, 0) + r0) // CHUNK
            c = lax.broadcasted_iota(jnp.int32, ss[0].shape, 1) // CHUNK
            keep = c <= r
            ss = [jnp.where(keep, s, -1e30) for s in ss]
        m_prev = [m_sc[h, rows] for h in heads]
        m_new = [jnp.maximum(mp, jnp.max(s, axis=-1, keepdims=True)) for mp, s in zip(m_prev, ss)]
        alpha = [jnp.exp2(mp - mn) for mp, mn in zip(m_prev, m_new)]
        ps = [jnp.exp2(s - _lane_repeat(mn, nk // LANES)).astype(BF16) for s, mn in zip(ss, m_new)]
        ones = jnp.ones((nk, MLA_V), BF16)
        pv = [_dot(p, jnp.concatenate([v_ref[0:nk, h * MLA_V:(h + 1) * MLA_V], ones], axis=1))
              for h, p in zip(heads, ps)]
        for h in heads:
            acc_sc[h, rows] = _lane_repeat(alpha[h], 2) * acc_sc[h, rows] + pv[h]
            m_sc[h, rows] = m_new[h]

    @pl.when(j < i)
    def _():
        update(0, tk, tk, False)

    @pl.when(j == i)
    def _():
        half = tk // 2
        if half % CHUNK == 0 and half % LANES == 0:
            update(0, half, half, True)
            update(half, half, tk, True)
        else:
            update(0, tk, tk, True)
        for h in heads:
            acc = acc_sc[h]
            o_ref[:, h * MLA_V:(h + 1) * MLA_V] = (acc[:, :MLA_V] / acc[:, MLA_V:]).astype(o_ref.dtype)


def _mla_attention(q, k, v):
    s = q.shape[0]
    t = min(FLASH_TILE, s)
    n = s // t
    hps = FLASH_HEADS
    qi = np.array([i for i in range(n) for _ in range(i + 1)], np.int32)
    kj = np.array([j for i in range(n) for j in range(i + 1)], np.int32)
    grid_spec = pltpu.PrefetchScalarGridSpec(
        num_scalar_prefetch=2,
        grid=(MLA_HEADS // hps, qi.shape[0]),
        in_specs=[pl.BlockSpec((t, hps * MLA_QK_PAD), lambda h, st, qi_r, kj_r: (qi_r[st], h)),
                  pl.BlockSpec((t, hps * MLA_QK_PAD), lambda h, st, qi_r, kj_r: (kj_r[st], h)),
                  pl.BlockSpec((t, hps * MLA_V), lambda h, st, qi_r, kj_r: (kj_r[st], h))],
        out_specs=pl.BlockSpec((t, hps * MLA_V), lambda h, st, qi_r, kj_r: (qi_r[st], h)),
        scratch_shapes=[pltpu.VMEM((hps, t, LANES), F32), pltpu.VMEM((hps, t, 2 * MLA_V), F32)],
    )
    return pl.pallas_call(
        _flash_kernel,
        grid_spec=grid_spec,
        out_shape=jax.ShapeDtypeStruct((s, MLA_HEADS * MLA_V), BF16),
        compiler_params=_cparams("parallel", "arbitrary"),
        name="mla_flash",
    )(jnp.asarray(qi), jnp.asarray(kj), q, k, v)


CONV_HALO = 8


def _qkv_conv_kernel(x_ref, w_ref, cw_ref, o_ref, buf, *, tm, tn, n_qk_blocks):
    c = pl.program_id(0)
    i = pl.program_id(1)
    halo = CONV_HALO

    @pl.when(i == 0)
    def _():
        buf[0:halo, :] = jnp.zeros((halo, tn), F32)

    h = _dot(x_ref[...], w_ref[...])
    buf[halo:halo + tm, :] = h
    ext = buf[...]
    y = cw_ref[GDN_CONV - 1:GDN_CONV, :] * h
    for k in range(1, GDN_CONV):
        y = y + cw_ref[GDN_CONV - 1 - k:GDN_CONV - k, :] * pltpu.roll(ext, k, axis=0)[halo:halo + tm, :]
    buf[0:halo, :] = h[tm - halo:tm, :]
    y = y * (0.5 * jnp.tanh(0.5 * y) + 0.5)
    is_qk = c < n_qk_blocks
    for g in range(tn // LANES):
        seg = y[:, g * LANES:(g + 1) * LANES]
        inv = lax.rsqrt(jnp.sum(seg * seg, axis=-1, keepdims=True) + 1e-6)
        o_ref[:, g * LANES:(g + 1) * LANES] = (seg * jnp.where(is_qk, inv, 1.0)).astype(o_ref.dtype)


def _qkv_conv_proj(xb, w_qkv, conv_w):
    s, d = xb.shape
    c = w_qkv.shape[1]
    tm = min(1024, s)
    tn = 1024
    n_qk_blocks = (2 * GDN_QK_HEADS * GDN_DK) // tn
    kern = functools.partial(_qkv_conv_kernel, tm=tm, tn=tn, n_qk_blocks=n_qk_blocks)
    return pl.pallas_call(
        kern,
        grid=(c // tn, s // tm),
        in_specs=[pl.BlockSpec((tm, d), lambda cc, i: (i, 0)),
                  pl.BlockSpec((d, tn), lambda cc, i: (0, cc)),
                  pl.BlockSpec((GDN_CONV, tn), lambda cc, i: (0, cc))],
        out_specs=pl.BlockSpec((tm, tn), lambda cc, i: (i, cc)),
        out_shape=jax.ShapeDtypeStruct((s, c), BF16),
        scratch_shapes=[pltpu.VMEM((tm + CONV_HALO, tn), F32)],
        compiler_params=_cparams("parallel", "arbitrary"),
        name="qkv_conv_proj",
    )(xb, w_qkv, conv_w)


def _softplus(x):
    return jnp.maximum(x, 0.0) + jnp.log1p(jnp.exp(-jnp.abs(x)))


def _gdn_chunk_kernel(q_ref, k_ref, v_ref, z_ref, ac_ref, bc_ref, ar_ref, alc_ref, dtc_ref, alr_ref, dtr_ref,
                      nw_ref, o_ref, state):
    n = pl.program_id(1)
    g_heads = GDN_GROUP
    c = CHUNK

    @pl.when(n == 0)
    def _():
        state[...] = jnp.zeros(state.shape, F32)

    ri = lax.broadcasted_iota(jnp.int32, (c, c), 0)
    ci = lax.broadcasted_iota(jnp.int32, (c, c), 1)
    tril = ci <= ri
    strict = ci < ri
    ltri = tril.astype(F32)
    utri = (ri <= ci).astype(F32)
    eye = (ri == ci).astype(F32)

    g_col = -jnp.exp(alc_ref[...]) * _softplus(ac_ref[...] + dtc_ref[...])
    gc_col = _dot(ltri, g_col, precision=HIGHEST)
    g_row = -jnp.exp(alr_ref[...]) * _softplus(ar_ref[...] + dtr_ref[...])
    gc_row = _dot(g_row, utri, precision=HIGHEST)
    beta_col = _sigmoid(bc_ref[...])
    nw = nw_ref[...]
    scale = float(GDN_DK ** -0.5)

    vheads = range(g_heads)
    qkheads = range(g_heads // 2)
    qs = [q_ref[:, h * GDN_DK:(h + 1) * GDN_DK] for h in qkheads]
    ks = [k_ref[:, h * GDN_DK:(h + 1) * GDN_DK] for h in qkheads]
    kqs = [_dot_nt(jnp.concatenate([ks[h], qs[h]], axis=0), ks[h]) for h in qkheads]
    bcol = [beta_col[:, h:h + 1] for h in vheads]
    gcol = [gc_col[:, h:h + 1] for h in vheads]
    glast = [gc_col[c - 1:c, h:h + 1] for h in vheads]
    decay = [jnp.where(tril, jnp.exp(jnp.where(tril, gcol[h] - gc_row[h:h + 1, :], 0.0)), 0.0) for h in vheads]
    m = [jnp.where(strict, kqs[h // 2][:c] * bcol[h] * decay[h], 0.0) for h in vheads]
    a_qk = [(jnp.where(tril, kqs[h // 2][c:] * decay[h], 0.0) * scale).astype(BF16) for h in vheads]
    t_inv = [eye - m[h] for h in vheads]
    xp = [m[h].astype(BF16) for h in vheads]
    xp = [_dot(x, x).astype(BF16) for x in xp]
    for _ in range(4):
        prod = [_dot(jnp.concatenate([x, t.astype(BF16)], axis=0), x) for x, t in zip(xp, t_inv)]
        xp = [p[:c].astype(BF16) for p in prod]
        t_inv = [t + p[c:] for t, p in zip(t_inv, prod)]
    t_inv = [t + _dot(t.astype(BF16), x) for t, x in zip(t_inv, xp)]
    egc = [jnp.exp(gcol[h]) for h in vheads]
    kf = [ks[h].astype(F32) for h in qkheads]
    rhs = [jnp.concatenate([v_ref[:, h * GDN_DV:(h + 1) * GDN_DV].astype(F32) * bcol[h],
                            kf[h // 2] * (bcol[h] * egc[h])], axis=1).astype(BF16) for h in vheads]
    uw = [_dot(t_inv[h].astype(BF16), rhs[h]) for h in vheads]
    st = [state[h] for h in vheads]
    lhs = [jnp.concatenate([uw[h][:, GDN_DV:], qs[h // 2].astype(F32) * egc[h]], axis=0).astype(BF16)
           for h in vheads]
    ws = [_dot(lhs[h], st[h].astype(BF16)) for h in vheads]
    v_new = [(uw[h][:, :GDN_DV] - ws[h][:c]).astype(BF16) for h in vheads]
    kdec = [(kf[h // 2] * jnp.exp(glast[h] - gcol[h])).astype(BF16) for h in vheads]
    o = [ws[h][c:] * scale + _dot(a_qk[h], v_new[h]) for h in vheads]
    for h in vheads:
        state[h] = st[h] * jnp.exp(glast[h]) + _dot_tn(kdec[h], v_new[h])
    for h in vheads:
        on = o[h] * lax.rsqrt(jnp.mean(o[h] * o[h], axis=-1, keepdims=True) + RMS_EPS) * nw
        zz = z_ref[:, h * GDN_DV:(h + 1) * GDN_DV].astype(F32)
        o_ref[:, h * GDN_DV:(h + 1) * GDN_DV] = (on * (zz * _sigmoid(zz))).astype(o_ref.dtype)


def _gdn_delta(qkv, z, a, b, a_log, dt_bias, norm_w):
    s = qkv.shape[0]
    g = GDN_GROUP
    ng = GDN_V_HEADS // g
    nc = s // CHUNK
    gq = g // 2
    wq = gq * GDN_DK
    wv = g * GDN_DV
    k_off = (GDN_QK_HEADS * GDN_DK) // wq
    v_off = (2 * GDN_QK_HEADS * GDN_DK) // wv
    a_col = a.reshape(s, ng, g).transpose(1, 0, 2)
    b_col = b.reshape(s, ng, g).transpose(1, 0, 2)
    a_row = a.reshape(nc, CHUNK, ng, g).transpose(2, 0, 3, 1)
    al_c = a_log.reshape(ng, 1, g)
    dt_c = dt_bias.reshape(ng, 1, g)
    al_r = a_log.reshape(ng, g, 1)
    dt_r = dt_bias.reshape(ng, g, 1)
    return pl.pallas_call(
        _gdn_chunk_kernel,
        grid=(ng, nc),
        in_specs=[pl.BlockSpec((CHUNK, wq), lambda hg, n: (n, hg)),
                  pl.BlockSpec((CHUNK, wq), lambda hg, n: (n, k_off + hg)),
                  pl.BlockSpec((CHUNK, wv), lambda hg, n: (n, v_off + hg)),
                  pl.BlockSpec((CHUNK, wv), lambda hg, n: (n, hg)),
                  pl.BlockSpec((None, CHUNK, g), lambda hg, n: (hg, n, 0)),
                  pl.BlockSpec((None, CHUNK, g), lambda hg, n: (hg, n, 0)),
                  pl.BlockSpec((None, None, g, CHUNK), lambda hg, n: (hg, n, 0, 0)),
                  pl.BlockSpec((None, 1, g), lambda hg, n: (hg, 0, 0)),
                  pl.BlockSpec((None, 1, g), lambda hg, n: (hg, 0, 0)),
                  pl.BlockSpec((None, g, 1), lambda hg, n: (hg, 0, 0)),
                  pl.BlockSpec((None, g, 1), lambda hg, n: (hg, 0, 0)),
                  pl.BlockSpec((1, GDN_DV), lambda hg, n: (0, 0))],
        out_specs=pl.BlockSpec((CHUNK, wv), lambda hg, n: (n, hg)),
        out_shape=jax.ShapeDtypeStruct((s, GDN_V_HEADS * GDN_DV), BF16),
        scratch_shapes=[pltpu.VMEM((g, GDN_DK, GDN_DV), F32)],
        compiler_params=_cparams("parallel", "arbitrary"),
        name="gdn_delta",
    )(qkv, qkv, qkv, z, a_col, b_col, a_row, al_c, dt_c, al_r, dt_r, norm_w)


def _branch_kernel(om_ref, og_ref, wm_ref, wg_ref, gm_ref, gg_ref, bm_ref, bg_ref, o_ref):
    ym = _dot(om_ref[...], wm_ref[...])
    yg = _dot(og_ref[...], wg_ref[...])
    sm = _sigmoid(gm_ref[...].astype(F32) + bm_ref[...])
    sg = _sigmoid(gg_ref[...].astype(F32) + bg_ref[...])
    o_ref[...] = (sm * ym + sg * yg).astype(o_ref.dtype)


def _branch_merge(o_mla, o_gdn, w_o_mla, w_o_gdn, gates, b_gate):
    s = o_mla.shape[0]
    d = w_o_mla.shape[1]
    tm = min(1024, s)
    tn = min(512, d)
    nb = d // tn
    return pl.pallas_call(
        _branch_kernel,
        grid=(nb, s // tm),
        in_specs=[pl.BlockSpec((tm, o_mla.shape[1]), lambda j, i: (i, 0)),
                  pl.BlockSpec((tm, o_gdn.shape[1]), lambda j, i: (i, 0)),
                  pl.BlockSpec((w_o_mla.shape[0], tn), lambda j, i: (0, j)),
                  pl.BlockSpec((w_o_gdn.shape[0], tn), lambda j, i: (0, j)),
                  pl.BlockSpec((tm, tn), lambda j, i: (i, j)),
                  pl.BlockSpec((tm, tn), lambda j, i: (i, nb + j)),
                  pl.BlockSpec((1, tn), lambda j, i: (0, j)),
                  pl.BlockSpec((1, tn), lambda j, i: (0, nb + j))],
        out_specs=pl.BlockSpec((tm, tn), lambda j, i: (i, j)),
        out_shape=jax.ShapeDtypeStruct((s, d), BF16),
        compiler_params=_cparams("parallel", "parallel"),
        name="branch_merge",
    )(o_mla, o_gdn, w_o_mla, w_o_gdn, gates, gates, b_gate, b_gate)


def _out_ln_kernel(m_ref, w_ref, x_ref, g_ref, b_ref, o_ref):
    y = DN_ALPHA * x_ref[...] + _dot(m_ref[...], w_ref[...])
    o_ref[...] = _layernorm(y, g_ref[...], b_ref[...])


def _out_ln(mixed, w_out, x, g, b):
    s, d = x.shape
    tm = min(512, s)
    return pl.pallas_call(
        _out_ln_kernel,
        grid=(s // tm,),
        in_specs=[pl.BlockSpec((tm, d), lambda i: (i, 0)),
                  pl.BlockSpec((d, d), lambda i: (0, 0)),
                  pl.BlockSpec((tm, d), lambda i: (i, 0)),
                  pl.BlockSpec((1, d), lambda i: (0, 0)),
                  pl.BlockSpec((1, d), lambda i: (0, 0))],
        out_specs=pl.BlockSpec((tm, d), lambda i: (i, 0)),
        out_shape=jax.ShapeDtypeStruct((s, d), F32),
        compiler_params=_cparams("parallel"),
        name="out_ln1",
    )(mixed, w_out, x, g, b)


SLAB = 16


def _store_rows(ref2, val, per_token, row0=0):
    n = val.shape[0]
    for s in range(val.shape[1] // LANES):
        ref2[pl.ds(row0 + s, n, stride=per_token), :] = val[:, s * LANES:(s + 1) * LANES]


def _load_rows(ref2, n, per_token, row0, k):
    return jnp.concatenate([ref2[pl.ds(row0 + s, n, stride=per_token), :] for s in range(k)], axis=1)


def _mem_kernel(x_ref, wq_ref, kv_ref, wo_ref, g_ref, b_ref, wr_ref, br_ref, x2_ref, x2r_ref, lg_ref):
    x1 = x_ref[...]
    hd = MEM_HEAD_DIM
    nh = MEM_HEADS
    q = (_dot(x1.astype(BF16), wq_ref[...]) * float(hd ** -0.5)).astype(BF16)
    outs = []
    for h in range(nh):
        kh = kv_ref[:, h * hd:(h + 1) * hd]
        vh = kv_ref[:, (nh + h) * hd:(nh + h + 1) * hd]
        s = _dot_nt(q[:, h * hd:(h + 1) * hd], kh)
        p = jnp.exp(s - jnp.max(s, axis=-1, keepdims=True))
        o = _dot(p.astype(BF16), vh) / jnp.sum(p, axis=-1, keepdims=True)
        outs.append(o.astype(BF16))
    o = jnp.concatenate(outs, axis=1)
    y = DN_ALPHA * x1 + _dot(o, wo_ref[...])
    x2 = _layernorm(y, g_ref[...], b_ref[...])
    x2_ref[...] = x2
    _store_rows(x2r_ref, x2, SLAB)
    x_hi = x2.astype(BF16)
    x_lo = (x2 - x_hi.astype(F32)).astype(BF16)
    w = wr_ref[...]
    w_hi = w.astype(BF16)
    w_lo = (w - w_hi.astype(F32)).astype(BF16)
    lg_ref[...] = (_dot(x_hi, w_hi) + _dot(x_lo, w_hi)) + _dot(x_hi, w_lo) + br_ref[...]


def _mem_attn_ln(x1, w_mq, kvm, w_mo, g, b, w_r, b_r):
    s, d = x1.shape
    tm = min(512, s)
    return pl.pallas_call(
        _mem_kernel,
        grid=(s // tm,),
        in_specs=[pl.BlockSpec((tm, d), lambda i: (i, 0)),
                  pl.BlockSpec(w_mq.shape, lambda i: (0, 0)),
                  pl.BlockSpec(kvm.shape, lambda i: (0, 0)),
                  pl.BlockSpec(w_mo.shape, lambda i: (0, 0)),
                  pl.BlockSpec((1, d), lambda i: (0, 0)),
                  pl.BlockSpec((1, d), lambda i: (0, 0)),
                  pl.BlockSpec(w_r.shape, lambda i: (0, 0)),
                  pl.BlockSpec((1, LANES), lambda i: (0, 0))],
        out_specs=[pl.BlockSpec((tm, d), lambda i: (i, 0)),
                   pl.BlockSpec((tm * SLAB, LANES), lambda i: (i, 0)),
                   pl.BlockSpec((tm, LANES), lambda i: (i, 0))],
        out_shape=[jax.ShapeDtypeStruct((s, d), F32),
                   jax.ShapeDtypeStruct((s * SLAB, LANES), F32),
                   jax.ShapeDtypeStruct((s, LANES), F32)],
        compiler_params=_cparams("parallel"),
        name="mem_attn_ln2",
    )(x1, w_mq, kvm, w_mo, g, b, w_r, b_r)


SEL_E1, SEL_E2, SEL_R1, SEL_R2, SEL_G1, SEL_G2 = range(6)
GRP_LANE0 = N_EXPERTS


def _route_kernel(lg_ref, sel_ref, cnt_ref, carry):
    i = pl.program_id(0)

    @pl.when(i == 0)
    def _():
        carry[...] = jnp.zeros(carry.shape, F32)

    lg = lg_ref[...]
    tm = lg.shape[0]
    lane = lax.broadcasted_iota(jnp.int32, lg.shape, 1)
    big = jnp.int32(4 * LANES)
    neg = jnp.float32(-jnp.inf)

    def first_max(vals):
        mx = jnp.max(vals, axis=-1, keepdims=True)
        idx = jnp.min(jnp.where(vals == mx, lane, big), axis=-1, keepdims=True)
        return mx, idx

    is_grp = (lane >= GRP_LANE0) & (lane < GRP_LANE0 + N_GROUPS)
    gl = jnp.where(is_grp, lg, neg)
    gmax, gidx = first_max(gl)
    p_top = 1.0 / jnp.sum(jnp.where(is_grp, jnp.exp(gl - gmax), 0.0), axis=-1, keepdims=True)
    lo = (gidx - GRP_LANE0) * EXPERTS_PER_GROUP
    in_grp = (lane >= lo) & (lane < lo + EXPERTS_PER_GROUP)
    el = jnp.where(in_grp, lg, neg)
    m1, i1 = first_max(el)
    m2, i2 = first_max(jnp.where(lane == i1, neg, el))
    r = jnp.exp(m2 - m1)
    g1 = p_top / (1.0 + r)
    g2 = p_top * r / (1.0 + r)

    hot1 = lane == i1
    hot2 = lane == i2
    onehot = jnp.where(hot1, 1.0, 0.0) + jnp.where(hot2, 1.0, 0.0)
    ri = lax.broadcasted_iota(jnp.int32, (tm, tm), 0)
    ci = lax.broadcasted_iota(jnp.int32, (tm, tm), 1)
    before = jnp.where(ci < ri, 1.0, 0.0).astype(BF16)
    rank = _dot(before, onehot.astype(BF16)) + carry[...]
    r1 = jnp.sum(jnp.where(hot1, rank, 0.0), axis=-1, keepdims=True)
    r2 = jnp.sum(jnp.where(hot2, rank, 0.0), axis=-1, keepdims=True)
    carry[...] = carry[...] + jnp.sum(onehot, axis=0, keepdims=True)
    cnt_ref[...] = carry[...]

    out = jnp.zeros(lg.shape, F32)
    for ln, val in ((SEL_E1, i1.astype(F32)), (SEL_E2, i2.astype(F32)), (SEL_R1, r1), (SEL_R2, r2),
                    (SEL_G1, g1), (SEL_G2, g2)):
        out = jnp.where(lane == ln, val, out)
    sel_ref[...] = out


def _route(logits):
    t = logits.shape[0]
    tm = min(512, t)
    return pl.pallas_call(
        _route_kernel,
        grid=(t // tm,),
        in_specs=[pl.BlockSpec((tm, LANES), lambda i: (i, 0))],
        out_specs=[pl.BlockSpec((tm, LANES), lambda i: (i, 0)),
                   pl.BlockSpec((1, LANES), lambda i: (0, 0))],
        out_shape=[jax.ShapeDtypeStruct((t, LANES), F32),
                   jax.ShapeDtypeStruct((1, LANES), F32)],
        scratch_shapes=[pltpu.VMEM((1, LANES), F32)],
        compiler_params=_cparams("arbitrary"),
        name="moe_route",
    )(logits)


def _dispatch_kernel(cnt_ref, e1_ref, e2_ref, r1_ref, r2_ref, rowa_ref, blk_ref, rows_ref, d1_ref, d2_ref, pstart,
                     used_ref, *, t, n_blocks):
    def seg(e, start):
        pstart[e] = start
        c = cnt_ref[e]
        nb = (c + MOE_ROWS - 1) // MOE_ROWS
        end = start + nb * MOE_ROWS

        def mark(p, carry):
            rowa_ref[p] = -1
            return carry
        lax.fori_loop(start + c, end, mark, 0)

        def blk(bi, carry):
            blk_ref[bi] = e
            rows_ref[bi] = jnp.minimum(start + c - bi * MOE_ROWS, MOE_ROWS)
            return carry
        lax.fori_loop(start // MOE_ROWS, end // MOE_ROWS, blk, 0)
        return end

    total = lax.fori_loop(0, N_EXPERTS, seg, 0)
    used = total // MOE_ROWS
    used_ref[0] = used
    last = blk_ref[jnp.maximum(used - 1, 0)]

    def tail(bi, carry):
        blk_ref[bi] = last
        rows_ref[bi] = 0

        def unused(p, c2):
            rowa_ref[bi * MOE_ROWS + p] = -1
            return c2
        lax.fori_loop(0, MOE_ROWS, unused, 0, unroll=16)
        return carry
    lax.fori_loop(used, n_blocks, tail, 0)

    def place(tok, carry):
        p1 = pstart[e1_ref[tok]] + r1_ref[tok]
        p2 = pstart[e2_ref[tok]] + r2_ref[tok]
        rowa_ref[p1] = 2 * tok
        rowa_ref[p2] = 2 * tok + 1
        d1_ref[tok] = p1
        d2_ref[tok] = p2
        return carry
    lax.fori_loop(0, t, place, 0, unroll=8)


def _dispatch(cnt, e1, e2, r1, r2, n_blocks):
    t = e1.shape[0]
    smem = pl.BlockSpec(memory_space=pltpu.SMEM)
    i32 = lambda n: jax.ShapeDtypeStruct((n,), jnp.int32)
    return pl.pallas_call(
        functools.partial(_dispatch_kernel, t=t, n_blocks=n_blocks),
        in_specs=[smem] * 5,
        out_specs=[smem] * 7,
        out_shape=[i32(n_blocks * MOE_ROWS), i32(n_blocks), i32(n_blocks), i32(t), i32(t), i32(N_EXPERTS), i32(1)],
        name="moe_dispatch",
    )(cnt, e1, e2, r1, r2)


def _scatter_in_kernel(d1_ref, d2_ref, cnt_ref, pstart_ref, used_ref, x_ref, xs_hbm, zbuf, sem, zsem, *, tm, n_blocks):
    i = pl.program_id(0)
    blk_rows = MOE_ROWS * SLAB

    def slab(j):
        return pl.ds(pl.multiple_of(j * SLAB, SLAB), SLAB)

    def block(bi):
        return pl.ds(pl.multiple_of(bi * blk_rows, blk_rows), blk_rows)

    def pad_range(e):
        c = cnt_ref[e]
        lo = pstart_ref[e] + c
        return lo, pstart_ref[e] + (c + MOE_ROWS - 1) // MOE_ROWS * MOE_ROWS

    @pl.when(i == 0)
    def _():
        zbuf[...] = jnp.zeros(zbuf.shape, F32)

        def zero_pads(e, carry):
            lo, hi = pad_range(e)

            def z(p, c2):
                pltpu.make_async_copy(zbuf.at[slab(0)], xs_hbm.at[slab(p)], zsem).start()
                return c2
            lax.fori_loop(lo, hi, z, 0)
            return carry
        lax.fori_loop(0, N_EXPERTS, zero_pads, 0)

        def zero_block(bi, carry):
            pltpu.make_async_copy(zbuf, xs_hbm.at[block(bi)], zsem).start()
            return carry
        lax.fori_loop(used_ref[0], n_blocks, zero_block, 0)

    def scatter(r, carry):
        tok = i * tm + r
        pltpu.make_async_copy(x_ref.at[slab(r)], xs_hbm.at[slab(d1_ref[tok])], sem).start()
        pltpu.make_async_copy(x_ref.at[slab(r)], xs_hbm.at[slab(d2_ref[tok])], sem).start()
        return carry
    lax.fori_loop(0, tm, scatter, 0, unroll=8)

    for _ in range(2):
        pltpu.make_async_copy(x_ref, xs_hbm.at[pl.ds(0, tm * SLAB)], sem).wait()

    @pl.when(i == pl.num_programs(0) - 1)
    def _():
        def wait_pads(e, carry):
            lo, hi = pad_range(e)
            n = hi - lo

            @pl.when(n > 0)
            def _():
                rows = pl.ds(0, pl.multiple_of(n * SLAB, SLAB))
                pltpu.make_async_copy(zbuf.at[rows], xs_hbm.at[rows], zsem).wait()
            return carry
        lax.fori_loop(0, N_EXPERTS, wait_pads, 0)

        def wait_block(bi, carry):
            pltpu.make_async_copy(zbuf, xs_hbm.at[block(0)], zsem).wait()
            return carry
        lax.fori_loop(used_ref[0], n_blocks, wait_block, 0)


def _scatter_in(x2r, d1, d2, cnt, pstart, used, n_blocks):
    t = x2r.shape[0] // SLAB
    tm = min(MOE_ROWS, t)
    grid_spec = pltpu.PrefetchScalarGridSpec(
        num_scalar_prefetch=5,
        grid=(t // tm,),
        in_specs=[pl.BlockSpec((tm * SLAB, LANES), lambda i, *_: (i, 0))],
        out_specs=pl.BlockSpec(memory_space=pl.ANY),
        scratch_shapes=[pltpu.VMEM((MOE_ROWS * SLAB, LANES), F32),
                        pltpu.SemaphoreType.DMA(()), pltpu.SemaphoreType.DMA(())],
    )
    return pl.pallas_call(
        functools.partial(_scatter_in_kernel, tm=tm, n_blocks=n_blocks),
        grid_spec=grid_spec,
        out_shape=jax.ShapeDtypeStruct((n_blocks * MOE_ROWS * SLAB, LANES), F32),
        compiler_params=_cparams("arbitrary"),
        name="moe_scatter_in",
    )(d1, d2, cnt, pstart, used, x2r)


def _expert_kernel(blk_ref, rowa_ref, rows_ref, used_ref, x_ref, wg_ref, wu_ref, wd_ref, out_hbm, ybuf, ssem):
    b = pl.program_id(0)
    n_blocks = pl.num_programs(0)
    slot = b % 2

    def slab(i):
        return pl.ds(pl.multiple_of(i * SLAB, SLAB), SLAB)

    def scatter_copy(blk_slot, r, dst):
        return pltpu.make_async_copy(ybuf.at[blk_slot, slab(r)], out_hbm.at[slab(dst)], ssem.at[blk_slot])

    def wait_scatters(blk, blk_slot):
        n = rows_ref[blk]

        @pl.when(n > 0)
        def _():
            rows = pl.ds(0, pl.multiple_of(n * SLAB, SLAB))
            pltpu.make_async_copy(ybuf.at[blk_slot, rows], out_hbm.at[rows], ssem.at[blk_slot]).wait()

    @pl.when(b >= 2)
    def _():
        wait_scatters(jnp.maximum(b - 2, 0), slot)

    n_rows = rows_ref[b]

    @pl.when(n_rows > 0)
    def _():
        xb = _load_rows(x_ref, MOE_ROWS, SLAB, 0, SLAB).astype(BF16)
        hg = _dot(xb, wg_ref[...].astype(BF16))
        hu = _dot(xb, wu_ref[...].astype(BF16))
        hb = (hg * _sigmoid(hg) * hu).astype(BF16)
        _store_rows(ybuf.at[slot], _dot(hb, wd_ref[...].astype(BF16)), SLAB)

        def issue_scatter(r, carry):
            scatter_copy(slot, r, rowa_ref[b * MOE_ROWS + r]).start()
            return carry
        lax.fori_loop(0, n_rows, issue_scatter, 0)

    @pl.when(b == n_blocks - 1)
    def _():
        @pl.when(b >= 1)
        def _():
            wait_scatters(jnp.maximum(b - 1, 0), 1 - slot)
        wait_scatters(b, slot)


def _experts(xs, t, row_a, blk_exp, blk_rows, used, w_gate_e, w_up_e, w_down_e):
    d = SLAB * LANES
    n_blocks = blk_exp.shape[0]
    de = w_gate_e.shape[2]
    weights = lambda b, blk, rowa, rows, nu: (blk[b], 0, 0)
    grid_spec = pltpu.PrefetchScalarGridSpec(
        num_scalar_prefetch=4,
        grid=(n_blocks,),
        in_specs=[pl.BlockSpec((MOE_ROWS * SLAB, LANES), lambda b, blk, rowa, rows, nu: (jnp.minimum(b, nu[0] - 1), 0)),
                  pl.BlockSpec((None, d, de), weights),
                  pl.BlockSpec((None, d, de), weights),
                  pl.BlockSpec((None, de, d), weights)],
        out_specs=pl.BlockSpec(memory_space=pl.ANY),
        scratch_shapes=[pltpu.VMEM((2, MOE_ROWS * SLAB, LANES), F32), pltpu.SemaphoreType.DMA((2,))],
    )
    return pl.pallas_call(
        _expert_kernel,
        grid_spec=grid_spec,
        out_shape=jax.ShapeDtypeStruct((2 * t * SLAB, LANES), F32),
        compiler_params=_cparams("arbitrary"),
        name="moe_experts",
    )(blk_exp, row_a, blk_rows, used, xs, w_gate_e, w_up_e, w_down_e)


def _combine_kernel(x_ref, y_ref, sel_ref, g_ref, b_ref, o_ref):
    tm = x_ref.shape[0]
    sel = sel_ref[...]
    g1 = sel[:, SEL_G1:SEL_G1 + 1]
    g2 = sel[:, SEL_G2:SEL_G2 + 1]
    y1 = _load_rows(y_ref, tm, 2 * SLAB, 0, SLAB)
    y2 = _load_rows(y_ref, tm, 2 * SLAB, SLAB, SLAB)
    y = DN_ALPHA * x_ref[...] + (g1 * y1 + g2 * y2)
    o_ref[...] = _layernorm(y, g_ref[...], b_ref[...])


def _combine_ln(x2, ys, sel, g, b):
    t, d = x2.shape
    tm = min(512, t)
    return pl.pallas_call(
        _combine_kernel,
        grid=(t // tm,),
        in_specs=[pl.BlockSpec((tm, d), lambda i: (i, 0)),
                  pl.BlockSpec((tm * 2 * SLAB, LANES), lambda i: (i, 0)),
                  pl.BlockSpec((tm, LANES), lambda i: (i, 0)),
                  pl.BlockSpec((1, d), lambda i: (0, 0)),
                  pl.BlockSpec((1, d), lambda i: (0, 0))],
        out_specs=pl.BlockSpec((tm, d), lambda i: (i, 0)),
        out_shape=jax.ShapeDtypeStruct((t, d), F32),
        compiler_params=_cparams("parallel"),
        name="combine_ln3",
    )(x2, ys, sel, g, b)


def _mixer(xb, positions, w_in, b_gate, mla_q_norm, w_uq, mla_kv_norm, w_ukv, w_o_mla,
           gdn_conv, gdn_a_log, gdn_dt_bias, gdn_norm, w_o_gdn):
    s, d = xb.shape
    nqkv = 2 * GDN_QK_HEADS * GDN_DK + GDN_V_HEADS * GDN_DV
    nz = GDN_V_HEADS * GDN_DV
    o0 = 0
    o1 = o0 + MLA_Q_LORA
    o2 = o1 + MLA_KV_LORA + MLA_ROPE
    o3 = o2 + nqkv
    o4 = o3 + nz
    o5 = o4 + GDN_V_HEADS
    o6 = o5 + GDN_V_HEADS

    def rot_cols(w):
        half = MLA_ROPE // 2
        return jnp.concatenate([-w[..., half:], w[..., :half]], axis=-1)

    w_q = w_in[:, o0:o1].astype(BF16)
    w_kpe = w_in[:, o1 + MLA_KV_LORA:o2]
    w_kv = jnp.concatenate([w_in[:, o1:o1 + MLA_KV_LORA], w_kpe, rot_cols(w_kpe)], axis=1).astype(BF16)
    w_ba = jnp.concatenate([w_in[:, o4:o6], jnp.zeros((d, LANES - 2 * GDN_V_HEADS), F32)], axis=1).astype(BF16)
    qd = _matmul(xb, w_q, BF16, 1024, MLA_Q_LORA)
    kvd = _matmul(xb, w_kv, BF16, 1024, w_kv.shape[1])
    qkv_c = _qkv_conv_proj(xb, w_in[:, o2:o3].astype(BF16), gdn_conv)
    hz = _matmul(xb, w_in[:, o3:o4].astype(BF16), BF16, 1024, 2048)
    hba = _matmul(xb, w_ba, F32, 512, LANES)
    hgate = _matmul(xb, w_in[:, o6:].astype(BF16), BF16, 1024, 2048)

    cs = _rope_table(positions)
    hq = MLA_NOPE + MLA_ROPE
    wq3 = w_uq.reshape(MLA_Q_LORA, MLA_HEADS, hq)
    pe = wq3[..., MLA_NOPE:]
    wq = jnp.concatenate([pe, rot_cols(pe), wq3[..., :MLA_NOPE]], axis=-1)
    wq = wq.reshape(MLA_Q_LORA, MLA_HEADS * MLA_QK_PAD).astype(BF16)
    wkv3 = w_ukv.reshape(MLA_KV_LORA, MLA_HEADS, MLA_NOPE + MLA_V)
    wk = wkv3[..., :MLA_NOPE].reshape(MLA_KV_LORA, MLA_HEADS * MLA_NOPE).astype(BF16)
    wv = wkv3[..., MLA_NOPE:].reshape(MLA_KV_LORA, MLA_HEADS * MLA_V).astype(BF16)
    q = _mla_q_proj(qd, mla_q_norm.reshape(1, -1), wq, cs)
    k, v = _mla_kv_proj(kvd, mla_kv_norm.reshape(1, -1), wk, wv, cs)
    o_mla = _mla_attention(q, k, v)

    o_gdn = _gdn_delta(qkv_c, hz, hba[:, GDN_V_HEADS:2 * GDN_V_HEADS], hba[:, :GDN_V_HEADS],
                       gdn_a_log, gdn_dt_bias, gdn_norm.reshape(1, -1))

    return _branch_merge(o_mla, o_gdn, w_o_mla.astype(BF16), w_o_gdn.astype(BF16), hgate, b_gate.reshape(1, -1))


def _moe(x2, x2r, logits, w_gate_e, w_up_e, w_down_e, ln_g, ln_b):
    t, d = x2.shape
    n_blocks = (2 * t) // MOE_ROWS + N_EXPERTS
    sel, cnt = _route(logits)
    as_i32 = lambda col: sel[:, col].astype(jnp.int32)
    cnt_i = cnt[0, :N_EXPERTS].astype(jnp.int32)
    row_a, blk_exp, blk_rows, d1, d2, pstart, used = _dispatch(cnt_i, as_i32(SEL_E1), as_i32(SEL_E2),
                                                               as_i32(SEL_R1), as_i32(SEL_R2), n_blocks)
    xs = _scatter_in(x2r, d1, d2, cnt_i, pstart, used, n_blocks)
    ys = _experts(xs, t, row_a, blk_exp, blk_rows, used, w_gate_e, w_up_e, w_down_e)
    return _combine_ln(x2, ys, sel, ln_g, ln_b)


def _layer(x, mem, positions, w_in, b_gate, mla_q_norm, w_uq, mla_kv_norm, w_ukv, w_o_mla,
           gdn_conv, gdn_a_log, gdn_dt_bias, gdn_norm, w_o_gdn, w_out, ln1_g, ln1_b,
           w_mq, w_mkv, w_mo, ln2_g, ln2_b, w_route_grp, b_route_grp, w_route_exp, b_route_exp,
           w_gate_e, w_up_e, w_down_e, ln3_g, ln3_b):
    d = x.shape[1]
    row = lambda p: p.reshape(1, -1)
    mixed = _mixer(x.astype(BF16), positions, w_in, b_gate, mla_q_norm, w_uq, mla_kv_norm, w_ukv, w_o_mla,
                   gdn_conv, gdn_a_log, gdn_dt_bias, gdn_norm, w_o_gdn)
    x1 = _out_ln(mixed, w_out.astype(BF16), x, row(ln1_g), row(ln1_b))

    kvm = _matmul(mem.astype(BF16), w_mkv.astype(BF16), BF16, 256, 512)
    pad = LANES - N_EXPERTS - N_GROUPS
    w_r = jnp.concatenate([w_route_exp, w_route_grp, jnp.zeros((d, pad), F32)], axis=1)
    b_r = jnp.concatenate([b_route_exp, b_route_grp, jnp.zeros((pad,), F32)]).reshape(1, LANES)
    x2, x2r, logits = _mem_attn_ln(x1, w_mq.astype(BF16), kvm, w_mo.astype(BF16), row(ln2_g), row(ln2_b), w_r, b_r)

    return _moe(x2, x2r, logits, w_gate_e, w_up_e, w_down_e, row(ln3_g), row(ln3_b))


def kernel(x, mem, positions, w_in, b_gate, mla_q_norm, w_uq, mla_kv_norm, w_ukv, w_o_mla, gdn_conv, gdn_a_log,
           gdn_dt_bias, gdn_norm, w_o_gdn, w_out, ln1_g, ln1_b, w_mq, w_mkv, w_mo, ln2_g, ln2_b, w_route_grp,
           b_route_grp, w_route_exp, b_route_exp, w_gate_e, w_up_e, w_down_e, ln3_g, ln3_b):
    outs = []
    for bi in range(x.shape[0]):
        h = x[bi]
        for l in range(w_in.shape[0]):
            h = _layer(h, mem[bi], positions[bi], w_in[l], b_gate[l], mla_q_norm[l], w_uq[l], mla_kv_norm[l],
                       w_ukv[l], w_o_mla[l], gdn_conv[l], gdn_a_log[l], gdn_dt_bias[l], gdn_norm[l], w_o_gdn[l],
                       w_out[l], ln1_g[l], ln1_b[l], w_mq[l], w_mkv[l], w_mo[l], ln2_g[l], ln2_b[l],
                       w_route_grp[l], b_route_grp[l], w_route_exp[l], b_route_exp[l], w_gate_e[l], w_up_e[l],
                       w_down_e[l], ln3_g[l], ln3_b[l])
        outs.append(h)
    return jnp.stack(outs, axis=0)
```

```python
import functools

import numpy as np
import jax
import jax.numpy as jnp
from jax import lax
from jax.experimental import pallas as pl
from jax.experimental.pallas import tpu as pltpu

F32 = jnp.float32
BF16 = jnp.bfloat16
HIGHEST = lax.Precision.HIGHEST

LANES = 128
VMEM_LIMIT = 56 * 1024 * 1024

CHUNK = 64
MLA_HEADS = 16
MLA_Q_LORA = 768
MLA_KV_LORA = 512
MLA_NOPE = 128
MLA_ROPE = 64
MLA_V = 128
MLA_QK_PAD = 256
ROPE_THETA = 10000.0
GDN_QK_HEADS = 16
GDN_V_HEADS = 32
GDN_DK = 128
GDN_DV = 128
GDN_CONV = 4
GDN_GROUP = 32
GDN_STEP_CHUNKS = 2
MEM_HEADS = 4
MEM_HEAD_DIM = 128
N_GROUPS = 8
EXPERTS_PER_GROUP = 8
N_EXPERTS = 64
D_EXPERT = 512
MOE_ROWS = 256
RMS_EPS = 1e-6
LN_EPS = 1e-5
DN_ALPHA = 2.0 ** 0.25


def _cparams(*sem):
    return pltpu.CompilerParams(dimension_semantics=sem, vmem_limit_bytes=VMEM_LIMIT)


def _dot(a, b, **kw):
    return jnp.dot(a, b, preferred_element_type=F32, **kw)


def _dot_nt(a, b):
    return lax.dot_general(a, b, (((1,), (1,)), ((), ())), preferred_element_type=F32)


def _dot_tn(a, b):
    return lax.dot_general(a, b, (((0,), (0,)), ((), ())), preferred_element_type=F32)


def _sigmoid(x):
    return 1.0 / (1.0 + jnp.exp(-x))


def _layernorm(y, g, b):
    mu = jnp.mean(y, axis=-1, keepdims=True)
    d = y - mu
    var = jnp.mean(d * d, axis=-1, keepdims=True)
    return d * lax.rsqrt(var + LN_EPS) * g + b


def _mm_kernel(x_ref, w_ref, o_ref):
    o_ref[...] = _dot(x_ref[...], w_ref[...]).astype(o_ref.dtype)


def _matmul(x, w, out_dtype, tm, tn):
    m, k = x.shape
    n = w.shape[1]
    tm, tn = min(tm, m), min(tn, n)
    return pl.pallas_call(
        _mm_kernel,
        grid=(n // tn, m // tm),
        in_specs=[pl.BlockSpec((tm, k), lambda j, i: (i, 0)),
                  pl.BlockSpec((k, tn), lambda j, i: (0, j))],
        out_specs=pl.BlockSpec((tm, tn), lambda j, i: (i, j)),
        out_shape=jax.ShapeDtypeStruct((m, n), out_dtype),
        compiler_params=_cparams("parallel", "parallel"),
        name="matmul",
    )(x, w)


def _rope_table_kernel(pos_ref, inv_ref, o_ref):
    ang = pos_ref[...].astype(F32) * inv_ref[...]
    lane = lax.broadcasted_iota(jnp.int32, ang.shape, 1)
    o_ref[...] = jnp.where(lane < MLA_ROPE, jnp.cos(ang), jnp.sin(ang))


def _rope_table(positions):
    s = positions.shape[0]
    inv = 1.0 / (ROPE_THETA ** (np.arange(0, MLA_ROPE, 2, dtype=np.float32) / MLA_ROPE))
    inv4 = jnp.asarray(np.tile(inv.astype(np.float32), 4)[None, :])
    tm = min(512, s)
    return pl.pallas_call(
        _rope_table_kernel,
        grid=(s // tm,),
        in_specs=[pl.BlockSpec((tm, 1), lambda i: (i, 0)),
                  pl.BlockSpec((1, LANES), lambda i: (0, 0))],
        out_specs=pl.BlockSpec((tm, LANES), lambda i: (i, 0)),
        out_shape=jax.ShapeDtypeStruct((s, LANES), F32),
        compiler_params=_cparams("parallel"),
        name="rope_table",
    )(positions.reshape(s, 1), inv4)


def _rope_pair(t, cs):
    a = t * cs
    return a + pltpu.roll(a, MLA_ROPE, axis=1)


def _mla_q_kernel(qd_ref, qn_ref, w_ref, cs_ref, o_ref, *, scale):
    x = qd_ref[...].astype(F32)
    cq = x * lax.rsqrt(jnp.mean(x * x, axis=-1, keepdims=True) + RMS_EPS) * qn_ref[...]
    cqb = cq.astype(BF16)
    cs = cs_ref[...]
    for h in range(MLA_HEADS):
        lo = h * MLA_QK_PAD
        p = _dot(cqb, w_ref[:, lo:lo + MLA_QK_PAD])
        o_ref[:, lo:lo + LANES] = (_rope_pair(p[:, :LANES], cs) * scale).astype(o_ref.dtype)
        o_ref[:, lo + LANES:lo + MLA_QK_PAD] = (p[:, LANES:] * scale).astype(o_ref.dtype)


def _mla_q_proj(qd, q_norm, wq, cs):
    s = qd.shape[0]
    tm = min(512, s)
    n = MLA_HEADS * MLA_QK_PAD
    scale = float((MLA_NOPE + MLA_ROPE) ** -0.5 * np.log2(np.e))
    return pl.pallas_call(
        functools.partial(_mla_q_kernel, scale=scale),
        grid=(s // tm,),
        in_specs=[pl.BlockSpec((tm, MLA_Q_LORA), lambda i: (i, 0)),
                  pl.BlockSpec((1, MLA_Q_LORA), lambda i: (0, 0)),
                  pl.BlockSpec((MLA_Q_LORA, n), lambda i: (0, 0)),
                  pl.BlockSpec((tm, LANES), lambda i: (i, 0))],
        out_specs=pl.BlockSpec((tm, n), lambda i: (i, 0)),
        out_shape=jax.ShapeDtypeStruct((s, n), BF16),
        compiler_params=_cparams("parallel"),
        name="mla_q_proj",
    )(qd, q_norm, wq, cs)


def _mla_kv_kernel(kv_ref, kn_ref, wk_ref, wv_ref, cs_ref, k_ref, v_ref):
    x = kv_ref[:, :MLA_KV_LORA].astype(F32)
    ckv = x * lax.rsqrt(jnp.mean(x * x, axis=-1, keepdims=True) + RMS_EPS) * kn_ref[...]
    cb = ckv.astype(BF16)
    pe = _rope_pair(kv_ref[:, MLA_KV_LORA:].astype(F32), cs_ref[...])
    lane = lax.broadcasted_iota(jnp.int32, pe.shape, 1)
    pe = jnp.where(lane < MLA_ROPE, pe, 0.0).astype(k_ref.dtype)
    kn = _dot(cb, wk_ref[...]).astype(k_ref.dtype)
    for h in range(MLA_HEADS):
        lo = h * MLA_QK_PAD
        k_ref[:, lo:lo + LANES] = pe
        k_ref[:, lo + LANES:lo + MLA_QK_PAD] = kn[:, h * MLA_NOPE:(h + 1) * MLA_NOPE]
    v_ref[...] = _dot(cb, wv_ref[...]).astype(v_ref.dtype)


def _mla_kv_proj(kvd, kv_norm, wk, wv, cs):
    s, w = kvd.shape
    tm = min(512, s)
    nk = MLA_HEADS * MLA_QK_PAD
    nv = MLA_HEADS * MLA_V
    return pl.pallas_call(
        _mla_kv_kernel,
        grid=(s // tm,),
        in_specs=[pl.BlockSpec((tm, w), lambda i: (i, 0)),
                  pl.BlockSpec((1, MLA_KV_LORA), lambda i: (0, 0)),
                  pl.BlockSpec(wk.shape, lambda i: (0, 0)),
                  pl.BlockSpec(wv.shape, lambda i: (0, 0)),
                  pl.BlockSpec((tm, LANES), lambda i: (i, 0))],
        out_specs=[pl.BlockSpec((tm, nk), lambda i: (i, 0)),
                   pl.BlockSpec((tm, nv), lambda i: (i, 0))],
        out_shape=[jax.ShapeDtypeStruct((s, nk), BF16),
                   jax.ShapeDtypeStruct((s, nv), BF16)],
        compiler_params=_cparams("parallel"),
        name="mla_kv_proj",
    )(kvd, kv_norm, wk, wv, cs)


FLASH_HEADS = 4
FLASH_TILE = 1024


def _lane_repeat(x, n):
    return jnp.concatenate([x] * n, axis=1)


def _flash_kernel(qi_ref, kj_ref, q_ref, k_ref, v_ref, o_ref, m_sc, acc_sc):
    t = pl.program_id(1)
    i = qi_ref[t]
    j = kj_ref[t]
    hps = m_sc.shape[0]
    tk = k_ref.shape[0]
    heads = range(hps)

    @pl.when(j == 0)
    def _():
        m_sc[...] = jnp.full(m_sc.shape, -1e30, F32)
        acc_sc[...] = jnp.zeros(acc_sc.shape, F32)

    def update(r0, nr, nk, masked):
        rows = slice(r0, r0 + nr)
        ss = [_dot_nt(q_ref[rows, h * MLA_QK_PAD:(h + 1) * MLA_QK_PAD], k_ref[0:nk, h * MLA_QK_PAD:(h + 1) * MLA_QK_PAD])
              for h in heads]
        if masked:
            r = (lax.broadcasted_iota(jnp.int32, ss[0].shape, 0) + r0) // CHUNK
            c = lax.broadcasted_iota(jnp.int32, ss[0].shape, 1) // CHUNK
            keep = c <= r
            ss = [jnp.where(keep, s, -1e30) for s in ss]
        m_prev = [m_sc[h, rows] for h in heads]
        m_new = [jnp.maximum(mp, jnp.max(s, axis=-1, keepdims=True)) for mp, s in zip(m_prev, ss)]
        alpha = [jnp.exp2(mp - mn) for mp, mn in zip(m_prev, m_new)]
        ps = [jnp.exp2(s - _lane_repeat(mn, nk // LANES)).astype(BF16) for s, mn in zip(ss, m_new)]
        ones = jnp.ones((nk, MLA_V), BF16)
        pv = [_dot(p, jnp.concatenate([v_ref[0:nk, h * MLA_V:(h + 1) * MLA_V], ones], axis=1))
              for h, p in zip(heads, ps)]
        for h in heads:
            acc_sc[h, rows] = _lane_repeat(alpha[h], 2) * acc_sc[h, rows] + pv[h]
            m_sc[h, rows] = m_new[h]

    @pl.when(j < i)
    def _():
        update(0, tk, tk, False)

    @pl.when(j == i)
    def _():
        half = tk // 2
        if half % CHUNK == 0 and half % LANES == 0:
            update(0, half, half, True)
            update(half, half, tk, True)
        else:
            update(0, tk, tk, True)
        for h in heads:
            acc = acc_sc[h]
            o_ref[:, h * MLA_V:(h + 1) * MLA_V] = (acc[:, :MLA_V] / acc[:, MLA_V:]).astype(o_ref.dtype)


def _mla_attention(q, k, v):
    s = q.shape[0]
    t = min(FLASH_TILE, s)
    n = s // t
    hps = FLASH_HEADS
    qi = np.array([i for i in range(n) for _ in range(i + 1)], np.int32)
    kj = np.array([j for i in range(n) for j in range(i + 1)], np.int32)
    grid_spec = pltpu.PrefetchScalarGridSpec(
        num_scalar_prefetch=2,
        grid=(MLA_HEADS // hps, qi.shape[0]),
        in_specs=[pl.BlockSpec((t, hps * MLA_QK_PAD), lambda h, st, qi_r, kj_r: (qi_r[st], h)),
                  pl.BlockSpec((t, hps * MLA_QK_PAD), lambda h, st, qi_r, kj_r: (kj_r[st], h)),
                  pl.BlockSpec((t, hps * MLA_V), lambda h, st, qi_r, kj_r: (kj_r[st], h))],
        out_specs=pl.BlockSpec((t, hps * MLA_V), lambda h, st, qi_r, kj_r: (qi_r[st], h)),
        scratch_shapes=[pltpu.VMEM((hps, t, LANES), F32), pltpu.VMEM((hps, t, 2 * MLA_V), F32)],
    )
    return pl.pallas_call(
        _flash_kernel,
        grid_spec=grid_spec,
        out_shape=jax.ShapeDtypeStruct((s, MLA_HEADS * MLA_V), BF16),
        compiler_params=_cparams("parallel", "arbitrary"),
        name="mla_flash",
    )(jnp.asarray(qi), jnp.asarray(kj), q, k, v)


CONV_HALO = 8


def _qkv_conv_kernel(x_ref, w_ref, cw_ref, o_ref, buf, *, tm, tn, n_qk_blocks):
    c = pl.program_id(0)
    i = pl.program_id(1)
    halo = CONV_HALO

    @pl.when(i == 0)
    def _():
        buf[0:halo, :] = jnp.zeros((halo, tn), F32)

    h = _dot(x_ref[...], w_ref[...])
    buf[halo:halo + tm, :] = h
    ext = buf[...]
    y = cw_ref[GDN_CONV - 1:GDN_CONV, :] * h
    for k in range(1, GDN_CONV):
        y = y + cw_ref[GDN_CONV - 1 - k:GDN_CONV - k, :] * pltpu.roll(ext, k, axis=0)[halo:halo + tm, :]
    buf[0:halo, :] = h[tm - halo:tm, :]
    y = y * (0.5 * jnp.tanh(0.5 * y) + 0.5)
    is_qk = c < n_qk_blocks
    for g in range(tn // LANES):
        seg = y[:, g * LANES:(g + 1) * LANES]
        inv = lax.rsqrt(jnp.sum(seg * seg, axis=-1, keepdims=True) + 1e-6)
        o_ref[:, g * LANES:(g + 1) * LANES] = (seg * jnp.where(is_qk, inv, 1.0)).astype(o_ref.dtype)


def _qkv_conv_proj(xb, w_qkv, conv_w):
    s, d = xb.shape
    c = w_qkv.shape[1]
    tm = min(1024, s)
    tn = 1024
    n_qk_blocks = (2 * GDN_QK_HEADS * GDN_DK) // tn
    kern = functools.partial(_qkv_conv_kernel, tm=tm, tn=tn, n_qk_blocks=n_qk_blocks)
    return pl.pallas_call(
        kern,
        grid=(c // tn, s // tm),
        in_specs=[pl.BlockSpec((tm, d), lambda cc, i: (i, 0)),
                  pl.BlockSpec((d, tn), lambda cc, i: (0, cc)),
                  pl.BlockSpec((GDN_CONV, tn), lambda cc, i: (0, cc))],
        out_specs=pl.BlockSpec((tm, tn), lambda cc, i: (i, cc)),
        out_shape=jax.ShapeDtypeStruct((s, c), BF16),
        scratch_shapes=[pltpu.VMEM((tm + CONV_HALO, tn), F32)],
        compiler_params=_cparams("parallel", "arbitrary"),
        name="qkv_conv_proj",
    )(xb, w_qkv, conv_w)


def _softplus(x):
    return jnp.maximum(x, 0.0) + jnp.log1p(jnp.exp(-jnp.abs(x)))


def _gdn_chunk_kernel(q_ref, k_ref, v_ref, z_ref, ac_ref, bc_ref, ar_ref, alc_ref, dtc_ref, alr_ref, dtr_ref,
                      nw_ref, o_ref, state):
    n = pl.program_id(1)
    g_heads = GDN_GROUP
    c = CHUNK

    @pl.when(n == 0)
    def _():
        state[...] = jnp.zeros(state.shape, F32)

    ri = lax.broadcasted_iota(jnp.int32, (c, c), 0)
    ci = lax.broadcasted_iota(jnp.int32, (c, c), 1)
    tril = ci <= ri
    strict = ci < ri
    ltri = tril.astype(F32)
    utri = (ri <= ci).astype(F32)
    eye = (ri == ci).astype(F32)

    nsub = GDN_STEP_CHUNKS
    rows = lambda j: slice(j * c, (j + 1) * c)
    g_col = -jnp.exp(alc_ref[...]) * _softplus(ac_ref[...] + dtc_ref[...])
    beta_all = _sigmoid(bc_ref[...])
    gc_col = [_dot(ltri, g_col[rows(j)], precision=HIGHEST) for j in range(nsub)]
    gc_row = [_dot(-jnp.exp(alr_ref[...]) * _softplus(ar_ref[j] + dtr_ref[...]), utri, precision=HIGHEST)
              for j in range(nsub)]
    nw = nw_ref[...]
    scale = float(GDN_DK ** -0.5)

    vheads = range(g_heads)
    items = [(j, h) for j in range(nsub) for h in vheads]
    qitems = [(j, h) for j in range(nsub) for h in range(g_heads // 2)]
    qs = {(j, h): q_ref[rows(j), h * GDN_DK:(h + 1) * GDN_DK] for j, h in qitems}
    ks = {(j, h): k_ref[rows(j), h * GDN_DK:(h + 1) * GDN_DK] for j, h in qitems}
    kqs = {i: _dot_nt(jnp.concatenate([ks[i], qs[i]], axis=0), ks[i]) for i in qitems}
    bcol = {(j, h): beta_all[rows(j), h:h + 1] for j, h in items}
    gcol = {(j, h): gc_col[j][:, h:h + 1] for j, h in items}
    glast = {(j, h): gc_col[j][c - 1:c, h:h + 1] for j, h in items}
    decay = {(j, h): jnp.where(tril, jnp.exp(jnp.where(tril, gcol[j, h] - gc_row[j][h:h + 1, :], 0.0)), 0.0)
             for j, h in items}
    m = {(j, h): jnp.where(strict, kqs[j, h // 2][:c] * bcol[j, h] * decay[j, h], 0.0) for j, h in items}
    a_qk = {(j, h): (jnp.where(tril, kqs[j, h // 2][c:] * decay[j, h], 0.0) * scale).astype(BF16) for j, h in items}
    t_inv = {i: eye - m[i] for i in items}
    xp = {i: m[i].astype(BF16) for i in items}
    xp = {i: _dot(xp[i], xp[i]).astype(BF16) for i in items}
    for _ in range(4):
        prod = {i: _dot(jnp.concatenate([xp[i], t_inv[i].astype(BF16)], axis=0), xp[i]) for i in items}
        xp = {i: prod[i][:c].astype(BF16) for i in items}
        t_inv = {i: t_inv[i] + prod[i][c:] for i in items}
    t_inv = {i: t_inv[i] + _dot(t_inv[i].astype(BF16), xp[i]) for i in items}
    egc = {i: jnp.exp(gcol[i]) for i in items}
    kf = {i: ks[i].astype(F32) for i in qitems}
    rhs = {(j, h): jnp.concatenate([v_ref[rows(j), h * GDN_DV:(h + 1) * GDN_DV].astype(F32) * bcol[j, h],
                                    kf[j, h // 2] * (bcol[j, h] * egc[j, h])], axis=1).astype(BF16)
           for j, h in items}
    uw = {i: _dot(t_inv[i].astype(BF16), rhs[i]) for i in items}
    lhs = {(j, h): jnp.concatenate([uw[j, h][:, GDN_DV:], qs[j, h // 2].astype(F32) * egc[j, h]],
                                   axis=0).astype(BF16) for j, h in items}
    kdec = {(j, h): (kf[j, h // 2] * jnp.exp(glast[j, h] - gcol[j, h])).astype(BF16) for j, h in items}

    st = [state[h] for h in vheads]
    for j in range(nsub):
        ws = [_dot(lhs[j, h], st[h].astype(BF16)) for h in vheads]
        v_new = [(uw[j, h][:, :GDN_DV] - ws[h][:c]).astype(BF16) for h in vheads]
        o = [ws[h][c:] * scale + _dot(a_qk[j, h], v_new[h]) for h in vheads]
        st = [st[h] * jnp.exp(glast[j, h]) + _dot_tn(kdec[j, h], v_new[h]) for h in vheads]
        for h in vheads:
            on = o[h] * lax.rsqrt(jnp.mean(o[h] * o[h], axis=-1, keepdims=True) + RMS_EPS) * nw
            zz = z_ref[rows(j), h * GDN_DV:(h + 1) * GDN_DV].astype(F32)
            o_ref[rows(j), h * GDN_DV:(h + 1) * GDN_DV] = (on * (zz * _sigmoid(zz))).astype(o_ref.dtype)
    for h in vheads:
        state[h] = st[h]


def _gdn_delta(qkv, z, a, b, a_log, dt_bias, norm_w):
    s = qkv.shape[0]
    g = GDN_GROUP
    ng = GDN_V_HEADS // g
    nc = s // CHUNK
    gq = g // 2
    wq = gq * GDN_DK
    wv = g * GDN_DV
    k_off = (GDN_QK_HEADS * GDN_DK) // wq
    v_off = (2 * GDN_QK_HEADS * GDN_DK) // wv
    a_col = a.reshape(s, ng, g).transpose(1, 0, 2)
    b_col = b.reshape(s, ng, g).transpose(1, 0, 2)
    a_row = a.reshape(nc, CHUNK, ng, g).transpose(2, 0, 3, 1)
    al_c = a_log.reshape(ng, 1, g)
    dt_c = dt_bias.reshape(ng, 1, g)
    al_r = a_log.reshape(ng, g, 1)
    dt_r = dt_bias.reshape(ng, g, 1)
    nsub = GDN_STEP_CHUNKS
    rows = nsub * CHUNK
    return pl.pallas_call(
        _gdn_chunk_kernel,
        grid=(ng, nc // nsub),
        in_specs=[pl.BlockSpec((rows, wq), lambda hg, n: (n, hg)),
                  pl.BlockSpec((rows, wq), lambda hg, n: (n, k_off + hg)),
                  pl.BlockSpec((rows, wv), lambda hg, n: (n, v_off + hg)),
                  pl.BlockSpec((rows, wv), lambda hg, n: (n, hg)),
                  pl.BlockSpec((None, rows, g), lambda hg, n: (hg, n, 0)),
                  pl.BlockSpec((None, rows, g), lambda hg, n: (hg, n, 0)),
                  pl.BlockSpec((None, nsub, g, CHUNK), lambda hg, n: (hg, n, 0, 0)),
                  pl.BlockSpec((None, 1, g), lambda hg, n: (hg, 0, 0)),
                  pl.BlockSpec((None, 1, g), lambda hg, n: (hg, 0, 0)),
                  pl.BlockSpec((None, g, 1), lambda hg, n: (hg, 0, 0)),
                  pl.BlockSpec((None, g, 1), lambda hg, n: (hg, 0, 0)),
                  pl.BlockSpec((1, GDN_DV), lambda hg, n: (0, 0))],
        out_specs=pl.BlockSpec((rows, wv), lambda hg, n: (n, hg)),
        out_shape=jax.ShapeDtypeStruct((s, GDN_V_HEADS * GDN_DV), BF16),
        scratch_shapes=[pltpu.VMEM((g, GDN_DK, GDN_DV), F32)],
        compiler_params=_cparams("parallel", "arbitrary"),
        name="gdn_delta",
    )(qkv, qkv, qkv, z, a_col, b_col, a_row, al_c, dt_c, al_r, dt_r, norm_w)


def _branch_kernel(om_ref, og_ref, wm_ref, wg_ref, gm_ref, gg_ref, bm_ref, bg_ref, o_ref):
    ym = _dot(om_ref[...], wm_ref[...])
    yg = _dot(og_ref[...], wg_ref[...])
    sm = _sigmoid(gm_ref[...].astype(F32) + bm_ref[...])
    sg = _sigmoid(gg_ref[...].astype(F32) + bg_ref[...])
    o_ref[...] = (sm * ym + sg * yg).astype(o_ref.dtype)


def _branch_merge(o_mla, o_gdn, w_o_mla, w_o_gdn, gates, b_gate):
    s = o_mla.shape[0]
    d = w_o_mla.shape[1]
    tm = min(1024, s)
    tn = min(512, d)
    nb = d // tn
    return pl.pallas_call(
        _branch_kernel,
        grid=(nb, s // tm),
        in_specs=[pl.BlockSpec((tm, o_mla.shape[1]), lambda j, i: (i, 0)),
                  pl.BlockSpec((tm, o_gdn.shape[1]), lambda j, i: (i, 0)),
                  pl.BlockSpec((w_o_mla.shape[0], tn), lambda j, i: (0, j)),
                  pl.BlockSpec((w_o_gdn.shape[0], tn), lambda j, i: (0, j)),
                  pl.BlockSpec((tm, tn), lambda j, i: (i, j)),
                  pl.BlockSpec((tm, tn), lambda j, i: (i, nb + j)),
                  pl.BlockSpec((1, tn), lambda j, i: (0, j)),
                  pl.BlockSpec((1, tn), lambda j, i: (0, nb + j))],
        out_specs=pl.BlockSpec((tm, tn), lambda j, i: (i, j)),
        out_shape=jax.ShapeDtypeStruct((s, d), BF16),
        compiler_params=_cparams("parallel", "parallel"),
        name="branch_merge",
    )(o_mla, o_gdn, w_o_mla, w_o_gdn, gates, gates, b_gate, b_gate)


def _out_ln_kernel(m_ref, w_ref, x_ref, g_ref, b_ref, o_ref):
    y = DN_ALPHA * x_ref[...] + _dot(m_ref[...], w_ref[...])
    o_ref[...] = _layernorm(y, g_ref[...], b_ref[...])


def _out_ln(mixed, w_out, x, g, b):
    s, d = x.shape
    tm = min(512, s)
    return pl.pallas_call(
        _out_ln_kernel,
        grid=(s // tm,),
        in_specs=[pl.BlockSpec((tm, d), lambda i: (i, 0)),
                  pl.BlockSpec((d, d), lambda i: (0, 0)),
                  pl.BlockSpec((tm, d), lambda i: (i, 0)),
                  pl.BlockSpec((1, d), lambda i: (0, 0)),
                  pl.BlockSpec((1, d), lambda i: (0, 0))],
        out_specs=pl.BlockSpec((tm, d), lambda i: (i, 0)),
        out_shape=jax.ShapeDtypeStruct((s, d), F32),
        compiler_params=_cparams("parallel"),
        name="out_ln1",
    )(mixed, w_out, x, g, b)


SLAB = 16


def _store_rows(ref2, val, per_token, row0=0):
    n = val.shape[0]
    for s in range(val.shape[1] // LANES):
        ref2[pl.ds(row0 + s, n, stride=per_token), :] = val[:, s * LANES:(s + 1) * LANES]


def _load_rows(ref2, n, per_token, row0, k):
    return jnp.concatenate([ref2[pl.ds(row0 + s, n, stride=per_token), :] for s in range(k)], axis=1)


def _mem_kernel(x_ref, wq_ref, kv_ref, wo_ref, g_ref, b_ref, wr_ref, br_ref, x2_ref, x2r_ref, lg_ref):
    x1 = x_ref[...]
    hd = MEM_HEAD_DIM
    nh = MEM_HEADS
    q = (_dot(x1.astype(BF16), wq_ref[...]) * float(hd ** -0.5)).astype(BF16)
    outs = []
    for h in range(nh):
        kh = kv_ref[:, h * hd:(h + 1) * hd]
        vh = kv_ref[:, (nh + h) * hd:(nh + h + 1) * hd]
        s = _dot_nt(q[:, h * hd:(h + 1) * hd], kh)
        p = jnp.exp(s - jnp.max(s, axis=-1, keepdims=True))
        o = _dot(p.astype(BF16), vh) / jnp.sum(p, axis=-1, keepdims=True)
        outs.append(o.astype(BF16))
    o = jnp.concatenate(outs, axis=1)
    y = DN_ALPHA * x1 + _dot(o, wo_ref[...])
    x2 = _layernorm(y, g_ref[...], b_ref[...])
    x2_ref[...] = x2
    _store_rows(x2r_ref, x2, SLAB)
    x_hi = x2.astype(BF16)
    x_lo = (x2 - x_hi.astype(F32)).astype(BF16)
    w = wr_ref[...]
    w_hi = w.astype(BF16)
    w_lo = (w - w_hi.astype(F32)).astype(BF16)
    lg_ref[...] = (_dot(x_hi, w_hi) + _dot(x_lo, w_hi)) + _dot(x_hi, w_lo) + br_ref[...]


def _mem_attn_ln(x1, w_mq, kvm, w_mo, g, b, w_r, b_r):
    s, d = x1.shape
    tm = min(512, s)
    return pl.pallas_call(
        _mem_kernel,
        grid=(s // tm,),
        in_specs=[pl.BlockSpec((tm, d), lambda i: (i, 0)),
                  pl.BlockSpec(w_mq.shape, lambda i: (0, 0)),
                  pl.BlockSpec(kvm.shape, lambda i: (0, 0)),
                  pl.BlockSpec(w_mo.shape, lambda i: (0, 0)),
                  pl.BlockSpec((1, d), lambda i: (0, 0)),
                  pl.BlockSpec((1, d), lambda i: (0, 0)),
                  pl.BlockSpec(w_r.shape, lambda i: (0, 0)),
                  pl.BlockSpec((1, LANES), lambda i: (0, 0))],
        out_specs=[pl.BlockSpec((tm, d), lambda i: (i, 0)),
                   pl.BlockSpec((tm * SLAB, LANES), lambda i: (i, 0)),
                   pl.BlockSpec((tm, LANES), lambda i: (i, 0))],
        out_shape=[jax.ShapeDtypeStruct((s, d), F32),
                   jax.ShapeDtypeStruct((s * SLAB, LANES), F32),
                   jax.ShapeDtypeStruct((s, LANES), F32)],
        compiler_params=_cparams("parallel"),
        name="mem_attn_ln2",
    )(x1, w_mq, kvm, w_mo, g, b, w_r, b_r)


SEL_E1, SEL_E2, SEL_R1, SEL_R2, SEL_G1, SEL_G2 = range(6)
GRP_LANE0 = N_EXPERTS


def _route_kernel(lg_ref, sel_ref, cnt_ref, carry):
    i = pl.program_id(0)

    @pl.when(i == 0)
    def _():
        carry[...] = jnp.zeros(carry.shape, F32)

    lg = lg_ref[...]
    tm = lg.shape[0]
    lane = lax.broadcasted_iota(jnp.int32, lg.shape, 1)
    big = jnp.int32(4 * LANES)
    neg = jnp.float32(-jnp.inf)

    def first_max(vals):
        mx = jnp.max(vals, axis=-1, keepdims=True)
        idx = jnp.min(jnp.where(vals == mx, lane, big), axis=-1, keepdims=True)
        return mx, idx

    is_grp = (lane >= GRP_LANE0) & (lane < GRP_LANE0 + N_GROUPS)
    gl = jnp.where(is_grp, lg, neg)
    gmax, gidx = first_max(gl)
    p_top = 1.0 / jnp.sum(jnp.where(is_grp, jnp.exp(gl - gmax), 0.0), axis=-1, keepdims=True)
    lo = (gidx - GRP_LANE0) * EXPERTS_PER_GROUP
    in_grp = (lane >= lo) & (lane < lo + EXPERTS_PER_GROUP)
    el = jnp.where(in_grp, lg, neg)
    m1, i1 = first_max(el)
    m2, i2 = first_max(jnp.where(lane == i1, neg, el))
    r = jnp.exp(m2 - m1)
    g1 = p_top / (1.0 + r)
    g2 = p_top * r / (1.0 + r)

    hot1 = lane == i1
    hot2 = lane == i2
    onehot = jnp.where(hot1, 1.0, 0.0) + jnp.where(hot2, 1.0, 0.0)
    ri = lax.broadcasted_iota(jnp.int32, (tm, tm), 0)
    ci = lax.broadcasted_iota(jnp.int32, (tm, tm), 1)
    before = jnp.where(ci < ri, 1.0, 0.0).astype(BF16)
    rank = _dot(before, onehot.astype(BF16)) + carry[...]
    r1 = jnp.sum(jnp.where(hot1, rank, 0.0), axis=-1, keepdims=True)
    r2 = jnp.sum(jnp.where(hot2, rank, 0.0), axis=-1, keepdims=True)
    carry[...] = carry[...] + jnp.sum(onehot, axis=0, keepdims=True)
    cnt_ref[...] = carry[...]

    out = jnp.zeros(lg.shape, F32)
    for ln, val in ((SEL_E1, i1.astype(F32)), (SEL_E2, i2.astype(F32)), (SEL_R1, r1), (SEL_R2, r2),
                    (SEL_G1, g1), (SEL_G2, g2)):
        out = jnp.where(lane == ln, val, out)
    sel_ref[...] = out


def _route(logits):
    t = logits.shape[0]
    tm = min(512, t)
    return pl.pallas_call(
        _route_kernel,
        grid=(t // tm,),
        in_specs=[pl.BlockSpec((tm, LANES), lambda i: (i, 0))],
        out_specs=[pl.BlockSpec((tm, LANES), lambda i: (i, 0)),
                   pl.BlockSpec((1, LANES), lambda i: (0, 0))],
        out_shape=[jax.ShapeDtypeStruct((t, LANES), F32),
                   jax.ShapeDtypeStruct((1, LANES), F32)],
        scratch_shapes=[pltpu.VMEM((1, LANES), F32)],
        compiler_params=_cparams("arbitrary"),
        name="moe_route",
    )(logits)


def _dispatch_kernel(cnt_ref, e1_ref, e2_ref, r1_ref, r2_ref, rowa_ref, blk_ref, rows_ref, d1_ref, d2_ref, pstart,
                     used_ref, *, t, n_blocks):
    def seg(e, start):
        pstart[e] = start
        c = cnt_ref[e]
        nb = (c + MOE_ROWS - 1) // MOE_ROWS
        end = start + nb * MOE_ROWS

        def mark(p, carry):
            rowa_ref[p] = -1
            return carry
        lax.fori_loop(start + c, end, mark, 0)

        def blk(bi, carry):
            blk_ref[bi] = e
            rows_ref[bi] = jnp.minimum(start + c - bi * MOE_ROWS, MOE_ROWS)
            return carry
        lax.fori_loop(start // MOE_ROWS, end // MOE_ROWS, blk, 0)
        return end

    total = lax.fori_loop(0, N_EXPERTS, seg, 0)
    used = total // MOE_ROWS
    used_ref[0] = used
    last = blk_ref[jnp.maximum(used - 1, 0)]

    def tail(bi, carry):
        blk_ref[bi] = last
        rows_ref[bi] = 0

        def unused(p, c2):
            rowa_ref[bi * MOE_ROWS + p] = -1
            return c2
        lax.fori_loop(0, MOE_ROWS, unused, 0, unroll=16)
        return carry
    lax.fori_loop(used, n_blocks, tail, 0)

    def place(tok, carry):
        p1 = pstart[e1_ref[tok]] + r1_ref[tok]
        p2 = pstart[e2_ref[tok]] + r2_ref[tok]
        rowa_ref[p1] = 2 * tok
        rowa_ref[p2] = 2 * tok + 1
        d1_ref[tok] = p1
        d2_ref[tok] = p2
        return carry
    lax.fori_loop(0, t, place, 0, unroll=8)


def _dispatch(cnt, e1, e2, r1, r2, n_blocks):
    t = e1.shape[0]
    smem = pl.BlockSpec(memory_space=pltpu.SMEM)
    i32 = lambda n: jax.ShapeDtypeStruct((n,), jnp.int32)
    return pl.pallas_call(
        functools.partial(_dispatch_kernel, t=t, n_blocks=n_blocks),
        in_specs=[smem] * 5,
        out_specs=[smem] * 7,
        out_shape=[i32(n_blocks * MOE_ROWS), i32(n_blocks), i32(n_blocks), i32(t), i32(t), i32(N_EXPERTS), i32(1)],
        name="moe_dispatch",
    )(cnt, e1, e2, r1, r2)


def _scatter_in_kernel(d1_ref, d2_ref, cnt_ref, pstart_ref, used_ref, x_ref, xs_hbm, zbuf, sem, zsem, *, tm, n_blocks):
    i = pl.program_id(0)
    blk_rows = MOE_ROWS * SLAB

    def slab(j):
        return pl.ds(pl.multiple_of(j * SLAB, SLAB), SLAB)

    def block(bi):
        return pl.ds(pl.multiple_of(bi * blk_rows, blk_rows), blk_rows)

    def pad_range(e):
        c = cnt_ref[e]
        lo = pstart_ref[e] + c
        return lo, pstart_ref[e] + (c + MOE_ROWS - 1) // MOE_ROWS * MOE_ROWS

    @pl.when(i == 0)
    def _():
        zbuf[...] = jnp.zeros(zbuf.shape, F32)

        def zero_pads(e, carry):
            lo, hi = pad_range(e)

            def z(p, c2):
                pltpu.make_async_copy(zbuf.at[slab(0)], xs_hbm.at[slab(p)], zsem).start()
                return c2
            lax.fori_loop(lo, hi, z, 0)
            return carry
        lax.fori_loop(0, N_EXPERTS, zero_pads, 0)

        def zero_block(bi, carry):
            pltpu.make_async_copy(zbuf, xs_hbm.at[block(bi)], zsem).start()
            return carry
        lax.fori_loop(used_ref[0], n_blocks, zero_block, 0)

    def scatter(r, carry):
        tok = i * tm + r
        pltpu.make_async_copy(x_ref.at[slab(r)], xs_hbm.at[slab(d1_ref[tok])], sem).start()
        pltpu.make_async_copy(x_ref.at[slab(r)], xs_hbm.at[slab(d2_ref[tok])], sem).start()
        return carry
    lax.fori_loop(0, tm, scatter, 0, unroll=8)

    for _ in range(2):
        pltpu.make_async_copy(x_ref, xs_hbm.at[pl.ds(0, tm * SLAB)], sem).wait()

    @pl.when(i == pl.num_programs(0) - 1)
    def _():
        def wait_pads(e, carry):
            lo, hi = pad_range(e)
            n = hi - lo

            @pl.when(n > 0)
            def _():
                rows = pl.ds(0, pl.multiple_of(n * SLAB, SLAB))
                pltpu.make_async_copy(zbuf.at[rows], xs_hbm.at[rows], zsem).wait()
            return carry
        lax.fori_loop(0, N_EXPERTS, wait_pads, 0)

        def wait_block(bi, carry):
            pltpu.make_async_copy(zbuf, xs_hbm.at[block(0)], zsem).wait()
            return carry
        lax.fori_loop(used_ref[0], n_blocks, wait_block, 0)


def _scatter_in(x2r, d1, d2, cnt, pstart, used, n_blocks):
    t = x2r.shape[0] // SLAB
    tm = min(MOE_ROWS, t)
    grid_spec = pltpu.PrefetchScalarGridSpec(
        num_scalar_prefetch=5,
        grid=(t // tm,),
        in_specs=[pl.BlockSpec((tm * SLAB, LANES), lambda i, *_: (i, 0))],
        out_specs=pl.BlockSpec(memory_space=pl.ANY),
        scratch_shapes=[pltpu.VMEM((MOE_ROWS * SLAB, LANES), F32),
                        pltpu.SemaphoreType.DMA(()), pltpu.SemaphoreType.DMA(())],
    )
    return pl.pallas_call(
        functools.partial(_scatter_in_kernel, tm=tm, n_blocks=n_blocks),
        grid_spec=grid_spec,
        out_shape=jax.ShapeDtypeStruct((n_blocks * MOE_ROWS * SLAB, LANES), F32),
        compiler_params=_cparams("arbitrary"),
        name="moe_scatter_in",
    )(d1, d2, cnt, pstart, used, x2r)


def _expert_kernel(blk_ref, rowa_ref, rows_ref, used_ref, x_ref, wg_ref, wu_ref, wd_ref, out_hbm, ybuf, ssem):
    b = pl.program_id(0)
    n_blocks = pl.num_programs(0)
    slot = b % 2

    def slab(i):
        return pl.ds(pl.multiple_of(i * SLAB, SLAB), SLAB)

    def scatter_copy(blk_slot, r, dst):
        return pltpu.make_async_copy(ybuf.at[blk_slot, slab(r)], out_hbm.at[slab(dst)], ssem.at[blk_slot])

    def wait_scatters(blk, blk_slot):
        n = rows_ref[blk]

        @pl.when(n > 0)
        def _():
            rows = pl.ds(0, pl.multiple_of(n * SLAB, SLAB))
            pltpu.make_async_copy(ybuf.at[blk_slot, rows], out_hbm.at[rows], ssem.at[blk_slot]).wait()

    @pl.when(b >= 2)
    def _():
        wait_scatters(jnp.maximum(b - 2, 0), slot)

    n_rows = rows_ref[b]

    @pl.when(n_rows > 0)
    def _():
        xb = _load_rows(x_ref, MOE_ROWS, SLAB, 0, SLAB).astype(BF16)
        hg = _dot(xb, wg_ref[...].astype(BF16))
        hu = _dot(xb, wu_ref[...].astype(BF16))
        hb = (hg * _sigmoid(hg) * hu).astype(BF16)
        _store_rows(ybuf.at[slot], _dot(hb, wd_ref[...].astype(BF16)), SLAB)

        def issue_scatter(r, carry):
            scatter_copy(slot, r, rowa_ref[b * MOE_ROWS + r]).start()
            return carry
        lax.fori_loop(0, n_rows, issue_scatter, 0)

    @pl.when(b == n_blocks - 1)
    def _():
        @pl.when(b >= 1)
        def _():
            wait_scatters(jnp.maximum(b - 1, 0), 1 - slot)
        wait_scatters(b, slot)


def _experts(xs, t, row_a, blk_exp, blk_rows, used, w_gate_e, w_up_e, w_down_e):
    d = SLAB * LANES
    n_blocks = blk_exp.shape[0]
    de = w_gate_e.shape[2]
    weights = lambda b, blk, rowa, rows, nu: (blk[b], 0, 0)
    grid_spec = pltpu.PrefetchScalarGridSpec(
        num_scalar_prefetch=4,
        grid=(n_blocks,),
        in_specs=[pl.BlockSpec((MOE_ROWS * SLAB, LANES), lambda b, blk, rowa, rows, nu: (jnp.minimum(b, nu[0] - 1), 0)),
                  pl.BlockSpec((None, d, de), weights),
                  pl.BlockSpec((None, d, de), weights),
                  pl.BlockSpec((None, de, d), weights)],
        out_specs=pl.BlockSpec(memory_space=pl.ANY),
        scratch_shapes=[pltpu.VMEM((2, MOE_ROWS * SLAB, LANES), F32), pltpu.SemaphoreType.DMA((2,))],
    )
    return pl.pallas_call(
        _expert_kernel,
        grid_spec=grid_spec,
        out_shape=jax.ShapeDtypeStruct((2 * t * SLAB, LANES), F32),
        compiler_params=_cparams("arbitrary"),
        name="moe_experts",
    )(blk_exp, row_a, blk_rows, used, xs, w_gate_e, w_up_e, w_down_e)


def _combine_kernel(x_ref, y_ref, sel_ref, g_ref, b_ref, o_ref):
    tm = x_ref.shape[0]
    sel = sel_ref[...]
    g1 = sel[:, SEL_G1:SEL_G1 + 1]
    g2 = sel[:, SEL_G2:SEL_G2 + 1]
    y1 = _load_rows(y_ref, tm, 2 * SLAB, 0, SLAB)
    y2 = _load_rows(y_ref, tm, 2 * SLAB, SLAB, SLAB)
    y = DN_ALPHA * x_ref[...] + (g1 * y1 + g2 * y2)
    o_ref[...] = _layernorm(y, g_ref[...], b_ref[...])


def _combine_ln(x2, ys, sel, g, b):
    t, d = x2.shape
    tm = min(512, t)
    return pl.pallas_call(
        _combine_kernel,
        grid=(t // tm,),
        in_specs=[pl.BlockSpec((tm, d), lambda i: (i, 0)),
                  pl.BlockSpec((tm * 2 * SLAB, LANES), lambda i: (i, 0)),
                  pl.BlockSpec((tm, LANES), lambda i: (i, 0)),
                  pl.BlockSpec((1, d), lambda i: (0, 0)),
                  pl.BlockSpec((1, d), lambda i: (0, 0))],
        out_specs=pl.BlockSpec((tm, d), lambda i: (i, 0)),
        out_shape=jax.ShapeDtypeStruct((t, d), F32),
        compiler_params=_cparams("parallel"),
        name="combine_ln3",
    )(x2, ys, sel, g, b)


def _mixer(xb, positions, w_in, b_gate, mla_q_norm, w_uq, mla_kv_norm, w_ukv, w_o_mla,
           gdn_conv, gdn_a_log, gdn_dt_bias, gdn_norm, w_o_gdn):
    s, d = xb.shape
    nqkv = 2 * GDN_QK_HEADS * GDN_DK + GDN_V_HEADS * GDN_DV
    nz = GDN_V_HEADS * GDN_DV
    o0 = 0
    o1 = o0 + MLA_Q_LORA
    o2 = o1 + MLA_KV_LORA + MLA_ROPE
    o3 = o2 + nqkv
    o4 = o3 + nz
    o5 = o4 + GDN_V_HEADS
    o6 = o5 + GDN_V_HEADS

    def rot_cols(w):
        half = MLA_ROPE // 2
        return jnp.concatenate([-w[..., half:], w[..., :half]], axis=-1)

    w_q = w_in[:, o0:o1].astype(BF16)
    w_kpe = w_in[:, o1 + MLA_KV_LORA:o2]
    w_kv = jnp.concatenate([w_in[:, o1:o1 + MLA_KV_LORA], w_kpe, rot_cols(w_kpe)], axis=1).astype(BF16)
    w_ba = jnp.concatenate([w_in[:, o4:o6], jnp.zeros((d, LANES - 2 * GDN_V_HEADS), F32)], axis=1).astype(BF16)
    qd = _matmul(xb, w_q, BF16, 1024, MLA_Q_LORA)
    kvd = _matmul(xb, w_kv, BF16, 1024, w_kv.shape[1])
    qkv_c = _qkv_conv_proj(xb, w_in[:, o2:o3].astype(BF16), gdn_conv)
    hz = _matmul(xb, w_in[:, o3:o4].astype(BF16), BF16, 1024, 2048)
    hba = _matmul(xb, w_ba, F32, 512, LANES)
    hgate = _matmul(xb, w_in[:, o6:].astype(BF16), BF16, 1024, 2048)

    cs = _rope_table(positions)
    hq = MLA_NOPE + MLA_ROPE
    wq3 = w_uq.reshape(MLA_Q_LORA, MLA_HEADS, hq)
    pe = wq3[..., MLA_NOPE:]
    wq = jnp.concatenate([pe, rot_cols(pe), wq3[..., :MLA_NOPE]], axis=-1)
    wq = wq.reshape(MLA_Q_LORA, MLA_HEADS * MLA_QK_PAD).astype(BF16)
    wkv3 = w_ukv.reshape(MLA_KV_LORA, MLA_HEADS, MLA_NOPE + MLA_V)
    wk = wkv3[..., :MLA_NOPE].reshape(MLA_KV_LORA, MLA_HEADS * MLA_NOPE).astype(BF16)
    wv = wkv3[..., MLA_NOPE:].reshape(MLA_KV_LORA, MLA_HEADS * MLA_V).astype(BF16)
    q = _mla_q_proj(qd, mla_q_norm.reshape(1, -1), wq, cs)
    k, v = _mla_kv_proj(kvd, mla_kv_norm.reshape(1, -1), wk, wv, cs)
    o_mla = _mla_attention(q, k, v)

    o_gdn = _gdn_delta(qkv_c, hz, hba[:, GDN_V_HEADS:2 * GDN_V_HEADS], hba[:, :GDN_V_HEADS],
                       gdn_a_log, gdn_dt_bias, gdn_norm.reshape(1, -1))

    return _branch_merge(o_mla, o_gdn, w_o_mla.astype(BF16), w_o_gdn.astype(BF16), hgate, b_gate.reshape(1, -1))


def _moe(x2, x2r, logits, w_gate_e, w_up_e, w_down_e, ln_g, ln_b):
    t, d = x2.shape
    n_blocks = (2 * t) // MOE_ROWS + N_EXPERTS
    sel, cnt = _route(logits)
    as_i32 = lambda col: sel[:, col].astype(jnp.int32)
    cnt_i = cnt[0, :N_EXPERTS].astype(jnp.int32)
    row_a, blk_exp, blk_rows, d1, d2, pstart, used = _dispatch(cnt_i, as_i32(SEL_E1), as_i32(SEL_E2),
                                                               as_i32(SEL_R1), as_i32(SEL_R2), n_blocks)
    xs = _scatter_in(x2r, d1, d2, cnt_i, pstart, used, n_blocks)
    ys = _experts(xs, t, row_a, blk_exp, blk_rows, used, w_gate_e, w_up_e, w_down_e)
    return _combine_ln(x2, ys, sel, ln_g, ln_b)


def _layer(x, mem, positions, w_in, b_gate, mla_q_norm, w_uq, mla_kv_norm, w_ukv, w_o_mla,
           gdn_conv, gdn_a_log, gdn_dt_bias, gdn_norm, w_o_gdn, w_out, ln1_g, ln1_b,
           w_mq, w_mkv, w_mo, ln2_g, ln2_b, w_route_grp, b_route_grp, w_route_exp, b_route_exp,
           w_gate_e, w_up_e, w_down_e, ln3_g, ln3_b):
    d = x.shape[1]
    row = lambda p: p.reshape(1, -1)
    mixed = _mixer(x.astype(BF16), positions, w_in, b_gate, mla_q_norm, w_uq, mla_kv_norm, w_ukv, w_o_mla,
                   gdn_conv, gdn_a_log, gdn_dt_bias, gdn_norm, w_o_gdn)
    x1 = _out_ln(mixed, w_out.astype(BF16), x, row(ln1_g), row(ln1_b))

    kvm = _matmul(mem.astype(BF16), w_mkv.astype(BF16), BF16, 256, 512)
    pad = LANES - N_EXPERTS - N_GROUPS
    w_r = jnp.concatenate([w_route_exp, w_route_grp, jnp.zeros((d, pad), F32)], axis=1)
    b_r = jnp.concatenate([b_route_exp, b_route_grp, jnp.zeros((pad,), F32)]).reshape(1, LANES)
    x2, x2r, logits = _mem_attn_ln(x1, w_mq.astype(BF16), kvm, w_mo.astype(BF16), row(ln2_g), row(ln2_b), w_r, b_r)

    return _moe(x2, x2r, logits, w_gate_e, w_up_e, w_down_e, row(ln3_g), row(ln3_b))


def kernel(x, mem, positions, w_in, b_gate, mla_q_norm, w_uq, mla_kv_norm, w_ukv, w_o_mla, gdn_conv, gdn_a_log,
           gdn_dt_bias, gdn_norm, w_o_gdn, w_out, ln1_g, ln1_b, w_mq, w_mkv, w_mo, ln2_g, ln2_b, w_route_grp,
           b_route_grp, w_route_exp, b_route_exp, w_gate_e, w_up_e, w_down_e, ln3_g, ln3_b):
    outs = []
    for bi in range(x.shape[0]):
        h = x[bi]
        for l in range(w_in.shape[0]):
            h = _layer(h, mem[bi], positions[bi], w_in[l], b_gate[l], mla_q_norm[l], w_uq[l], mla_kv_norm[l],
                       w_ukv[l], w_o_mla[l], gdn_conv[l], gdn_a_log[l], gdn_dt_bias[l], gdn_norm[l], w_o_gdn[l],
                       w_out[l], ln1_g[l], ln1_b[l], w_mq[l], w_mkv[l], w_mo[l], ln2_g[l], ln2_b[l],
                       w_route_grp[l], b_route_grp[l], w_route_exp[l], b_route_exp[l], w_gate_e[l], w_up_e[l],
                       w_down_e[l], ln3_g[l], ln3_b[l])
        outs.append(h)
    return jnp.stack(outs, axis=0)
```

```python
import functools

import numpy as np
import jax
import jax.numpy as jnp
from jax import lax
from jax.experimental import pallas as pl
from jax.experimental.pallas import tpu as pltpu

F32 = jnp.float32
BF16 = jnp.bfloat16
HIGHEST = lax.Precision.HIGHEST

LANES = 128
VMEM_LIMIT = 56 * 1024 * 1024

CHUNK = 64
MLA_HEADS = 16
MLA_Q_LORA = 768
MLA_KV_LORA = 512
MLA_NOPE = 128
MLA_ROPE = 64
MLA_V = 128
MLA_QK_PAD = 256
ROPE_THETA = 10000.0
GDN_QK_HEADS = 16
GDN_V_HEADS = 32
GDN_DK = 128
GDN_DV = 128
GDN_CONV = 4
GDN_GROUP = 32
GDN_STEP_CHUNKS = 2
MEM_HEADS = 4
MEM_HEAD_DIM = 128
N_GROUPS = 8
EXPERTS_PER_GROUP = 8
N_EXPERTS = 64
D_EXPERT = 512
MOE_ROWS = 256
RMS_EPS = 1e-6
LN_EPS = 1e-5
DN_ALPHA = 2.0 ** 0.25


def _cparams(*sem):
    return pltpu.CompilerParams(dimension_semantics=sem, vmem_limit_bytes=VMEM_LIMIT)


def _dot(a, b, **kw):
    return jnp.dot(a, b, preferred_element_type=F32, **kw)


def _dot_nt(a, b):
    return lax.dot_general(a, b, (((1,), (1,)), ((), ())), preferred_element_type=F32)


def _dot_tn(a, b):
    return lax.dot_general(a, b, (((0,), (0,)), ((), ())), preferred_element_type=F32)


def _sigmoid(x):
    return 1.0 / (1.0 + jnp.exp(-x))


def _layernorm(y, g, b):
    mu = jnp.mean(y, axis=-1, keepdims=True)
    d = y - mu
    var = jnp.mean(d * d, axis=-1, keepdims=True)
    return d * lax.rsqrt(var + LN_EPS) * g + b


def _mm_kernel(x_ref, w_ref, o_ref):
    o_ref[...] = _dot(x_ref[...], w_ref[...]).astype(o_ref.dtype)


def _matmul(x, w, out_dtype, tm, tn):
    m, k = x.shape
    n = w.shape[1]
    tm, tn = min(tm, m), min(tn, n)
    return pl.pallas_call(
        _mm_kernel,
        grid=(n // tn, m // tm),
        in_specs=[pl.BlockSpec((tm, k), lambda j, i: (i, 0)),
                  pl.BlockSpec((k, tn), lambda j, i: (0, j))],
        out_specs=pl.BlockSpec((tm, tn), lambda j, i: (i, j)),
        out_shape=jax.ShapeDtypeStruct((m, n), out_dtype),
        compiler_params=_cparams("parallel", "parallel"),
        name="matmul",
    )(x, w)


def _rope_table_kernel(pos_ref, inv_ref, o_ref):
    ang = pos_ref[...].astype(F32) * inv_ref[...]
    lane = lax.broadcasted_iota(jnp.int32, ang.shape, 1)
    o_ref[...] = jnp.where(lane < MLA_ROPE, jnp.cos(ang), jnp.sin(ang))


def _rope_table(positions):
    s = positions.shape[0]
    inv = 1.0 / (ROPE_THETA ** (np.arange(0, MLA_ROPE, 2, dtype=np.float32) / MLA_ROPE))
    inv4 = jnp.asarray(np.tile(inv.astype(np.float32), 4)[None, :])
    tm = min(512, s)
    return pl.pallas_call(
        _rope_table_kernel,
        grid=(s // tm,),
        in_specs=[pl.BlockSpec((tm, 1), lambda i: (i, 0)),
                  pl.BlockSpec((1, LANES), lambda i: (0, 0))],
        out_specs=pl.BlockSpec((tm, LANES), lambda i: (i, 0)),
        out_shape=jax.ShapeDtypeStruct((s, LANES), F32),
        compiler_params=_cparams("parallel"),
        name="rope_table",
    )(positions.reshape(s, 1), inv4)


def _rope_pair(t, cs):
    a = t * cs
    return a + pltpu.roll(a, MLA_ROPE, axis=1)


def _mla_q_kernel(qd_ref, qn_ref, w_ref, cs_ref, o_ref, *, scale):
    x = qd_ref[...].astype(F32)
    cq = x * lax.rsqrt(jnp.mean(x * x, axis=-1, keepdims=True) + RMS_EPS) * qn_ref[...]
    cqb = cq.astype(BF16)
    cs = cs_ref[...]
    for h in range(MLA_HEADS):
        lo = h * MLA_QK_PAD
        p = _dot(cqb, w_ref[:, lo:lo + MLA_QK_PAD])
        o_ref[:, lo:lo + LANES] = (_rope_pair(p[:, :LANES], cs) * scale).astype(o_ref.dtype)
        o_ref[:, lo + LANES:lo + MLA_QK_PAD] = (p[:, LANES:] * scale).astype(o_ref.dtype)


def _mla_q_proj(qd, q_norm, wq, cs):
    s = qd.shape[0]
    tm = min(512, s)
    n = MLA_HEADS * MLA_QK_PAD
    scale = float((MLA_NOPE + MLA_ROPE) ** -0.5 * np.log2(np.e))
    return pl.pallas_call(
        functools.partial(_mla_q_kernel, scale=scale),
        grid=(s // tm,),
        in_specs=[pl.BlockSpec((tm, MLA_Q_LORA), lambda i: (i, 0)),
                  pl.BlockSpec((1, MLA_Q_LORA), lambda i: (0, 0)),
                  pl.BlockSpec((MLA_Q_LORA, n), lambda i: (0, 0)),
                  pl.BlockSpec((tm, LANES), lambda i: (i, 0))],
        out_specs=pl.BlockSpec((tm, n), lambda i: (i, 0)),
        out_shape=jax.ShapeDtypeStruct((s, n), BF16),
        compiler_params=_cparams("parallel"),
        name="mla_q_proj",
    )(qd, q_norm, wq, cs)


def _mla_kv_kernel(kv_ref, kn_ref, wk_ref, wv_ref, cs_ref, k_ref, v_ref):
    x = kv_ref[:, :MLA_KV_LORA].astype(F32)
    ckv = x * lax.rsqrt(jnp.mean(x * x, axis=-1, keepdims=True) + RMS_EPS) * kn_ref[...]
    cb = ckv.astype(BF16)
    pe = _rope_pair(kv_ref[:, MLA_KV_LORA:].astype(F32), cs_ref[...])
    lane = lax.broadcasted_iota(jnp.int32, pe.shape, 1)
    pe = jnp.where(lane < MLA_ROPE, pe, 0.0).astype(k_ref.dtype)
    kn = _dot(cb, wk_ref[...]).astype(k_ref.dtype)
    for h in range(MLA_HEADS):
        lo = h * MLA_QK_PAD
        k_ref[:, lo:lo + LANES] = pe
        k_ref[:, lo + LANES:lo + MLA_QK_PAD] = kn[:, h * MLA_NOPE:(h + 1) * MLA_NOPE]
    v_ref[...] = _dot(cb, wv_ref[...]).astype(v_ref.dtype)


def _mla_kv_proj(kvd, kv_norm, wk, wv, cs):
    s, w = kvd.shape
    tm = min(512, s)
    nk = MLA_HEADS * MLA_QK_PAD
    nv = MLA_HEADS * MLA_V
    return pl.pallas_call(
        _mla_kv_kernel,
        grid=(s // tm,),
        in_specs=[pl.BlockSpec((tm, w), lambda i: (i, 0)),
                  pl.BlockSpec((1, MLA_KV_LORA), lambda i: (0, 0)),
                  pl.BlockSpec(wk.shape, lambda i: (0, 0)),
                  pl.BlockSpec(wv.shape, lambda i: (0, 0)),
                  pl.BlockSpec((tm, LANES), lambda i: (i, 0))],
        out_specs=[pl.BlockSpec((tm, nk), lambda i: (i, 0)),
                   pl.BlockSpec((tm, nv), lambda i: (i, 0))],
        out_shape=[jax.ShapeDtypeStruct((s, nk), BF16),
                   jax.ShapeDtypeStruct((s, nv), BF16)],
        compiler_params=_cparams("parallel"),
        name="mla_kv_proj",
    )(kvd, kv_norm, wk, wv, cs)


FLASH_HEADS = 4
FLASH_TILE = 1024


def _lane_repeat(x, n):
    return jnp.concatenate([x] * n, axis=1)


def _flash_kernel(qi_ref, kj_ref, q_ref, k_ref, v_ref, o_ref, m_sc, acc_sc):
    t = pl.program_id(1)
    i = qi_ref[t]
    j = kj_ref[t]
    hps = m_sc.shape[0]
    tk = k_ref.shape[0]
    heads = range(hps)

    @pl.when(j == 0)
    def _():
        m_sc[...] = jnp.full(m_sc.shape, -1e30, F32)
        acc_sc[...] = jnp.zeros(acc_sc.shape, F32)

    def update(r0, nr, nk, masked):
        rows = slice(r0, r0 + nr)
        ss = [_dot_nt(q_ref[rows, h * MLA_QK_PAD:(h + 1) * MLA_QK_PAD], k_ref[0:nk, h * MLA_QK_PAD:(h + 1) * MLA_QK_PAD])
              for h in heads]
        if masked:
            r = (lax.broadcasted_iota(jnp.int32, ss[0].shape, 0) + r0) // CHUNK
            c = lax.broadcasted_iota(jnp.int32, ss[0].shape, 1) // CHUNK
            keep = c <= r
            ss = [jnp.where(keep, s, -1e30) for s in ss]
        m_prev = [m_sc[h, rows] for h in heads]
        m_new = [jnp.maximum(mp, jnp.max(s, axis=-1, keepdims=True)) for mp, s in zip(m_prev, ss)]
        alpha = [jnp.exp2(mp - mn) for mp, mn in zip(m_prev, m_new)]
        ps = [jnp.exp2(s - _lane_repeat(mn, nk // LANES)).astype(BF16) for s, mn in zip(ss, m_new)]
        ones = jnp.ones((nk, MLA_V), BF16)
        pv = [_dot(p, jnp.concatenate([v_ref[0:nk, h * MLA_V:(h + 1) * MLA_V], ones], axis=1))
              for h, p in zip(heads, ps)]
        for h in heads:
            acc_sc[h, rows] = _lane_repeat(alpha[h], 2) * acc_sc[h, rows] + pv[h]
            m_sc[h, rows] = m_new[h]

    @pl.when(j < i)
    def _():
        update(0, tk, tk, False)

    @pl.when(j == i)
    def _():
        half = tk // 2
        if half % CHUNK == 0 and half % LANES == 0:
            update(0, half, half, True)
            update(half, half, tk, True)
        else:
            update(0, tk, tk, True)
        for h in heads:
            acc = acc_sc[h]
            o_ref[:, h * MLA_V:(h + 1) * MLA_V] = (acc[:, :MLA_V] / acc[:, MLA_V:]).astype(o_ref.dtype)


def _mla_attention(q, k, v):
    s = q.shape[0]
    t = min(FLASH_TILE, s)
    n = s // t
    hps = FLASH_HEADS
    qi = np.array([i for i in range(n) for _ in range(i + 1)], np.int32)
    kj = np.array([j for i in range(n) for j in range(i + 1)], np.int32)
    grid_spec = pltpu.PrefetchScalarGridSpec(
        num_scalar_prefetch=2,
        grid=(MLA_HEADS // hps, qi.shape[0]),
        in_specs=[pl.BlockSpec((t, hps * MLA_QK_PAD), lambda h, st, qi_r, kj_r: (qi_r[st], h)),
                  pl.BlockSpec((t, hps * MLA_QK_PAD), lambda h, st, qi_r, kj_r: (kj_r[st], h)),
                  pl.BlockSpec((t, hps * MLA_V), lambda h, st, qi_r, kj_r: (kj_r[st], h))],
        out_specs=pl.BlockSpec((t, hps * MLA_V), lambda h, st, qi_r, kj_r: (qi_r[st], h)),
        scratch_shapes=[pltpu.VMEM((hps, t, LANES), F32), pltpu.VMEM((hps, t, 2 * MLA_V), F32)],
    )
    return pl.pallas_call(
        _flash_kernel,
        grid_spec=grid_spec,
        out_shape=jax.ShapeDtypeStruct((s, MLA_HEADS * MLA_V), BF16),
        compiler_params=_cparams("parallel", "arbitrary"),
        name="mla_flash",
    )(jnp.asarray(qi), jnp.asarray(kj), q, k, v)


CONV_HALO = 8


def _qkv_conv_kernel(x_ref, w_ref, cw_ref, o_ref, buf, *, tm, tn, n_qk_blocks):
    c = pl.program_id(0)
    i = pl.program_id(1)
    halo = CONV_HALO

    @pl.when(i == 0)
    def _():
        buf[0:halo, :] = jnp.zeros((halo, tn), F32)

    h = _dot(x_ref[...], w_ref[...])
    buf[halo:halo + tm, :] = h
    ext = buf[...]
    y = cw_ref[GDN_CONV - 1:GDN_CONV, :] * h
    for k in range(1, GDN_CONV):
        y = y + cw_ref[GDN_CONV - 1 - k:GDN_CONV - k, :] * pltpu.roll(ext, k, axis=0)[halo:halo + tm, :]
    buf[0:halo, :] = h[tm - halo:tm, :]
    y = y * (0.5 * jnp.tanh(0.5 * y) + 0.5)
    is_qk = c < n_qk_blocks
    for g in range(tn // LANES):
        seg = y[:, g * LANES:(g + 1) * LANES]
        inv = lax.rsqrt(jnp.sum(seg * seg, axis=-1, keepdims=True) + 1e-6)
        o_ref[:, g * LANES:(g + 1) * LANES] = (seg * jnp.where(is_qk, inv, 1.0)).astype(o_ref.dtype)


def _qkv_conv_proj(xb, w_qkv, conv_w):
    s, d = xb.shape
    c = w_qkv.shape[1]
    tm = min(1024, s)
    tn = 1024
    n_qk_blocks = (2 * GDN_QK_HEADS * GDN_DK) // tn
    kern = functools.partial(_qkv_conv_kernel, tm=tm, tn=tn, n_qk_blocks=n_qk_blocks)
    return pl.pallas_call(
        kern,
        grid=(c // tn, s // tm),
        in_specs=[pl.BlockSpec((tm, d), lambda cc, i: (i, 0)),
                  pl.BlockSpec((d, tn), lambda cc, i: (0, cc)),
                  pl.BlockSpec((GDN_CONV, tn), lambda cc, i: (0, cc))],
        out_specs=pl.BlockSpec((tm, tn), lambda cc, i: (i, cc)),
        out_shape=jax.ShapeDtypeStruct((s, c), BF16),
        scratch_shapes=[pltpu.VMEM((tm + CONV_HALO, tn), F32)],
        compiler_params=_cparams("parallel", "arbitrary"),
        name="qkv_conv_proj",
    )(xb, w_qkv, conv_w)


def _softplus(x):
    return jnp.maximum(x, 0.0) + jnp.log1p(jnp.exp(-jnp.abs(x)))


def _gdn_chunk_kernel(q_ref, k_ref, v_ref, z_ref, ac_ref, bc_ref, ar_ref, alc_ref, dtc_ref, alr_ref, dtr_ref,
                      nw_ref, o_ref, state):
    n = pl.program_id(1)
    g_heads = GDN_GROUP
    c = CHUNK

    @pl.when(n == 0)
    def _():
        state[...] = jnp.zeros(state.shape, F32)

    ri = lax.broadcasted_iota(jnp.int32, (c, c), 0)
    ci = lax.broadcasted_iota(jnp.int32, (c, c), 1)
    tril = ci <= ri
    strict = ci < ri
    ltri = tril.astype(F32)
    utri = (ri <= ci).astype(F32)
    eye = (ri == ci).astype(F32)

    nsub = GDN_STEP_CHUNKS
    rows = lambda j: slice(j * c, (j + 1) * c)
    g_col = -jnp.exp(alc_ref[...]) * _softplus(ac_ref[...] + dtc_ref[...])
    beta_all = _sigmoid(bc_ref[...])
    gc_col = [_dot(ltri, g_col[rows(j)], precision=HIGHEST) for j in range(nsub)]
    gc_row = [_dot(-jnp.exp(alr_ref[...]) * _softplus(ar_ref[j] + dtr_ref[...]), utri, precision=HIGHEST)
              for j in range(nsub)]
    nw = nw_ref[...]
    scale = float(GDN_DK ** -0.5)

    vheads = range(g_heads)
    items = [(j, h) for j in range(nsub) for h in vheads]
    qitems = [(j, h) for j in range(nsub) for h in range(g_heads // 2)]
    qs = {(j, h): q_ref[rows(j), h * GDN_DK:(h + 1) * GDN_DK] for j, h in qitems}
    ks = {(j, h): k_ref[rows(j), h * GDN_DK:(h + 1) * GDN_DK] for j, h in qitems}
    kqs = {i: _dot_nt(jnp.concatenate([ks[i], qs[i]], axis=0), ks[i]) for i in qitems}
    bcol = {(j, h): beta_all[rows(j), h:h + 1] for j, h in items}
    gcol = {(j, h): gc_col[j][:, h:h + 1] for j, h in items}
    glast = {(j, h): gc_col[j][c - 1:c, h:h + 1] for j, h in items}
    decay = {(j, h): jnp.where(tril, jnp.exp(jnp.where(tril, gcol[j, h] - gc_row[j][h:h + 1, :], 0.0)), 0.0)
             for j, h in items}
    m = {(j, h): jnp.where(strict, kqs[j, h // 2][:c] * bcol[j, h] * decay[j, h], 0.0) for j, h in items}
    a_qk = {(j, h): (jnp.where(tril, kqs[j, h // 2][c:] * decay[j, h], 0.0) * scale).astype(BF16) for j, h in items}
    t_inv = {i: eye - m[i] for i in items}
    xp = {i: m[i].astype(BF16) for i in items}
    xp = {i: _dot(xp[i], xp[i]).astype(BF16) for i in items}
    for _ in range(4):
        prod = {i: _dot(jnp.concatenate([xp[i], t_inv[i].astype(BF16)], axis=0), xp[i]) for i in items}
        xp = {i: prod[i][:c].astype(BF16) for i in items}
        t_inv = {i: t_inv[i] + prod[i][c:] for i in items}
    t_inv = {i: t_inv[i] + _dot(t_inv[i].astype(BF16), xp[i]) for i in items}
    egc = {i: jnp.exp(gcol[i]) for i in items}
    kf = {i: ks[i].astype(F32) for i in qitems}
    rhs = {(j, h): jnp.concatenate([v_ref[rows(j), h * GDN_DV:(h + 1) * GDN_DV].astype(F32) * bcol[j, h],
                                    kf[j, h // 2] * (bcol[j, h] * egc[j, h])], axis=1).astype(BF16)
           for j, h in items}
    uw = {i: _dot(t_inv[i].astype(BF16), rhs[i]) for i in items}
    lhs = {(j, h): jnp.concatenate([uw[j, h][:, GDN_DV:], qs[j, h // 2].astype(F32) * egc[j, h]],
                                   axis=0).astype(BF16) for j, h in items}
    kdec = {(j, h): (kf[j, h // 2] * jnp.exp(glast[j, h] - gcol[j, h])).astype(BF16) for j, h in items}

    st = [state[h] for h in vheads]
    for j in range(nsub):
        ws = [_dot(lhs[j, h], st[h].astype(BF16)) for h in vheads]
        v_new = [(uw[j, h][:, :GDN_DV] - ws[h][:c]).astype(BF16) for h in vheads]
        o = [ws[h][c:] * scale + _dot(a_qk[j, h], v_new[h]) for h in vheads]
        st = [st[h] * jnp.exp(glast[j, h]) + _dot_tn(kdec[j, h], v_new[h]) for h in vheads]
        for h in vheads:
            on = o[h] * lax.rsqrt(jnp.mean(o[h] * o[h], axis=-1, keepdims=True) + RMS_EPS) * nw
            zz = z_ref[rows(j), h * GDN_DV:(h + 1) * GDN_DV].astype(F32)
            o_ref[rows(j), h * GDN_DV:(h + 1) * GDN_DV] = (on * (zz * _sigmoid(zz))).astype(o_ref.dtype)
    for h in vheads:
        state[h] = st[h]


def _gdn_delta(qkv, z, a, b, a_log, dt_bias, norm_w):
    s = qkv.shape[0]
    g = GDN_GROUP
    ng = GDN_V_HEADS // g
    nc = s // CHUNK
    gq = g // 2
    wq = gq * GDN_DK
    wv = g * GDN_DV
    k_off = (GDN_QK_HEADS * GDN_DK) // wq
    v_off = (2 * GDN_QK_HEADS * GDN_DK) // wv
    a_col = a.reshape(s, ng, g).transpose(1, 0, 2)
    b_col = b.reshape(s, ng, g).transpose(1, 0, 2)
    a_row = a.reshape(nc, CHUNK, ng, g).transpose(2, 0, 3, 1)
    al_c = a_log.reshape(ng, 1, g)
    dt_c = dt_bias.reshape(ng, 1, g)
    al_r = a_log.reshape(ng, g, 1)
    dt_r = dt_bias.reshape(ng, g, 1)
    nsub = GDN_STEP_CHUNKS
    rows = nsub * CHUNK
    return pl.pallas_call(
        _gdn_chunk_kernel,
        grid=(ng, nc // nsub),
        in_specs=[pl.BlockSpec((rows, wq), lambda hg, n: (n, hg)),
                  pl.BlockSpec((rows, wq), lambda hg, n: (n, k_off + hg)),
                  pl.BlockSpec((rows, wv), lambda hg, n: (n, v_off + hg)),
                  pl.BlockSpec((rows, wv), lambda hg, n: (n, hg)),
                  pl.BlockSpec((None, rows, g), lambda hg, n: (hg, n, 0)),
                  pl.BlockSpec((None, rows, g), lambda hg, n: (hg, n, 0)),
                  pl.BlockSpec((None, nsub, g, CHUNK), lambda hg, n: (hg, n, 0, 0)),
                  pl.BlockSpec((None, 1, g), lambda hg, n: (hg, 0, 0)),
                  pl.BlockSpec((None, 1, g), lambda hg, n: (hg, 0, 0)),
                  pl.BlockSpec((None, g, 1), lambda hg, n: (hg, 0, 0)),
                  pl.BlockSpec((None, g, 1), lambda hg, n: (hg, 0, 0)),
                  pl.BlockSpec((1, GDN_DV), lambda hg, n: (0, 0))],
        out_specs=pl.BlockSpec((rows, wv), lambda hg, n: (n, hg)),
        out_shape=jax.ShapeDtypeStruct((s, GDN_V_HEADS * GDN_DV), BF16),
        scratch_shapes=[pltpu.VMEM((g, GDN_DK, GDN_DV), F32)],
        compiler_params=_cparams("parallel", "arbitrary"),
        name="gdn_delta",
    )(qkv, qkv, qkv, z, a_col, b_col, a_row, al_c, dt_c, al_r, dt_r, norm_w)


def _branch_kernel(om_ref, og_ref, wm_ref, wg_ref, gm_ref, gg_ref, bm_ref, bg_ref, o_ref):
    ym = _dot(om_ref[...], wm_ref[...])
    yg = _dot(og_ref[...], wg_ref[...])
    sm = _sigmoid(gm_ref[...].astype(F32) + bm_ref[...])
    sg = _sigmoid(gg_ref[...].astype(F32) + bg_ref[...])
    o_ref[...] = (sm * ym + sg * yg).astype(o_ref.dtype)


def _branch_merge(o_mla, o_gdn, w_o_mla, w_o_gdn, gates, b_gate):
    s = o_mla.shape[0]
    d = w_o_mla.shape[1]
    tm = min(1024, s)
    tn = min(512, d)
    nb = d // tn
    return pl.pallas_call(
        _branch_kernel,
        grid=(nb, s // tm),
        in_specs=[pl.BlockSpec((tm, o_mla.shape[1]), lambda j, i: (i, 0)),
                  pl.BlockSpec((tm, o_gdn.shape[1]), lambda j, i: (i, 0)),
                  pl.BlockSpec((w_o_mla.shape[0], tn), lambda j, i: (0, j)),
                  pl.BlockSpec((w_o_gdn.shape[0], tn), lambda j, i: (0, j)),
                  pl.BlockSpec((tm, tn), lambda j, i: (i, j)),
                  pl.BlockSpec((tm, tn), lambda j, i: (i, nb + j)),
                  pl.BlockSpec((1, tn), lambda j, i: (0, j)),
                  pl.BlockSpec((1, tn), lambda j, i: (0, nb + j))],
        out_specs=pl.BlockSpec((tm, tn), lambda j, i: (i, j)),
        out_shape=jax.ShapeDtypeStruct((s, d), BF16),
        compiler_params=_cparams("parallel", "parallel"),
        name="branch_merge",
    )(o_mla, o_gdn, w_o_mla, w_o_gdn, gates, gates, b_gate, b_gate)


def _out_ln_kernel(m_ref, w_ref, x_ref, g_ref, b_ref, o_ref):
    y = DN_ALPHA * x_ref[...] + _dot(m_ref[...], w_ref[...])
    o_ref[...] = _layernorm(y, g_ref[...], b_ref[...])


def _out_ln(mixed, w_out, x, g, b):
    s, d = x.shape
    tm = min(512, s)
    return pl.pallas_call(
        _out_ln_kernel,
        grid=(s // tm,),
        in_specs=[pl.BlockSpec((tm, d), lambda i: (i, 0)),
                  pl.BlockSpec((d, d), lambda i: (0, 0)),
                  pl.BlockSpec((tm, d), lambda i: (i, 0)),
                  pl.BlockSpec((1, d), lambda i: (0, 0)),
                  pl.BlockSpec((1, d), lambda i: (0, 0))],
        out_specs=pl.BlockSpec((tm, d), lambda i: (i, 0)),
        out_shape=jax.ShapeDtypeStruct((s, d), F32),
        compiler_params=_cparams("parallel"),
        name="out_ln1",
    )(mixed, w_out, x, g, b)


SLAB = 16


def _store_rows(ref2, val, per_token, row0=0):
    n = val.shape[0]
    for s in range(val.shape[1] // LANES):
        ref2[pl.ds(row0 + s, n, stride=per_token), :] = val[:, s * LANES:(s + 1) * LANES]


def _load_rows(ref2, n, per_token, row0, k):
    return jnp.concatenate([ref2[pl.ds(row0 + s, n, stride=per_token), :] for s in range(k)], axis=1)


def _mem_kernel(x_ref, wq_ref, kv_ref, wo_ref, g_ref, b_ref, wr_ref, br_ref, x2_ref, x2r_ref, lg_ref):
    x1 = x_ref[...]
    hd = MEM_HEAD_DIM
    nh = MEM_HEADS
    q = (_dot(x1.astype(BF16), wq_ref[...]) * float(hd ** -0.5)).astype(BF16)
    outs = []
    for h in range(nh):
        kh = kv_ref[:, h * hd:(h + 1) * hd]
        vh = kv_ref[:, (nh + h) * hd:(nh + h + 1) * hd]
        s = _dot_nt(q[:, h * hd:(h + 1) * hd], kh)
        p = jnp.exp(s - jnp.max(s, axis=-1, keepdims=True))
        o = _dot(p.astype(BF16), vh) / jnp.sum(p, axis=-1, keepdims=True)
        outs.append(o.astype(BF16))
    o = jnp.concatenate(outs, axis=1)
    y = DN_ALPHA * x1 + _dot(o, wo_ref[...])
    x2 = _layernorm(y, g_ref[...], b_ref[...])
    x2_ref[...] = x2
    _store_rows(x2r_ref, x2, SLAB)
    x_hi = x2.astype(BF16)
    x_lo = (x2 - x_hi.astype(F32)).astype(BF16)
    w = wr_ref[...]
    w_hi = w.astype(BF16)
    w_lo = (w - w_hi.astype(F32)).astype(BF16)
    lg_ref[...] = (_dot(x_hi, w_hi) + _dot(x_lo, w_hi)) + _dot(x_hi, w_lo) + br_ref[...]


def _mem_attn_ln(x1, w_mq, kvm, w_mo, g, b, w_r, b_r):
    s, d = x1.shape
    tm = min(512, s)
    return pl.pallas_call(
        _mem_kernel,
        grid=(s // tm,),
        in_specs=[pl.BlockSpec((tm, d), lambda i: (i, 0)),
                  pl.BlockSpec(w_mq.shape, lambda i: (0, 0)),
                  pl.BlockSpec(kvm.shape, lambda i: (0, 0)),
                  pl.BlockSpec(w_mo.shape, lambda i: (0, 0)),
                  pl.BlockSpec((1, d), lambda i: (0, 0)),
                  pl.BlockSpec((1, d), lambda i: (0, 0)),
                  pl.BlockSpec(w_r.shape, lambda i: (0, 0)),
                  pl.BlockSpec((1, LANES), lambda i: (0, 0))],
        out_specs=[pl.BlockSpec((tm, d), lambda i: (i, 0)),
                   pl.BlockSpec((tm * SLAB, LANES), lambda i: (i, 0)),
                   pl.BlockSpec((tm, LANES), lambda i: (i, 0))],
        out_shape=[jax.ShapeDtypeStruct((s, d), F32),
                   jax.ShapeDtypeStruct((s * SLAB, LANES), F32),
                   jax.ShapeDtypeStruct((s, LANES), F32)],
        compiler_params=_cparams("parallel"),
        name="mem_attn_ln2",
    )(x1, w_mq, kvm, w_mo, g, b, w_r, b_r)


SEL_E1, SEL_E2, SEL_R1, SEL_R2, SEL_G1, SEL_G2 = range(6)
GRP_LANE0 = N_EXPERTS


def _route_kernel(lg_ref, sel_ref, cnt_ref, carry):
    i = pl.program_id(0)

    @pl.when(i == 0)
    def _():
        carry[...] = jnp.zeros(carry.shape, F32)

    lg = lg_ref[...]
    tm = lg.shape[0]
    lane = lax.broadcasted_iota(jnp.int32, lg.shape, 1)
    big = jnp.int32(4 * LANES)
    neg = jnp.float32(-jnp.inf)

    def first_max(vals):
        mx = jnp.max(vals, axis=-1, keepdims=True)
        idx = jnp.min(jnp.where(vals == mx, lane, big), axis=-1, keepdims=True)
        return mx, idx

    is_grp = (lane >= GRP_LANE0) & (lane < GRP_LANE0 + N_GROUPS)
    gl = jnp.where(is_grp, lg, neg)
    gmax, gidx = first_max(gl)
    p_top = 1.0 / jnp.sum(jnp.where(is_grp, jnp.exp(gl - gmax), 0.0), axis=-1, keepdims=True)
    lo = (gidx - GRP_LANE0) * EXPERTS_PER_GROUP
    in_grp = (lane >= lo) & (lane < lo + EXPERTS_PER_GROUP)
    el = jnp.where(in_grp, lg, neg)
    m1, i1 = first_max(el)
    m2, i2 = first_max(jnp.where(lane == i1, neg, el))
    r = jnp.exp(m2 - m1)
    g1 = p_top / (1.0 + r)
    g2 = p_top * r / (1.0 + r)

    hot1 = lane == i1
    hot2 = lane == i2
    onehot = jnp.where(hot1, 1.0, 0.0) + jnp.where(hot2, 1.0, 0.0)
    ri = lax.broadcasted_iota(jnp.int32, (tm, tm), 0)
    ci = lax.broadcasted_iota(jnp.int32, (tm, tm), 1)
    before = jnp.where(ci < ri, 1.0, 0.0).astype(BF16)
    rank = _dot(before, onehot.astype(BF16)) + carry[...]
    r1 = jnp.sum(jnp.where(hot1, rank, 0.0), axis=-1, keepdims=True)
    r2 = jnp.sum(jnp.where(hot2, rank, 0.0), axis=-1, keepdims=True)
    carry[...] = carry[...] + jnp.sum(onehot, axis=0, keepdims=True)
    cnt_ref[...] = carry[...]

    out = jnp.zeros(lg.shape, F32)
    for ln, val in ((SEL_E1, i1.astype(F32)), (SEL_E2, i2.astype(F32)), (SEL_R1, r1), (SEL_R2, r2),
                    (SEL_G1, g1), (SEL_G2, g2)):
        out = jnp.where(lane == ln, val, out)
    sel_ref[...] = out


def _route(logits):
    t = logits.shape[0]
    tm = min(512, t)
    return pl.pallas_call(
        _route_kernel,
        grid=(t // tm,),
        in_specs=[pl.BlockSpec((tm, LANES), lambda i: (i, 0))],
        out_specs=[pl.BlockSpec((tm, LANES), lambda i: (i, 0)),
                   pl.BlockSpec((1, LANES), lambda i: (0, 0))],
        out_shape=[jax.ShapeDtypeStruct((t, LANES), F32),
                   jax.ShapeDtypeStruct((1, LANES), F32)],
        scratch_shapes=[pltpu.VMEM((1, LANES), F32)],
        compiler_params=_cparams("arbitrary"),
        name="moe_route",
    )(logits)


def _dispatch_kernel(cnt_ref, e1_ref, e2_ref, r1_ref, r2_ref, rowa_ref, blk_ref, rows_ref, d1_ref, d2_ref, pstart,
                     used_ref, *, t, n_blocks):
    def seg(e, start):
        pstart[e] = start
        c = cnt_ref[e]
        nb = (c + MOE_ROWS - 1) // MOE_ROWS
        end = start + nb * MOE_ROWS

        def mark(p, carry):
            rowa_ref[p] = -1
            return carry
        lax.fori_loop(start + c, end, mark, 0)

        def blk(bi, carry):
            blk_ref[bi] = e
            rows_ref[bi] = jnp.minimum(start + c - bi * MOE_ROWS, MOE_ROWS)
            return carry
        lax.fori_loop(start // MOE_ROWS, end // MOE_ROWS, blk, 0)
        return end

    total = lax.fori_loop(0, N_EXPERTS, seg, 0)
    used = total // MOE_ROWS
    used_ref[0] = used
    last = blk_ref[jnp.maximum(used - 1, 0)]

    def tail(bi, carry):
        blk_ref[bi] = last
        rows_ref[bi] = 0

        def unused(p, c2):
            rowa_ref[bi * MOE_ROWS + p] = -1
            return c2
        lax.fori_loop(0, MOE_ROWS, unused, 0, unroll=16)
        return carry
    lax.fori_loop(used, n_blocks, tail, 0)

    def place(tok, carry):
        p1 = pstart[e1_ref[tok]] + r1_ref[tok]
        p2 = pstart[e2_ref[tok]] + r2_ref[tok]
        rowa_ref[p1] = 2 * tok
        rowa_ref[p2] = 2 * tok + 1
        d1_ref[tok] = p1
        d2_ref[tok] = p2
        return carry
    lax.fori_loop(0, t, place, 0, unroll=8)


def _dispatch(cnt, e1, e2, r1, r2, n_blocks):
    t = e1.shape[0]
    smem = pl.BlockSpec(memory_space=pltpu.SMEM)
    i32 = lambda n: jax.ShapeDtypeStruct((n,), jnp.int32)
    return pl.pallas_call(
        functools.partial(_dispatch_kernel, t=t, n_blocks=n_blocks),
        in_specs=[smem] * 5,
        out_specs=[smem] * 7,
        out_shape=[i32(n_blocks * MOE_ROWS), i32(n_blocks), i32(n_blocks), i32(t), i32(t), i32(N_EXPERTS), i32(1)],
        name="moe_dispatch",
    )(cnt, e1, e2, r1, r2)


def _scatter_in_kernel(d1_ref, d2_ref, cnt_ref, pstart_ref, used_ref, x_ref, xs_hbm, zbuf, sem, zsem, *, tm, n_blocks):
    i = pl.program_id(0)
    blk_rows = MOE_ROWS * SLAB

    def slab(j):
        return pl.ds(pl.multiple_of(j * SLAB, SLAB), SLAB)

    def block(bi):
        return pl.ds(pl.multiple_of(bi * blk_rows, blk_rows), blk_rows)

    def pad_range(e):
        c = cnt_ref[e]
        lo = pstart_ref[e] + c
        return lo, pstart_ref[e] + (c + MOE_ROWS - 1) // MOE_ROWS * MOE_ROWS

    @pl.when(i == 0)
    def _():
        zbuf[...] = jnp.zeros(zbuf.shape, F32)

        def zero_pads(e, carry):
            lo, hi = pad_range(e)

            def z(p, c2):
                pltpu.make_async_copy(zbuf.at[slab(0)], xs_hbm.at[slab(p)], zsem).start()
                return c2
            lax.fori_loop(lo, hi, z, 0)
            return carry
        lax.fori_loop(0, N_EXPERTS, zero_pads, 0)

        def zero_block(bi, carry):
            pltpu.make_async_copy(zbuf, xs_hbm.at[block(bi)], zsem).start()
            return carry
        lax.fori_loop(used_ref[0], n_blocks, zero_block, 0)

    def scatter(r, carry):
        tok = i * tm + r
        pltpu.make_async_copy(x_ref.at[slab(r)], xs_hbm.at[slab(d1_ref[tok])], sem).start(priority=0)
        pltpu.make_async_copy(x_ref.at[slab(r)], xs_hbm.at[slab(d2_ref[tok])], sem).start(priority=1)
        return carry
    lax.fori_loop(0, tm, scatter, 0, unroll=8)

    for _ in range(2):
        pltpu.make_async_copy(x_ref, xs_hbm.at[pl.ds(0, tm * SLAB)], sem).wait()

    @pl.when(i == pl.num_programs(0) - 1)
    def _():
        def wait_pads(e, carry):
            lo, hi = pad_range(e)
            n = hi - lo

            @pl.when(n > 0)
            def _():
                rows = pl.ds(0, pl.multiple_of(n * SLAB, SLAB))
                pltpu.make_async_copy(zbuf.at[rows], xs_hbm.at[rows], zsem).wait()
            return carry
        lax.fori_loop(0, N_EXPERTS, wait_pads, 0)

        def wait_block(bi, carry):
            pltpu.make_async_copy(zbuf, xs_hbm.at[block(0)], zsem).wait()
            return carry
        lax.fori_loop(used_ref[0], n_blocks, wait_block, 0)


def _scatter_in(x2r, d1, d2, cnt, pstart, used, n_blocks):
    t = x2r.shape[0] // SLAB
    tm = min(MOE_ROWS, t)
    grid_spec = pltpu.PrefetchScalarGridSpec(
        num_scalar_prefetch=5,
        grid=(t // tm,),
        in_specs=[pl.BlockSpec((tm * SLAB, LANES), lambda i, *_: (i, 0))],
        out_specs=pl.BlockSpec(memory_space=pl.ANY),
        scratch_shapes=[pltpu.VMEM((MOE_ROWS * SLAB, LANES), F32),
                        pltpu.SemaphoreType.DMA(()), pltpu.SemaphoreType.DMA(())],
    )
    return pl.pallas_call(
        functools.partial(_scatter_in_kernel, tm=tm, n_blocks=n_blocks),
        grid_spec=grid_spec,
        out_shape=jax.ShapeDtypeStruct((n_blocks * MOE_ROWS * SLAB, LANES), F32),
        compiler_params=_cparams("arbitrary"),
        name="moe_scatter_in",
    )(d1, d2, cnt, pstart, used, x2r)


def _expert_kernel(blk_ref, rowa_ref, rows_ref, used_ref, x_ref, wg_ref, wu_ref, wd_ref, out_hbm, ybuf, ssem):
    b = pl.program_id(0)
    n_blocks = pl.num_programs(0)
    slot = b % 2

    def slab(i):
        return pl.ds(pl.multiple_of(i * SLAB, SLAB), SLAB)

    def scatter_copy(blk_slot, r, dst):
        return pltpu.make_async_copy(ybuf.at[blk_slot, slab(r)], out_hbm.at[slab(dst)], ssem.at[blk_slot])

    def wait_scatters(blk, blk_slot):
        n = rows_ref[blk]

        @pl.when(n > 0)
        def _():
            rows = pl.ds(0, pl.multiple_of(n * SLAB, SLAB))
            pltpu.make_async_copy(ybuf.at[blk_slot, rows], out_hbm.at[rows], ssem.at[blk_slot]).wait()

    @pl.when(b >= 2)
    def _():
        wait_scatters(jnp.maximum(b - 2, 0), slot)

    n_rows = rows_ref[b]

    @pl.when(n_rows > 0)
    def _():
        xb = _load_rows(x_ref, MOE_ROWS, SLAB, 0, SLAB).astype(BF16)
        hg = _dot(xb, wg_ref[...].astype(BF16))
        hu = _dot(xb, wu_ref[...].astype(BF16))
        hb = (hg * _sigmoid(hg) * hu).astype(BF16)
        _store_rows(ybuf.at[slot], _dot(hb, wd_ref[...].astype(BF16)), SLAB)

        def issue_scatter(r, carry):
            scatter_copy(slot, r, rowa_ref[b * MOE_ROWS + r]).start()
            return carry
        lax.fori_loop(0, n_rows, issue_scatter, 0)

    @pl.when(b == n_blocks - 1)
    def _():
        @pl.when(b >= 1)
        def _():
            wait_scatters(jnp.maximum(b - 1, 0), 1 - slot)
        wait_scatters(b, slot)


def _experts(xs, t, row_a, blk_exp, blk_rows, used, w_gate_e, w_up_e, w_down_e):
    d = SLAB * LANES
    n_blocks = blk_exp.shape[0]
    de = w_gate_e.shape[2]
    weights = lambda b, blk, rowa, rows, nu: (blk[b], 0, 0)
    grid_spec = pltpu.PrefetchScalarGridSpec(
        num_scalar_prefetch=4,
        grid=(n_blocks,),
        in_specs=[pl.BlockSpec((MOE_ROWS * SLAB, LANES), lambda b, blk, rowa, rows, nu: (jnp.minimum(b, nu[0] - 1), 0)),
                  pl.BlockSpec((None, d, de), weights),
                  pl.BlockSpec((None, d, de), weights),
                  pl.BlockSpec((None, de, d), weights)],
        out_specs=pl.BlockSpec(memory_space=pl.ANY),
        scratch_shapes=[pltpu.VMEM((2, MOE_ROWS * SLAB, LANES), F32), pltpu.SemaphoreType.DMA((2,))],
    )
    return pl.pallas_call(
        _expert_kernel,
        grid_spec=grid_spec,
        out_shape=jax.ShapeDtypeStruct((2 * t * SLAB, LANES), F32),
        compiler_params=_cparams("arbitrary"),
        name="moe_experts",
    )(blk_exp, row_a, blk_rows, used, xs, w_gate_e, w_up_e, w_down_e)


def _combine_kernel(x_ref, y_ref, sel_ref, g_ref, b_ref, o_ref):
    tm = x_ref.shape[0]
    sel = sel_ref[...]
    g1 = sel[:, SEL_G1:SEL_G1 + 1]
    g2 = sel[:, SEL_G2:SEL_G2 + 1]
    y1 = _load_rows(y_ref, tm, 2 * SLAB, 0, SLAB)
    y2 = _load_rows(y_ref, tm, 2 * SLAB, SLAB, SLAB)
    y = DN_ALPHA * x_ref[...] + (g1 * y1 + g2 * y2)
    o_ref[...] = _layernorm(y, g_ref[...], b_ref[...])


def _combine_ln(x2, ys, sel, g, b):
    t, d = x2.shape
    tm = min(512, t)
    return pl.pallas_call(
        _combine_kernel,
        grid=(t // tm,),
        in_specs=[pl.BlockSpec((tm, d), lambda i: (i, 0)),
                  pl.BlockSpec((tm * 2 * SLAB, LANES), lambda i: (i, 0)),
                  pl.BlockSpec((tm, LANES), lambda i: (i, 0)),
                  pl.BlockSpec((1, d), lambda i: (0, 0)),
                  pl.BlockSpec((1, d), lambda i: (0, 0))],
        out_specs=pl.BlockSpec((tm, d), lambda i: (i, 0)),
        out_shape=jax.ShapeDtypeStruct((t, d), F32),
        compiler_params=_cparams("parallel"),
        name="combine_ln3",
    )(x2, ys, sel, g, b)


def _mixer(xb, positions, w_in, b_gate, mla_q_norm, w_uq, mla_kv_norm, w_ukv, w_o_mla,
           gdn_conv, gdn_a_log, gdn_dt_bias, gdn_norm, w_o_gdn):
    s, d = xb.shape
    nqkv = 2 * GDN_QK_HEADS * GDN_DK + GDN_V_HEADS * GDN_DV
    nz = GDN_V_HEADS * GDN_DV
    o0 = 0
    o1 = o0 + MLA_Q_LORA
    o2 = o1 + MLA_KV_LORA + MLA_ROPE
    o3 = o2 + nqkv
    o4 = o3 + nz
    o5 = o4 + GDN_V_HEADS
    o6 = o5 + GDN_V_HEADS

    def rot_cols(w):
        half = MLA_ROPE // 2
        return jnp.concatenate([-w[..., half:], w[..., :half]], axis=-1)

    w_q = w_in[:, o0:o1].astype(BF16)
    w_kpe = w_in[:, o1 + MLA_KV_LORA:o2]
    w_kv = jnp.concatenate([w_in[:, o1:o1 + MLA_KV_LORA], w_kpe, rot_cols(w_kpe)], axis=1).astype(BF16)
    w_ba = jnp.concatenate([w_in[:, o4:o6], jnp.zeros((d, LANES - 2 * GDN_V_HEADS), F32)], axis=1).astype(BF16)
    qd = _matmul(xb, w_q, BF16, 1024, MLA_Q_LORA)
    kvd = _matmul(xb, w_kv, BF16, 1024, w_kv.shape[1])
    qkv_c = _qkv_conv_proj(xb, w_in[:, o2:o3].astype(BF16), gdn_conv)
    hz = _matmul(xb, w_in[:, o3:o4].astype(BF16), BF16, 1024, 2048)
    hba = _matmul(xb, w_ba, F32, 512, LANES)
    hgate = _matmul(xb, w_in[:, o6:].astype(BF16), BF16, 1024, 2048)

    cs = _rope_table(positions)
    hq = MLA_NOPE + MLA_ROPE
    wq3 = w_uq.reshape(MLA_Q_LORA, MLA_HEADS, hq)
    pe = wq3[..., MLA_NOPE:]
    wq = jnp.concatenate([pe, rot_cols(pe), wq3[..., :MLA_NOPE]], axis=-1)
    wq = wq.reshape(MLA_Q_LORA, MLA_HEADS * MLA_QK_PAD).astype(BF16)
    wkv3 = w_ukv.reshape(MLA_KV_LORA, MLA_HEADS, MLA_NOPE + MLA_V)
    wk = wkv3[..., :MLA_NOPE].reshape(MLA_KV_LORA, MLA_HEADS * MLA_NOPE).astype(BF16)
    wv = wkv3[..., MLA_NOPE:].reshape(MLA_KV_LORA, MLA_HEADS * MLA_V).astype(BF16)
    q = _mla_q_proj(qd, mla_q_norm.reshape(1, -1), wq, cs)
    k, v = _mla_kv_proj(kvd, mla_kv_norm.reshape(1, -1), wk, wv, cs)
    o_mla = _mla_attention(q, k, v)

    o_gdn = _gdn_delta(qkv_c, hz, hba[:, GDN_V_HEADS:2 * GDN_V_HEADS], hba[:, :GDN_V_HEADS],
                       gdn_a_log, gdn_dt_bias, gdn_norm.reshape(1, -1))

    return _branch_merge(o_mla, o_gdn, w_o_mla.astype(BF16), w_o_gdn.astype(BF16), hgate, b_gate.reshape(1, -1))


def _moe(x2, x2r, logits, w_gate_e, w_up_e, w_down_e, ln_g, ln_b):
    t, d = x2.shape
    n_blocks = (2 * t) // MOE_ROWS + N_EXPERTS
    sel, cnt = _route(logits)
    as_i32 = lambda col: sel[:, col].astype(jnp.int32)
    cnt_i = cnt[0, :N_EXPERTS].astype(jnp.int32)
    row_a, blk_exp, blk_rows, d1, d2, pstart, used = _dispatch(cnt_i, as_i32(SEL_E1), as_i32(SEL_E2),
                                                               as_i32(SEL_R1), as_i32(SEL_R2), n_blocks)
    xs = _scatter_in(x2r, d1, d2, cnt_i, pstart, used, n_blocks)
    ys = _experts(xs, t, row_a, blk_exp, blk_rows, used, w_gate_e, w_up_e, w_down_e)
    return _combine_ln(x2, ys, sel, ln_g, ln_b)


def _layer(x, mem, positions, w_in, b_gate, mla_q_norm, w_uq, mla_kv_norm, w_ukv, w_o_mla,
           gdn_conv, gdn_a_log, gdn_dt_bias, gdn_norm, w_o_gdn, w_out, ln1_g, ln1_b,
           w_mq, w_mkv, w_mo, ln2_g, ln2_b, w_route_grp, b_route_grp, w_route_exp, b_route_exp,
           w_gate_e, w_up_e, w_down_e, ln3_g, ln3_b):
    d = x.shape[1]
    row = lambda p: p.reshape(1, -1)
    mixed = _mixer(x.astype(BF16), positions, w_in, b_gate, mla_q_norm, w_uq, mla_kv_norm, w_ukv, w_o_mla,
                   gdn_conv, gdn_a_log, gdn_dt_bias, gdn_norm, w_o_gdn)
    x1 = _out_ln(mixed, w_out.astype(BF16), x, row(ln1_g), row(ln1_b))

    kvm = _matmul(mem.astype(BF16), w_mkv.astype(BF16), BF16, 256, 512)
    pad = LANES - N_EXPERTS - N_GROUPS
    w_r = jnp.concatenate([w_route_exp, w_route_grp, jnp.zeros((d, pad), F32)], axis=1)
    b_r = jnp.concatenate([b_route_exp, b_route_grp, jnp.zeros((pad,), F32)]).reshape(1, LANES)
    x2, x2r, logits = _mem_attn_ln(x1, w_mq.astype(BF16), kvm, w_mo.astype(BF16), row(ln2_g), row(ln2_b), w_r, b_r)

    return _moe(x2, x2r, logits, w_gate_e, w_up_e, w_down_e, row(ln3_g), row(ln3_b))


def kernel(x, mem, positions, w_in, b_gate, mla_q_norm, w_uq, mla_kv_norm, w_ukv, w_o_mla, gdn_conv, gdn_a_log,
           gdn_dt_bias, gdn_norm, w_o_gdn, w_out, ln1_g, ln1_b, w_mq, w_mkv, w_mo, ln2_g, ln2_b, w_route_grp,
           b_route_grp, w_route_exp, b_route_exp, w_gate_e, w_up_e, w_down_e, ln3_g, ln3_b):
    outs = []
    for bi in range(x.shape[0]):
        h = x[bi]
        for l in range(w_in.shape[0]):
            h = _layer(h, mem[bi], positions[bi], w_in[l], b_gate[l], mla_q_norm[l], w_uq[l], mla_kv_norm[l],
                       w_ukv[l], w_o_mla[l], gdn_conv[l], gdn_a_log[l], gdn_dt_bias[l], gdn_norm[l], w_o_gdn[l],
                       w_out[l], ln1_g[l], ln1_b[l], w_mq[l], w_mkv[l], w_mo[l], ln2_g[l], ln2_b[l],
                       w_route_grp[l], b_route_grp[l], w_route_exp[l], b_route_exp[l], w_gate_e[l], w_up_e[l],
                       w_down_e[l], ln3_g[l], ln3_b[l])
        outs.append(h)
    return jnp.stack(outs, axis=0)
```
